```python
import math
import jax, jax.numpy as jnp
from jax import lax
import numpy as np

D_MODEL = 1024
BATCH = 8
SEQ = 2048
DEPTH = 4

GRID_W = 64
CTX_LEN = 256
MIX = D_MODEL
N_MIXERS = 4
W_GROUP = MIX // N_MIXERS
HEAD_DIM = 64
N_HEADS = W_GROUP // HEAD_DIM
ATT_KV_HEADS = 2
ATT_GROUPS = N_HEADS // ATT_KV_HEADS
WINDOW = 128
ATT_BLOCK = 128
CHUNK = 64
GLA_RANK = 16
GLA_NORMALIZER = 16.0
HYENA_BANDS = 16
HYENA_EMB = 1 + 2 * HYENA_BANDS
HYENA_FFN = 64
HYENA_MIN_DECAY = math.log(1e-2) / 1.5
HYENA_MAX_DECAY = math.log(1e-2) / 0.3
SHORT_CONV = 3
ROPE_BASE = 10000.0
ROPE_AXIS_FREQS = HEAD_DIM // 4
EPS = 1e-6

MLSTM_COLS = (W_GROUP, W_GROUP, W_GROUP, W_GROUP, W_GROUP, 4 * N_HEADS)
HYENA_COLS = (W_GROUP, W_GROUP, W_GROUP, W_GROUP)
GLA_COLS = (W_GROUP, W_GROUP, W_GROUP, W_GROUP, GLA_RANK, GLA_RANK)
ATT_COLS = (W_GROUP, ATT_KV_HEADS * HEAD_DIM, ATT_KV_HEADS * HEAD_DIM, W_GROUP)
GROUP_COLS = (sum(MLSTM_COLS), sum(HYENA_COLS), sum(GLA_COLS), sum(ATT_COLS))
N_IN = sum(GROUP_COLS)

kernel_name = "hybrid_parallel_heads_flow_backbone"

F32 = jnp.float32


def split_cols(p, sizes):
    out, start = [], 0
    for s in sizes:
        out.append(p[..., start:start + s])
        start += s
    return out


def rms_norm(x, g):
    x32 = x.astype(F32)
    y = x32 * lax.rsqrt(jnp.mean(x32 * x32, axis=-1, keepdims=True) + EPS)
    return (y * g.astype(F32)).astype(x.dtype)


def head_rms_norm(h, g):
    b_, l_, _ = h.shape
    hh = h.reshape(b_, l_, N_HEADS, HEAD_DIM)
    hh = hh * lax.rsqrt(jnp.mean(hh * hh, axis=-1, keepdims=True) + EPS)
    return hh.reshape(b_, l_, W_GROUP) * g.astype(F32)


def to_heads(a):
    b_, l_, _ = a.shape
    return a.reshape(b_, l_, -1, HEAD_DIM).transpose(0, 2, 1, 3)


def from_heads(h):
    b_, h_, l_, d_ = h.shape
    return h.transpose(0, 2, 1, 3).reshape(b_, l_, h_ * d_)


def to_chunks(a):
    nc = a.shape[2] // CHUNK
    return jnp.moveaxis(a.reshape(a.shape[:2] + (nc, CHUNK) + a.shape[3:]), 2, 0)


def from_chunks(h):
    h = jnp.moveaxis(h, 0, 2)
    return h.reshape(h.shape[:2] + (h.shape[2] * h.shape[3],) + h.shape[4:])


def mlstm_chunks(q, k, v, i_pre, f_pre, state):
    tri = jnp.tril(jnp.ones((CHUNK, CHUNK), bool))

    def step(carry, inp):
        c_prev, n_prev, m_prev = carry
        qb, kb, vb, ib, fb = inp
        b = jnp.cumsum(jax.nn.log_sigmoid(fb), axis=-1)
        d = jnp.where(tri, b[..., :, None] - b[..., None, :] + ib[..., None, :], -jnp.inf)
        inter = b + m_prev[..., None]
        m = jnp.maximum(inter, jnp.max(d, axis=-1))
        w_intra = jnp.exp(d - m[..., None])
        w_inter = jnp.exp(inter - m)
        s = jnp.einsum('bhtd,bhsd->bhts', qb, kb) * w_intra
        num = w_inter[..., None] * jnp.einsum('bhtd,bhde->bhte', qb, c_prev) + jnp.einsum('bhts,bhse->bhte', s, vb)
        den = w_inter * jnp.einsum('bhtd,bhd->bht', qb, n_prev) + jnp.sum(s, axis=-1)
        h = num / jnp.maximum(jnp.abs(den), jnp.exp(-m))[..., None]
        b_end = b[..., -1]
        g = b_end[..., None] - b + ib
        m_new = jnp.maximum(b_end + m_prev, jnp.max(g, axis=-1))
        a_prev = jnp.exp(b_end + m_prev - m_new)
        w_s = jnp.exp(g - m_new[..., None])
        c_new = a_prev[..., None, None] * c_prev + jnp.einsum('bhs,bhsd,bhse->bhde', w_s, kb, vb)
        n_new = a_prev[..., None] * n_prev + jnp.einsum('bhs,bhsd->bhd', w_s, kb)
        return (c_new, n_new, m_new), h

    state, h = lax.scan(step, state, tuple(to_chunks(a) for a in (q, k, v, i_pre, f_pre)))
    return from_chunks(h), state


def gla_chunks(q, k, v, log_a, state):
    tri = jnp.tril(jnp.ones((CHUNK, CHUNK), bool))[:, :, None]

    def step(s_prev, inp):
        qb, kb, vb, ab = inp
        b = jnp.cumsum(ab, axis=2)
        o_inter = jnp.einsum('bhtd,bhde->bhte', qb * jnp.exp(b), s_prev)
        decay = jnp.exp(jnp.where(tri, b[:, :, :, None, :] - b[:, :, None, :, :], -jnp.inf))
        att = jnp.einsum('bhtd,bhsd,bhtsd->bhts', qb, kb, decay)
        o = o_inter + jnp.einsum('bhts,bhse->bhte', att, vb)
        b_end = b[:, :, -1]
        s_new = jnp.exp(b_end)[..., None] * s_prev + jnp.einsum(
            'bhsd,bhse->bhde', kb * jnp.exp(b_end[:, :, None, :] - b), vb)
        return s_new, o

    state, o = lax.scan(step, state, tuple(to_chunks(a) for a in (q, k, v, log_a)))
    return from_chunks(o), state


def run_direction(scan_fn, ctx_in, lat_in, state0, reverse):
    if reverse:
        ctx_in = tuple(jnp.flip(a, axis=2) for a in ctx_in)
        lat_in = tuple(jnp.flip(a, axis=2) for a in lat_in)
    h_c, state = scan_fn(*ctx_in, state0)
    h_x, _ = scan_fn(*lat_in, state)
    if reverse:
        h_c, h_x = jnp.flip(h_c, axis=2), jnp.flip(h_x, axis=2)
    return h_c, h_x


def mlstm_branch(pc, px, gate_b, norm_g):
    def prep(p):
        q, k, v, o, gate, gts = split_cols(p, MLSTM_COLS)
        b_, l_, _ = p.shape
        gts = gts.astype(F32).reshape(b_, l_, 4, N_HEADS) + gate_b.astype(F32)
        gts = jnp.transpose(gts, (2, 0, 3, 1))
        qkv = (to_heads(q).astype(F32), to_heads(k).astype(F32) * HEAD_DIM ** -0.5, to_heads(v).astype(F32))
        return qkv, gts, o, gate

    qkv_c, gts_c, o_c, gate_c = prep(pc)
    qkv_x, gts_x, o_x, gate_x = prep(px)
    b_ = px.shape[0]
    state0 = (jnp.zeros((b_, N_HEADS, HEAD_DIM, HEAD_DIM), F32), jnp.zeros((b_, N_HEADS, HEAD_DIM), F32),
              jnp.zeros((b_, N_HEADS), F32))
    hc_f, hx_f = run_direction(mlstm_chunks, qkv_c + (gts_c[0], gts_c[1]), qkv_x + (gts_x[0], gts_x[1]), state0, False)
    hc_b, hx_b = run_direction(mlstm_chunks, qkv_c + (gts_c[2], gts_c[3]), qkv_x + (gts_x[2], gts_x[3]), state0, True)

    def finish(h, o, gate, dtype):
        h = from_heads(h) * jax.nn.sigmoid(o.astype(F32))
        return (head_rms_norm(h, norm_g) * jax.nn.silu(gate.astype(F32))).astype(dtype)

    return finish(hc_f + hc_b, o_c, gate_c, pc.dtype), finish(hx_f + hx_b, o_x, gate_x, px.dtype)


def gla_branch(pc, px, w_alpha, b_alpha, norm_g):
    def prep(p):
        q, k, v, gate, r_f, r_b = split_cols(p, GLA_COLS)
        qkv = (to_heads(q).astype(F32) * HEAD_DIM ** -0.5, to_heads(k).astype(F32), to_heads(v).astype(F32))
        log_a = [to_heads(jax.nn.log_sigmoid((r @ w_alpha[d] + b_alpha[d]).astype(F32)) / GLA_NORMALIZER)
                 for d, r in enumerate((r_f, r_b))]
        return qkv, log_a, gate

    qkv_c, la_c, gate_c = prep(pc)
    qkv_x, la_x, gate_x = prep(px)
    state0 = jnp.zeros((px.shape[0], N_HEADS, HEAD_DIM, HEAD_DIM), F32)
    hc_f, hx_f = run_direction(gla_chunks, qkv_c + (la_c[0],), qkv_x + (la_x[0],), state0, False)
    hc_b, hx_b = run_direction(gla_chunks, qkv_c + (la_c[1],), qkv_x + (la_x[1],), state0, True)

    def finish(h, gate, dtype):
        return (head_rms_norm(from_heads(h), norm_g) * jax.nn.silu(gate.astype(F32))).astype(dtype)

    return finish(hc_f + hc_b, gate_c, pc.dtype), finish(hx_f + hx_b, gate_x, px.dtype)


def short_conv(u, w, b):
    l_ = u.shape[1]
    up = jnp.pad(u, ((0, 0), (SHORT_CONV // 2, SHORT_CONV // 2), (0, 0)))
    return sum(w[j] * up[:, j:j + l_] for j in range(SHORT_CONV)) + b


def hyena_filter_spectrum(l_, w1, b1, w2, b2, w3, b3, freq):
    pos = jnp.arange(l_, dtype=F32)
    t = pos / max(l_ - 1, 1)
    w = 2.0 * math.pi * pos / l_
    f = jnp.linspace(1e-4, HYENA_BANDS - 1, HYENA_BANDS, dtype=F32)
    z = jnp.concatenate([t[:, None], jnp.cos(w[:, None] * f), -jnp.sin(w[:, None] * f)], axis=-1)
    h = jnp.sin(freq[0].astype(F32) * (z @ w1.astype(F32) + b1.astype(F32)))
    h = jnp.sin(freq[1].astype(F32) * (h @ w2.astype(F32) + b2.astype(F32)))
    h = (h @ w3.astype(F32) + b3.astype(F32)).reshape(l_, 2, W_GROUP)
    deltas = jnp.abs(jnp.linspace(HYENA_MIN_DECAY, HYENA_MAX_DECAY, W_GROUP, dtype=F32))
    h = h * jnp.exp(-t[:, None, None] * deltas)
    kernel = jnp.concatenate([h[:, 0], jnp.zeros((1, W_GROUP), F32), h[:0:-1, 1]], axis=0)
    return jnp.fft.rfft(kernel, n=2 * l_, axis=0)


def fftconv(u, k_f, d):
    l_ = u.shape[1]
    u32 = u.astype(F32)
    y = jnp.fft.irfft(jnp.fft.rfft(u32, n=2 * l_, axis=1) * k_f, n=2 * l_, axis=1)[:, :l_]
    return y + u32 * d.astype(F32)


def hyena_branch(p, conv_w, conv_b, w1, b1, w2, b2, w3, b3, freq, d):
    v, x1, x0, gate = split_cols(p, HYENA_COLS)
    u = short_conv(jnp.concatenate([v, x1, x0], axis=-1), conv_w, conv_b)
    v, x1, x0 = split_cols(u, (W_GROUP, W_GROUP, W_GROUP))
    k_f = hyena_filter_spectrum(p.shape[1], w1, b1, w2, b2, w3, b3, freq)
    y = x0.astype(F32) * fftconv(v * x1, k_f, d)
    return (y * jax.nn.silu(gate.astype(F32))).astype(p.dtype)


def rope_2d(x):
    l_ = x.shape[1]
    rows = l_ // GRID_W
    row = jnp.repeat(jnp.arange(rows, dtype=F32), GRID_W)
    col = jnp.tile(jnp.arange(GRID_W, dtype=F32), rows)
    inv = ROPE_BASE ** (-jnp.arange(ROPE_AXIS_FREQS, dtype=F32) / ROPE_AXIS_FREQS)
    ang = jnp.concatenate([row[:, None] * inv, col[:, None] * inv], axis=-1)[:, None, :]
    cos, sin = jnp.cos(ang), jnp.sin(ang)
    x32 = x.astype(F32)
    x1, x2 = x32[..., :HEAD_DIM // 2], x32[..., HEAD_DIM // 2:]
    return jnp.concatenate([x1 * cos - x2 * sin, x2 * cos + x1 * sin], axis=-1).astype(x.dtype)


def attn_branch(pc, px, sink, need_ctx_out):
    b_, l_, _ = px.shape
    lc = pc.shape[1]
    q, k, v, gate = split_cols(px, ATT_COLS)
    q_c, k_c, v_c, gate_c = split_cols(pc, ATT_COLS)
    scale = HEAD_DIM ** -0.5
    q = rope_2d(q.reshape(b_, l_, N_HEADS, HEAD_DIM))
    k = rope_2d(k.reshape(b_, l_, ATT_KV_HEADS, HEAD_DIM))
    v = v.reshape(b_, l_, ATT_KV_HEADS, HEAD_DIM)
    kc = k_c.reshape(b_, lc, ATT_KV_HEADS, HEAD_DIM)
    vc = v_c.reshape(b_, lc, ATT_KV_HEADS, HEAD_DIM).astype(F32)
    sink_g = sink.astype(F32).reshape(ATT_KV_HEADS, ATT_GROUPS)

    nb = l_ // ATT_BLOCK
    qb = q.reshape(b_, nb, ATT_BLOCK, ATT_KV_HEADS, ATT_GROUPS, HEAD_DIM)

    def band(a):
        ap = jnp.pad(a, ((0, 0), (WINDOW, WINDOW), (0, 0), (0, 0))).reshape(b_, nb + 2, ATT_BLOCK, ATT_KV_HEADS, HEAD_DIM)
        return jnp.concatenate([ap[:, :-2], ap[:, 1:-1], ap[:, 2:]], axis=2)

    kb, vb = band(k), band(v).astype(F32)
    qi = jnp.arange(ATT_BLOCK)
    si = jnp.arange(3 * ATT_BLOCK)
    blk = jnp.arange(nb)
    rel = si[None, :] - ATT_BLOCK - qi[:, None]
    key_pos = blk[:, None] * ATT_BLOCK - ATT_BLOCK + si[None, :]
    mask = (jnp.abs(rel) <= WINDOW)[None] & ((key_pos >= 0) & (key_pos < l_))[:, None, :]
    s_loc = jnp.einsum('bnqkgd,bnskd->bnkgqs', qb, kb).astype(F32) * scale
    s_loc = jnp.where(mask[None, :, None, None], s_loc, -jnp.inf)
    s_ctx = jnp.einsum('bnqkgd,bckd->bnkgqc', qb, kc).astype(F32) * scale
    s_sink = jnp.broadcast_to(sink_g[None, None, :, :, None, None], s_loc.shape[:-1] + (1,))
    p = jax.nn.softmax(jnp.concatenate([s_loc, s_ctx, s_sink], axis=-1), axis=-1)
    o = jnp.einsum('bnkgqs,bnskd->bnqkgd', p[..., :3 * ATT_BLOCK], vb) + jnp.einsum(
        'bnkgqc,bckd->bnqkgd', p[..., 3 * ATT_BLOCK:3 * ATT_BLOCK + lc], vc)
    o = o.reshape(b_, l_, W_GROUP) * jax.nn.silu(gate.astype(F32))
    out_x = o.astype(px.dtype)

    out_c = None
    if need_ctx_out:
        qc = q_c.reshape(b_, lc, ATT_KV_HEADS, ATT_GROUPS, HEAD_DIM)
        s = jnp.einsum('bqkgd,bckd->bkgqc', qc, kc).astype(F32) * scale
        s_sink_c = jnp.broadcast_to(sink_g[None, :, :, None, None], s.shape[:-1] + (1,))
        pcx = jax.nn.softmax(jnp.concatenate([s, s_sink_c], axis=-1), axis=-1)
        oc = jnp.einsum('bkgqc,bckd->bqkgd', pcx[..., :lc], vc).reshape(b_, lc, W_GROUP)
        out_c = (oc * jax.nn.silu(gate_c.astype(F32))).astype(pc.dtype)
    return out_c, out_x


def setup_inputs(seed: int = 0) -> dict:
    key = jax.random.key(seed)
    ks = jax.random.split(key, 32)

    def nrm(k, shape, s):
        return jax.random.normal(k, shape, jnp.float32) * s

    f_bias = jnp.linspace(3.0, 6.0, N_HEADS, dtype=jnp.float32)
    mlstm_gate_b = jnp.stack([nrm(ks[9], (DEPTH, N_HEADS), 0.1),
                              f_bias + nrm(ks[10], (DEPTH, N_HEADS), 0.1),
                              nrm(ks[11], (DEPTH, N_HEADS), 0.1),
                              f_bias + nrm(ks[12], (DEPTH, N_HEADS), 0.1)], axis=1)
    return {
        "x": nrm(ks[0], (BATCH, SEQ, D_MODEL), 1.0),
        "c": nrm(ks[1], (BATCH, D_MODEL), 1.0),
        "ctx": nrm(ks[2], (BATCH, CTX_LEN, D_MODEL), 1.0),
        "c_ctx": nrm(ks[3], (D_MODEL,), 1.0),
        "w_ada": nrm(ks[4], (DEPTH, D_MODEL, 3 * D_MODEL), 0.5 * D_MODEL ** -0.5),
        "b_ada": nrm(ks[5], (DEPTH, 3 * D_MODEL), 0.02),
        "g_pre": 1.0 + nrm(ks[6], (DEPTH, D_MODEL), 0.05),
        "g_post": 1.0 + nrm(ks[7], (DEPTH, D_MODEL), 0.05),
        "w_in": nrm(ks[8], (DEPTH, D_MODEL, N_IN), D_MODEL ** -0.5),
        "mlstm_gate_b": mlstm_gate_b,
        "mlstm_norm_g": 1.0 + nrm(ks[13], (DEPTH, W_GROUP), 0.05),
        "hyena_conv_w": nrm(ks[14], (DEPTH, SHORT_CONV, 3 * W_GROUP), 0.5),
        "hyena_conv_b": nrm(ks[15], (DEPTH, 3 * W_GROUP), 0.02),
        "hyena_w1": nrm(ks[16], (DEPTH, HYENA_EMB, HYENA_FFN), HYENA_EMB ** -0.5),
        "hyena_b1": nrm(ks[17], (DEPTH, HYENA_FFN), 0.1),
        "hyena_w2": nrm(ks[18], (DEPTH, HYENA_FFN, HYENA_FFN), HYENA_FFN ** -0.5),
        "hyena_b2": nrm(ks[19], (DEPTH, HYENA_FFN), 0.1),
        "hyena_w3": nrm(ks[20], (DEPTH, HYENA_FFN, 2 * W_GROUP), 0.02),
        "hyena_b3": nrm(ks[21], (DEPTH, 2 * W_GROUP), 0.01),
        "hyena_freq": 1.0 + nrm(ks[22], (DEPTH, 2, HYENA_FFN), 0.1),
        "hyena_d": nrm(ks[23], (DEPTH, W_GROUP), 0.5),
        "gla_w_alpha": nrm(ks[24], (DEPTH, 2, GLA_RANK, W_GROUP), GLA_RANK ** -0.5),
        "gla_b_alpha": nrm(ks[25], (DEPTH, 2, W_GROUP), 0.1),
        "gla_norm_g": 1.0 + nrm(ks[26], (DEPTH, W_GROUP), 0.05),
        "attn_sink": nrm(ks[27], (DEPTH, N_HEADS), 0.5),
        "w_out": nrm(ks[28], (DEPTH, MIX, D_MODEL), MIX ** -0.5),
    }


def reference(x, c, ctx, c_ctx, w_ada, b_ada, g_pre, g_post, w_in, mlstm_gate_b, mlstm_norm_g,
              hyena_conv_w, hyena_conv_b, hyena_w1, hyena_b1, hyena_w2, hyena_b2, hyena_w3, hyena_b3,
              hyena_freq, hyena_d, gla_w_alpha, gla_b_alpha, gla_norm_g, attn_sink, w_out):
    for l in range(DEPTH):
        need_ctx_out = l < DEPTH - 1
        mod_x = jax.nn.silu(c) @ w_ada[l] + b_ada[l]
        mod_c = jax.nn.silu(c_ctx) @ w_ada[l] + b_ada[l]
        sh_x, sc_x, gt_x = split_cols(mod_x[:, None, :], (D_MODEL, D_MODEL, D_MODEL))
        sh_c, sc_c, gt_c = split_cols(mod_c, (D_MODEL, D_MODEL, D_MODEL))
        hx = rms_norm(x, g_pre[l]) * (1.0 + sc_x) + sh_x
        hc = rms_norm(ctx, g_pre[l]) * (1.0 + sc_c) + sh_c
        px_a, px_h, px_g, px_d = split_cols(hx @ w_in[l], GROUP_COLS)
        pc_a, pc_h, pc_g, pc_d = split_cols(hc @ w_in[l], GROUP_COLS)

        a_c, a_x = mlstm_branch(pc_a, px_a, mlstm_gate_b[l], mlstm_norm_g[l])
        hy_args = (hyena_conv_w[l], hyena_conv_b[l], hyena_w1[l], hyena_b1[l], hyena_w2[l], hyena_b2[l],
                   hyena_w3[l], hyena_b3[l], hyena_freq[l], hyena_d[l])
        h_x = hyena_branch(px_h, *hy_args)
        g_c, g_x = gla_branch(pc_g, px_g, gla_w_alpha[l], gla_b_alpha[l], gla_norm_g[l])
        d_c, d_x = attn_branch(pc_d, px_d, attn_sink[l], need_ctx_out)

        y_x = jnp.concatenate([a_x, h_x, g_x, d_x], axis=-1) @ w_out[l]
        x = x + gt_x * rms_norm(y_x, g_post[l])
        if need_ctx_out:
            h_c = hyena_branch(pc_h, *hy_args)
            y_c = jnp.concatenate([a_c, h_c, g_c, d_c], axis=-1) @ w_out[l]
            ctx = ctx + gt_c * rms_norm(y_c, g_post[l])
    return x
```

```python
import functools
import math

import numpy as np
import jax
import jax.numpy as jnp
from jax import lax
from jax.experimental import pallas as pl
from jax.experimental.pallas import tpu as pltpu

F32 = jnp.float32
BF16 = jnp.bfloat16

D_MODEL = 1024
DEPTH = 4
GRID_W = 64
W_GROUP = 256
HEAD_DIM = 64
N_HEADS = 4
ATT_KV_HEADS = 2
WINDOW = 128
ATT_BLOCK = 128
GLA_RANK = 16
GLA_NORMALIZER = 16.0
HYENA_BANDS = 16
HYENA_EMB = 1 + 2 * HYENA_BANDS
HYENA_FFN = 64
HYENA_MIN_DECAY = math.log(1e-2) / 1.5
HYENA_MAX_DECAY = math.log(1e-2) / 0.3
ROPE_BASE = 10000.0
ROPE_AXIS_FREQS = HEAD_DIM // 4
EPS = 1e-6

N_WIDE_BLOCKS = 17
SMALL_COL0 = N_WIDE_BLOCKS * W_GROUP
N_PROJ = SMALL_COL0 + 128
SMALL_BLOCK = SMALL_COL0 // 128
(MQ, MK, MV, MO, MG, HV, HX1, HX0, HG, GQ, GK, GV, GG, AQ, AK, AV, AG) = range(N_WIDE_BLOCKS)

ROW_TILE = 256
MLSTM_CHUNK = 256
GLA_CHUNK = 128
FFT_N2 = 64
FFT_PITCH = 72
VMEM_LIMIT = 56 * 1024 * 1024


def _dot(a, b):
    return jnp.dot(a, b, preferred_element_type=F32)


def _dot_nt(a, b):
    return lax.dot_general(a, b, (((1,), (1,)), ((), ())), preferred_element_type=F32)


def _dot_tn(a, b):
    return lax.dot_general(a, b, (((0,), (0,)), ((), ())), preferred_element_type=F32)


def _split2(x):
    hi = x.astype(BF16)
    lo = (x - hi.astype(F32)).astype(BF16)
    return hi, lo


def _split3(x):
    h1 = x.astype(BF16)
    r1 = x - h1.astype(F32)
    h2 = r1.astype(BF16)
    h3 = (r1 - h2.astype(F32)).astype(BF16)
    return h1, h2, h3


def _dot_const(ch, cl, x):
    xh, xl = _split2(x)
    return _dot(ch, xh) + _dot(ch, xl) + _dot(cl, xh)


def _dot_f32(a, b):
    ah, al = _split2(a)
    bh, bl = _split2(b)
    return _dot(ah, bh) + _dot(ah, bl) + _dot(al, bh)


def _dot_exact_lhs(c, x):
    x1, x2, x3 = _split3(x)
    return _dot(c, x1) + _dot(c, x2) + _dot(c, x3)


def _dot_exact_rhs(x, c):
    x1, x2, x3 = _split3(x)
    return _dot(x1, c) + _dot(x2, c) + _dot(x3, c)


def _log_sigmoid(x):
    return jnp.minimum(x, 0.0) - jnp.log(1.0 + jnp.exp(-jnp.abs(x)))


def _silu(x):
    return x * jax.nn.sigmoid(x)


def _head_masks():
    lane = lax.broadcasted_iota(jnp.int32, (1, W_GROUP), 1) // HEAD_DIM
    return [(lane == h).astype(F32) for h in range(N_HEADS)]


def _block_diag_mask():
    r = lax.broadcasted_iota(jnp.int32, (W_GROUP, W_GROUP), 0) // HEAD_DIM
    c = lax.broadcasted_iota(jnp.int32, (W_GROUP, W_GROUP), 1) // HEAD_DIM
    return (r == c).astype(F32)


def _spread(cols, masks):
    out = cols[0] * masks[0]
    for h in range(1, N_HEADS):
        out = out + cols[h] * masks[h]
    return out


def _head_rms_norm(h, bd_bf16, g_row):
    sq = h * h
    sh, sl = _split2(sq)
    ms = (_dot(sh, bd_bf16) + _dot(sl, bd_bf16)) * (1.0 / HEAD_DIM)
    return h * lax.rsqrt(ms + EPS) * g_row


def _mod_kernel(c_ref, w_ref, b_ref, o_ref):
    s = _silu(c_ref[...])
    o_ref[0] = _dot_f32(s, w_ref[0]) + b_ref[0]


def _modulation(cc, w_ada, b_ada):
    nblk = 3 * D_MODEL // 1024
    return pl.pallas_call(
        _mod_kernel,
        grid=(DEPTH, nblk),
        in_specs=[
            pl.BlockSpec((16, D_MODEL), lambda l, j: (0, 0)),
            pl.BlockSpec((1, D_MODEL, 1024), lambda l, j: (l, 0, j)),
            pl.BlockSpec((1, 1, 1024), lambda l, j: (l, 0, j)),
        ],
        out_specs=pl.BlockSpec((1, 16, 1024), lambda l, j: (l, 0, j)),
        out_shape=jax.ShapeDtypeStruct((DEPTH, 16, 3 * D_MODEL), F32),
        compiler_params=pltpu.CompilerParams(vmem_limit_bytes=VMEM_LIMIT),
        name="adaln_modulation",
    )(cc, w_ada, b_ada.reshape(DEPTH, 1, 3 * D_MODEL))


def _inproj_kernel(x_ref, mod_ref, g_ref, w_ref, wst_ref, p_ref, gt_ref):
    x = x_ref[0]
    ms = jnp.mean(x * x, axis=-1, keepdims=True)
    y = x * lax.rsqrt(ms + EPS) * g_ref[...]
    sh = mod_ref[0, :, 0:D_MODEL]
    sc = mod_ref[0, :, D_MODEL:2 * D_MODEL]
    hb = (y * (1.0 + sc) + sh).astype(BF16)
    p_ref[0] = _dot(hb, w_ref[...])
    gt_ref[0] = _dot_nt(wst_ref[...], hb)


def _inproj(xs, mod_l, g_pre, w_p, w_st, n_ctx_tiles):
    b_, lt, _ = xs.shape
    nt = lt // ROW_TILE
    return pl.pallas_call(
        _inproj_kernel,
        grid=(b_, nt),
        in_specs=[
            pl.BlockSpec((1, ROW_TILE, D_MODEL), lambda b, i: (b, i, 0)),
            pl.BlockSpec((1, 1, 3 * D_MODEL), lambda b, i: (2 * b + (i >= n_ctx_tiles).astype(jnp.int32), 0, 0)),
            pl.BlockSpec((1, D_MODEL), lambda b, i: (0, 0)),
            pl.BlockSpec((D_MODEL, N_PROJ), lambda b, i: (0, 0)),
            pl.BlockSpec((16, D_MODEL), lambda b, i: (0, 0)),
        ],
        out_specs=[
            pl.BlockSpec((1, ROW_TILE, N_PROJ), lambda b, i: (b, i, 0)),
            pl.BlockSpec((1, 16, ROW_TILE), lambda b, i: (b, 0, i)),
        ],
        out_shape=[
            jax.ShapeDtypeStruct((b_, lt, N_PROJ), F32),
            jax.ShapeDtypeStruct((b_, 16, lt), F32),
        ],
        compiler_params=pltpu.CompilerParams(vmem_limit_bytes=VMEM_LIMIT),
        name="inproj",
    )(xs, mod_l, g_pre.reshape(1, D_MODEL), w_p, w_st)


def _outproj_kernel(a_ref, h_ref, g_ref, d_ref, w_ref, gp_ref, mod_ref, x_ref, o_ref):
    y = _dot(a_ref[0].astype(BF16), w_ref[0:W_GROUP, :])
    y = y + _dot(h_ref[0].astype(BF16), w_ref[W_GROUP:2 * W_GROUP, :])
    y = y + _dot(g_ref[0].astype(BF16), w_ref[2 * W_GROUP:3 * W_GROUP, :])
    y = y + _dot(d_ref[0].astype(BF16), w_ref[3 * W_GROUP:4 * W_GROUP, :])
    ms = jnp.mean(y * y, axis=-1, keepdims=True)
    yn = y * lax.rsqrt(ms + EPS) * gp_ref[...]
    gt = mod_ref[0, :, 2 * D_MODEL:3 * D_MODEL]
    o_ref[0] = x_ref[0] + gt * yn


def _outproj(mix, w_o, g_post, mod_l, xs, n_ctx_tiles, skip_ctx):
    b_, lt, _ = xs.shape
    nt = lt // ROW_TILE
    t0 = n_ctx_tiles if skip_ctx else 0
    rows = lambda b, i: (b, i + t0, 0)
    return pl.pallas_call(
        _outproj_kernel,
        grid=(b_, nt - t0),
        in_specs=[pl.BlockSpec((1, ROW_TILE, W_GROUP), rows)] * 4 + [
            pl.BlockSpec((D_MODEL, D_MODEL), lambda b, i: (0, 0)),
            pl.BlockSpec((1, D_MODEL), lambda b, i: (0, 0)),
            pl.BlockSpec((1, 1, 3 * D_MODEL),
                         lambda b, i: (2 * b + (i + t0 >= n_ctx_tiles).astype(jnp.int32), 0, 0)),
            pl.BlockSpec((1, ROW_TILE, D_MODEL), rows),
        ],
        out_specs=pl.BlockSpec((1, ROW_TILE, D_MODEL), lambda b, i: (b, i, 0)),
        out_shape=jax.ShapeDtypeStruct((b_, lt - t0 * ROW_TILE, D_MODEL), F32),
        compiler_params=pltpu.CompilerParams(vmem_limit_bytes=VMEM_LIMIT),
        name="outproj",
    )(*mix, w_o, g_post.reshape(1, D_MODEL), mod_l, xs)


def _mlstm_kernel(q_ref, k_ref, v_ref, o_ref, sg_ref, sm_ref, gt_ref, bcol_ref, brow_ref, ng_ref,
                  tril_ref, triu_ref, bd_ref, out_ref, hf_ref, c_ref, n_ref, m_ref, *, chunk, n_ctx, n_tot):
    t_ = chunk
    masks = _head_masks()
    bdm = _block_diag_mask()
    row_i = lax.broadcasted_iota(jnp.int32, (t_, t_), 0)
    col_i = lax.broadcasted_iota(jnp.int32, (t_, t_), 1)
    ones_blk = jnp.concatenate([jnp.zeros((1, W_GROUP), F32), jnp.ones((1, W_GROUP), F32)], axis=1)
    neg_inf = jnp.float32(-jnp.inf)

    def run(reverse):
        c_ref[...] = jnp.zeros_like(c_ref)
        n_ref[...] = jnp.zeros_like(n_ref)
        m_ref[...] = jnp.zeros_like(m_ref)
        tri_c = triu_ref[...] if reverse else tril_ref[...]
        tri_r = tril_ref[...] if reverse else triu_ref[...]
        valid = (col_i >= row_i) if reverse else (col_i <= row_i)
        g0 = 8 if reverse else 0
        end_row = 0 if reverse else t_ - 1

        def body(s, carry):
            if reverse:
                c_idx = jnp.where(s < n_ctx, n_ctx - 1 - s, n_tot - 1 - (s - n_ctx))
            else:
                c_idx = s
            r0 = pl.multiple_of(c_idx * t_, t_)
            q = q_ref[0, pl.ds(r0, t_), :]
            k = k_ref[0, pl.ds(r0, t_), :] * (HEAD_DIM ** -0.5)
            v = v_ref[0, pl.ds(r0, t_), :]
            sm = sm_ref[0, pl.ds(r0, t_), :] + bcol_ref[...]
            gt = gt_ref[0, :, pl.ds(r0, t_)] + brow_ref[...]
            bc_all = _dot_exact_lhs(tri_c, _log_sigmoid(sm))
            br_all = _dot_exact_rhs(_log_sigmoid(gt), tri_r)
            qb = q.astype(BF16)
            kb = k.astype(BF16)
            c_prev = c_ref[...]
            n_prev = n_ref[...]
            acc = jnp.zeros((t_, 2 * W_GROUP), F32)
            m_cols, wi_cols, ws_cols, a_cols = [], [], [], []
            for h in range(N_HEADS):
                i_c = sm[:, g0 + h:g0 + h + 1]
                b_c = bc_all[:, g0 + 4 + h:g0 + 5 + h]
                i_r = gt[g0 + h:g0 + h + 1, :]
                b_r = br_all[g0 + 4 + h:g0 + 5 + h, :]
                m_prev = m_ref[h]
                r_r = i_r - b_r
                cm = jnp.max(jnp.where(valid, r_r, neg_inf), axis=-1, keepdims=True)
                inter = b_c + m_prev
                m_t = jnp.maximum(inter, b_c + cm)
                w_intra = jnp.where(valid, jnp.exp(b_c + r_r - m_t), 0.0)
                s_h = _dot_nt((q * masks[h]).astype(BF16), kb) * w_intra
                vext = (jnp.concatenate([v, jnp.ones_like(v)], axis=1)
                        * jnp.concatenate([masks[h], masks[h]], axis=1))
                acc = acc + _dot(s_h.astype(BF16), vext.astype(BF16))
                m_cols.append(m_t)
                wi_cols.append(jnp.exp(inter - m_t))
                b_end = b_c[end_row:end_row + 1, :]
                g_c = b_end - b_c + i_c
                m_new = jnp.maximum(b_end + m_prev, jnp.max(g_c, axis=0, keepdims=True))
                a_cols.append(jnp.exp(b_end + m_prev - m_new))
                ws_cols.append(jnp.exp(g_c - m_new))
                m_ref[h] = m_new
            m_b = _spread(m_cols, masks)
            wi_b = _spread(wi_cols, masks)
            qc = _dot(qb, c_prev.astype(BF16))
            qn = _dot((q * n_prev).astype(BF16), bd_ref[...])
            num = wi_b * qc + acc[:, 0:W_GROUP]
            den = wi_b * qn + acc[:, W_GROUP:2 * W_GROUP]
            h_out = num / jnp.maximum(jnp.abs(den), jnp.exp(-m_b))
            a_b = _spread(a_cols, masks)
            kw = k * _spread(ws_cols, masks)
            c_ref[...] = a_b * c_prev + bdm * _dot_tn(kw.astype(BF16), v.astype(BF16))
            n_ref[...] = a_b * n_prev + jnp.sum(kw, axis=0, keepdims=True)
            if reverse:
                hsum = hf_ref[pl.ds(r0, t_), :] + h_out
                hsum = hsum * jax.nn.sigmoid(o_ref[0, pl.ds(r0, t_), :])
                hn = _head_rms_norm(hsum, bd_ref[...], ng_ref[...])
                out_ref[0, pl.ds(r0, t_), :] = hn * _silu(sg_ref[0, pl.ds(r0, t_), :])
            else:
                hf_ref[pl.ds(r0, t_), :] = h_out
            return carry

        lax.fori_loop(0, n_tot, body, 0)

    del ones_blk
    run(False)
    run(True)


def _mlstm(p, gt, gate_b, norm_g, consts, ctx_len):
    b_, lt, _ = p.shape
    t_ = MLSTM_CHUNK
    bcol = jnp.zeros((1, 128), F32).at[0, 0:16].set(gate_b.reshape(16))
    brow = gate_b.reshape(16, 1)
    blk = lambda j: pl.BlockSpec((1, lt, W_GROUP), lambda b: (b, 0, j))
    full = lambda shape: pl.BlockSpec(shape, lambda b: (0,) * len(shape))
    kern = functools.partial(_mlstm_kernel, chunk=t_, n_ctx=ctx_len // t_, n_tot=lt // t_)
    return pl.pallas_call(
        kern,
        grid=(b_,),
        in_specs=[blk(MQ), blk(MK), blk(MV), blk(MO), blk(MG),
                  pl.BlockSpec((1, lt, 128), lambda b: (b, 0, SMALL_BLOCK)),
                  pl.BlockSpec((1, 16, lt), lambda b: (b, 0, 0)),
                  full((1, 128)), full((16, 1)), full((1, W_GROUP)),
                  full((t_, t_)), full((t_, t_)), full((W_GROUP, W_GROUP))],
        out_specs=pl.BlockSpec((1, lt, W_GROUP), lambda b: (b, 0, 0)),
        out_shape=jax.ShapeDtypeStruct((b_, lt, W_GROUP), F32),
        scratch_shapes=[pltpu.VMEM((lt, W_GROUP), F32), pltpu.VMEM((W_GROUP, W_GROUP), F32),
                        pltpu.VMEM((1, W_GROUP), F32), pltpu.VMEM((N_HEADS, 1, 1), F32)],
        compiler_params=pltpu.CompilerParams(vmem_limit_bytes=VMEM_LIMIT),
        name="mlstm_mixer",
    )(p, p, p, p, p, p, gt, bcol, brow, norm_g.reshape(1, W_GROUP),
      consts["tril_m"], consts["triu_m"], consts["bd"])


def _gla_kernel(q_ref, k_ref, v_ref, sg_ref, sm_ref, wa_ref, ba_ref, ng_ref, tril_ref, triu_ref, bd_ref,
                out_ref, hf_ref, st_ref, *, chunk, n_ctx, n_tot):
    t_ = chunk
    masks = _head_masks()
    bdm = _block_diag_mask()
    row_i = lax.broadcasted_iota(jnp.int32, (t_, t_), 0)
    col_i = lax.broadcasted_iota(jnp.int32, (t_, t_), 1)

    def run(reverse):
        st_ref[...] = jnp.zeros_like(st_ref)
        tri_c = triu_ref[...] if reverse else tril_ref[...]
        valid = (col_i >= row_i) if reverse else (col_i <= row_i)
        d_idx = 1 if reverse else 0
        end_row = 0 if reverse else t_ - 1

        def body(s, carry):
            if reverse:
                c_idx = jnp.where(s < n_ctx, n_ctx - 1 - s, n_tot - 1 - (s - n_ctx))
            else:
                c_idx = s
            r0 = pl.multiple_of(c_idx * t_, t_)
            q = q_ref[0, pl.ds(r0, t_), :] * (HEAD_DIM ** -0.5)
            k = k_ref[0, pl.ds(r0, t_), :]
            v = v_ref[0, pl.ds(r0, t_), :]
            sm = sm_ref[0, pl.ds(r0, t_), :]
            la = _log_sigmoid(_dot_f32(sm, wa_ref[d_idx]) + ba_ref[d_idx]) * (1.0 / GLA_NORMALIZER)
            bcum = _dot_exact_lhs(tri_c, la)
            b_end = bcum[end_row:end_row + 1, :]
            qd = q * jnp.exp(bcum)
            kd = (k * jnp.exp(-bcum)).astype(BF16)
            vb = v.astype(BF16)
            st_prev = st_ref[...]
            o = _dot_nt(qd.astype(BF16), st_prev.astype(BF16))
            for h in range(N_HEADS):
                att = jnp.where(valid, _dot_nt((qd * masks[h]).astype(BF16), kd), 0.0)
                o = o + _dot(att.astype(BF16), (v * masks[h]).astype(BF16))
            kdec = (k * jnp.exp(b_end - bcum)).astype(BF16)
            st_ref[...] = st_prev * jnp.exp(b_end) + bdm * _dot_tn(vb, kdec)
            if reverse:
                hsum = hf_ref[pl.ds(r0, t_), :] + o
                hn = _head_rms_norm(hsum, bd_ref[...], ng_ref[...])
                out_ref[0, pl.ds(r0, t_), :] = hn * _silu(sg_ref[0, pl.ds(r0, t_), :])
            else:
                hf_ref[pl.ds(r0, t_), :] = o
            return carry

        lax.fori_loop(0, n_tot, body, 0)

    run(False)
    run(True)


def _gla(p, w_alpha, b_alpha, norm_g, consts, ctx_len):
    b_, lt, _ = p.shape
    t_ = GLA_CHUNK
    wa = jnp.zeros((2, 128, W_GROUP), F32)
    wa = wa.at[0, 16:32].set(w_alpha[0]).at[1, 32:48].set(w_alpha[1])
    blk = lambda j: pl.BlockSpec((1, lt, W_GROUP), lambda b: (b, 0, j))
    full = lambda shape: pl.BlockSpec(shape, lambda b: (0,) * len(shape))
    kern = functools.partial(_gla_kernel, chunk=t_, n_ctx=ctx_len // t_, n_tot=lt // t_)
    return pl.pallas_call(
        kern,
        grid=(b_,),
        in_specs=[blk(GQ), blk(GK), blk(GV), blk(GG),
                  pl.BlockSpec((1, lt, 128), lambda b: (b, 0, SMALL_BLOCK)),
                  full((2, 128, W_GROUP)), full((2, 1, W_GROUP)), full((1, W_GROUP)),
                  full((t_, t_)), full((t_, t_)), full((W_GROUP, W_GROUP))],
        out_specs=pl.BlockSpec((1, lt, W_GROUP), lambda b: (b, 0, 0)),
        out_shape=jax.ShapeDtypeStruct((b_, lt, W_GROUP), F32),
        scratch_shapes=[pltpu.VMEM((lt, W_GROUP), F32), pltpu.VMEM((W_GROUP, W_GROUP), F32)],
        compiler_params=pltpu.CompilerParams(vmem_limit_bytes=VMEM_LIMIT),
        name="gla_mixer",
    )(p, p, p, p, p, wa, b_alpha.reshape(2, 1, W_GROUP), norm_g.reshape(1, W_GROUP),
      consts["tril_g"], consts["triu_g"], consts["bd"])


def _attn_kernel(q_ref, k_ref, v_ref, sg_ref, cos_ref, sin_ref, sink_ref, out_ref, *, ctx_len, lat_len):
    j = pl.program_id(1)
    n_ctx_blk = ctx_len // ATT_BLOCK
    is_lat = j >= n_ctx_blk
    i_lat = jnp.maximum(j - n_ctx_blk, 0)
    nwin = 3 * ATT_BLOCK
    w0 = ctx_len + jnp.clip((i_lat - 1) * ATT_BLOCK, 0, lat_len - nwin)
    w0 = pl.multiple_of(w0, ATT_BLOCK)
    q0 = pl.multiple_of(j * ATT_BLOCK, ATT_BLOCK)
    masks = _head_masks()
    lane = lax.broadcasted_iota(jnp.int32, (1, W_GROUP), 1)
    first_half = (lane % HEAD_DIM) < (HEAD_DIM // 2)

    def rope(x, r0, n):
        swapped = jnp.where(first_half, pltpu.roll(x, W_GROUP - HEAD_DIM // 2, 1), pltpu.roll(x, HEAD_DIM // 2, 1))
        return x * cos_ref[pl.ds(r0, n), :] + swapped * sin_ref[pl.ds(r0, n), :]

    q = rope(q_ref[0], q0, ATT_BLOCK) * (HEAD_DIM ** -0.5)
    k_all = jnp.concatenate([rope(k_ref[0, pl.ds(w0, nwin), :], w0, nwin), k_ref[0, 0:ctx_len, :]], axis=0)
    v_all = jnp.concatenate([v_ref[0, pl.ds(w0, nwin), :], v_ref[0, 0:ctx_len, :]], axis=0)
    kb = k_all.astype(BF16)
    nk = nwin + ctx_len
    qpos = q0 + lax.broadcasted_iota(jnp.int32, (ATT_BLOCK, nk), 0)
    ci = lax.broadcasted_iota(jnp.int32, (ATT_BLOCK, nk), 1)
    kpos = w0 + ci
    valid = (ci >= nwin) | (is_lat & (jnp.abs(kpos - qpos) <= WINDOW))
    neg_inf = jnp.float32(-jnp.inf)
    o = jnp.zeros((ATT_BLOCK, W_GROUP), F32)
    for h in range(N_HEADS):
        s = jnp.where(valid, _dot_nt((q * masks[h]).astype(BF16), kb), neg_inf)
        sink = sink_ref[h]
        m = jnp.maximum(jnp.max(s, axis=-1, keepdims=True), sink)
        p = jnp.exp(s - m)
        l = jnp.sum(p, axis=-1, keepdims=True) + jnp.exp(sink - m)
        o = o + _dot(p.astype(BF16), (v_all * masks[h]).astype(BF16)) / l
    out_ref[0] = o * _silu(sg_ref[0])


def _attn(p, sink, consts, ctx_len):
    b_, lt, _ = p.shape
    nb = lt // ATT_BLOCK
    kern = functools.partial(_attn_kernel, ctx_len=ctx_len, lat_len=lt - ctx_len)
    return pl.pallas_call(
        kern,
        grid=(b_, nb),
        in_specs=[pl.BlockSpec((1, ATT_BLOCK, W_GROUP), lambda b, j: (b, j, AQ)),
                  pl.BlockSpec((1, lt, W_GROUP), lambda b, j: (b, 0, AK)),
                  pl.BlockSpec((1, lt, W_GROUP), lambda b, j: (b, 0, AV)),
                  pl.BlockSpec((1, ATT_BLOCK, W_GROUP), lambda b, j: (b, j, AG)),
                  pl.BlockSpec((lt, W_GROUP), lambda b, j: (0, 0)),
                  pl.BlockSpec((lt, W_GROUP), lambda b, j: (0, 0)),
                  pl.BlockSpec((N_HEADS, 1, 1), lambda b, j: (0, 0, 0))],
        out_specs=pl.BlockSpec((1, ATT_BLOCK, W_GROUP), lambda b, j: (b, j, 0)),
        out_shape=jax.ShapeDtypeStruct((b_, lt, W_GROUP), F32),
        compiler_params=pltpu.CompilerParams(vmem_limit_bytes=VMEM_LIMIT),
        name="window_attention",
    )(p, p, p, p, consts["rope_cos"], consts["rope_sin"], sink.reshape(N_HEADS, 1, 1))


class _FftPlan:
    def __init__(self, n1):
        self.n1 = n1
        self.n = n1 * FFT_N2
        self.k1p = -(-(n1 // 2 + 1) // 8) * 8


def _fft_constants(n1, n_in, n_out):
    plan = _FftPlan(n1)
    n, k1p, half = plan.n, plan.k1p, n1 // 2
    k1 = np.arange(k1p)[:, None].astype(np.float64)
    live = (np.arange(k1p) <= half)[:, None]
    i1 = np.arange(n1)[None, :].astype(np.float64)
    ang = 2.0 * np.pi * k1 * i1 / n1
    f1 = np.concatenate([np.where(live, np.cos(ang), 0.0), np.where(live, -np.sin(ang), 0.0)], axis=0)
    i2 = np.arange(FFT_N2)[None, :].astype(np.float64)
    phi = 2.0 * np.pi * k1 * i2 / n
    tw = np.stack([np.cos(phi), -np.sin(phi)], axis=0)
    tw = np.broadcast_to(tw.reshape(2, k1p * FFT_N2, 1), (2, k1p * FFT_N2, 128))
    th = 2.0 * np.pi * np.outer(np.arange(FFT_N2), np.arange(FFT_N2)) / FFT_N2
    c2, s2 = np.cos(th), np.sin(th)
    m_fwd = np.block([[c2, s2], [-s2, c2]])
    m_inv = np.block([[c2, -s2], [s2, c2]])
    wk = np.where(np.arange(k1p) <= half, 2.0, 0.0)
    wk[0] = 1.0
    wk[half] = 1.0
    psi = 2.0 * np.pi * np.outer(np.arange(n1), np.arange(k1p)) / n1
    g = np.concatenate([np.cos(psi) * wk[None, :], -np.sin(psi) * wk[None, :]], axis=1) / n

    def hl(a):
        a32 = jnp.asarray(a, F32)
        hi = a32.astype(BF16)
        lo = (a32 - hi.astype(F32)).astype(BF16)
        return hi, lo

    return dict(f1=hl(f1), f1_in=hl(f1[:, 0:n_in]), g=hl(g[0:n_out]), m_fwd=hl(m_fwd), m_inv=hl(m_inv),
                tw=jnp.asarray(tw, F32))


def _slab_row(k):
    return k * FFT_PITCH if isinstance(k, int) else pl.multiple_of(k * FFT_PITCH, 8)


def _load_slab(ref, k):
    r0 = _slab_row(k)
    return jnp.concatenate([ref[0, pl.ds(r0, FFT_N2), :], ref[1, pl.ds(r0, FFT_N2), :]], axis=1)


def _store_slab(ref, k, val):
    r0 = _slab_row(k)
    ref[0, pl.ds(r0, FFT_N2), :] = val[:, 0:128]
    ref[1, pl.ds(r0, FFT_N2), :] = val[:, 128:256]


FFT_GROUP = 8


def _cross_slab(src_refs, n_src, mat_h, mat_l, dst_refs, n_dst):
    def body(gidx, carry):
        cols = []
        for i in range(FFT_GROUP):
            i2 = gidx * FFT_GROUP + i
            parts = []
            for half in range(2):
                parts.append(jnp.concatenate(
                    [r[half, pl.ds(i2, n_src, stride=FFT_PITCH), :] for r in src_refs], axis=0))
            cols.append(jnp.concatenate(parts, axis=1))
        x = jnp.concatenate(cols, axis=1)
        y = _dot_const(mat_h, mat_l, x)
        for i in range(FFT_GROUP):
            i2 = gidx * FFT_GROUP + i
            for half in range(2):
                c0 = (2 * i + half) * 128
                for d, ref in enumerate(dst_refs):
                    ref[half, pl.ds(i2, n_dst, stride=FFT_PITCH), :] = y[d * n_dst:(d + 1) * n_dst, c0:c0 + 128]
        return carry

    lax.fori_loop(0, FFT_N2 // FFT_GROUP, body, 0)


def _twiddle(ar, ai, tw_ref, k, conj):
    r0 = pl.multiple_of(k * FFT_N2, FFT_N2)
    tr = tw_ref[0, pl.ds(r0, FFT_N2), :]
    ti = tw_ref[1, pl.ds(r0, FFT_N2), :]
    tr = jnp.concatenate([tr, tr], axis=1)
    ti = jnp.concatenate([ti, ti], axis=1)
    if conj:
        return ar * tr + ai * ti, ai * tr - ar * ti
    return ar * tr - ai * ti, ar * ti + ai * tr


def _filter_kernel(feat_ref, dec_ref, w1_ref, b1_ref, w2_ref, b2_ref, w3_ref, b3_ref, fq_ref,
                   f1h_ref, f1l_ref, mh_ref, ml_ref, tw_ref, out_ref, zbuf, are, aim, *, n1, k1p):
    half = n1 // 2

    def fill(jslab, carry):
        r0 = pl.multiple_of(jslab * FFT_N2, FFT_N2)
        z = feat_ref[pl.ds(r0, FFT_N2), :]
        h = jnp.sin(fq_ref[0, 0:1, :] * (_dot_f32(z, w1_ref[0]) + b1_ref[0]))
        h = jnp.sin(fq_ref[0, 1:2, :] * (_dot_f32(h, w2_ref[0]) + b2_ref[0]))
        h = _dot_f32(h, w3_ref[0]) + b3_ref[0]
        hsel = jnp.where(jslab < half, h[:, 0:W_GROUP], h[:, W_GROUP:2 * W_GROUP])
        _store_slab(zbuf, jslab, hsel * dec_ref[pl.ds(r0, FFT_N2), :])
        return carry

    lax.fori_loop(0, n1, fill, 0)
    _cross_slab([zbuf], n1, f1h_ref[...], f1l_ref[...], [are, aim], k1p)

    def slab(k, carry):
        ar, ai = _twiddle(_load_slab(are, k), _load_slab(aim, k), tw_ref, k, conj=False)
        x = _dot_const(mh_ref[...], ml_ref[...], jnp.concatenate([ar, ai], axis=0))
        r0 = pl.multiple_of(k * FFT_N2, FFT_N2)
        out_ref[0, 0, pl.ds(r0, FFT_N2), :] = x[0:FFT_N2]
        out_ref[0, 1, pl.ds(r0, FFT_N2), :] = x[FFT_N2:2 * FFT_N2]
        return carry

    lax.fori_loop(0, k1p, slab, 0)


def _filter_tables(l_):
    pos = np.concatenate([np.arange(l_), l_ - np.arange(l_)]).astype(np.float32)
    pos[l_] = 0.0
    t = pos / np.float32(max(l_ - 1, 1))
    w = np.float32(2.0 * math.pi) * pos / np.float32(l_)
    f = np.linspace(1e-4, HYENA_BANDS - 1, HYENA_BANDS, dtype=np.float32)
    feat = np.zeros((2 * l_, 64), np.float32)
    feat[:, 0] = t
    feat[:, 1:1 + HYENA_BANDS] = np.cos(w[:, None] * f)
    feat[:, 1 + HYENA_BANDS:HYENA_EMB] = -np.sin(w[:, None] * f)
    deltas = np.abs(np.linspace(HYENA_MIN_DECAY, HYENA_MAX_DECAY, W_GROUP, dtype=np.float32))
    dec = np.exp(-t[:, None] * deltas[None, :]).astype(np.float32)
    dec[l_] = 0.0
    return jnp.asarray(feat), jnp.asarray(dec)


def _filter_spectra(l_, w1p, b1, w2, b2, w3, b3, freq, fc):
    n1 = 2 * l_ // FFT_N2
    plan = _FftPlan(n1)
    k1p = plan.k1p
    feat, dec = _filter_tables(l_)
    full = lambda shape: pl.BlockSpec(shape, lambda l: (0,) * len(shape))
    per = lambda shape: pl.BlockSpec((1,) + shape, lambda l: (l,) + (0,) * len(shape))
    kern = functools.partial(_filter_kernel, n1=n1, k1p=k1p)
    return pl.pallas_call(
        kern,
        grid=(DEPTH,),
        in_specs=[full((2 * l_, 64)), full((2 * l_, W_GROUP)),
                  per((64, HYENA_FFN)), per((1, HYENA_FFN)), per((HYENA_FFN, HYENA_FFN)), per((1, HYENA_FFN)),
                  per((HYENA_FFN, 2 * W_GROUP)), per((1, 2 * W_GROUP)), per((2, HYENA_FFN)),
                  full((2 * k1p, n1)), full((2 * k1p, n1)), full((128, 128)), full((128, 128)),
                  full((2, k1p * FFT_N2, 128))],
        out_specs=pl.BlockSpec((1, 2, k1p * FFT_N2, W_GROUP), lambda l: (l, 0, 0, 0)),
        out_shape=jax.ShapeDtypeStruct((DEPTH, 2, k1p * FFT_N2, W_GROUP), F32),
        scratch_shapes=[pltpu.VMEM((2, n1 * FFT_PITCH, 128), F32),
                        pltpu.VMEM((2, k1p * FFT_PITCH, 128), F32),
                        pltpu.VMEM((2, k1p * FFT_PITCH, 128), F32)],
        compiler_params=pltpu.CompilerParams(vmem_limit_bytes=VMEM_LIMIT),
        name="hyena_filter_spectrum_%d" % l_,
    )(feat, dec, w1p, b1.reshape(DEPTH, 1, HYENA_FFN), w2, b2.reshape(DEPTH, 1, HYENA_FFN),
      w3, b3.reshape(DEPTH, 1, 2 * W_GROUP), freq,
      fc["f1"][0], fc["f1"][1], fc["m_fwd"][0], fc["m_fwd"][1], fc["tw"])


def _hyena_segment(v_ref, x1_ref, x0_ref, sg_ref, cw_ref, cb_ref, d_ref, kf_ref, consts, scratch, out_ref,
                   *, row0, seg_len, n1):
    f1h, f1l, gh, gl, mfh, mfl, mih, mil, tw_ref = consts
    zbuf, are, aim, ybuf = scratch
    n_sig = seg_len // FFT_N2
    n_in = max(n_sig, 8)
    n_out = max(n_sig, 8)
    k1p = _FftPlan(n1).k1p
    row_i = lax.broadcasted_iota(jnp.int32, (FFT_N2, W_GROUP), 0)

    def conv3(ref, blk, jslab):
        r = pl.multiple_of(row0 + jslab * FFT_N2, FFT_N2)
        cur = ref[0, pl.ds(r, FFT_N2), :]
        pe = ref[0, pl.ds(jnp.maximum(r - 1, row0), 1), :] * (jslab > 0).astype(F32)
        ne = ref[0, pl.ds(jnp.minimum(r + FFT_N2, row0 + seg_len - 1), 1), :] * (jslab < n_sig - 1).astype(F32)
        prev = jnp.where(row_i == 0, pe, pltpu.roll(cur, 1, 0))
        nxt = jnp.where(row_i == FFT_N2 - 1, ne, pltpu.roll(cur, FFT_N2 - 1, 0))
        c0 = blk * W_GROUP
        return (cw_ref[0:1, c0:c0 + W_GROUP] * prev + cw_ref[1:2, c0:c0 + W_GROUP] * cur
                + cw_ref[2:3, c0:c0 + W_GROUP] * nxt + cb_ref[:, c0:c0 + W_GROUP])

    def pre(jslab, carry):
        _store_slab(zbuf, jslab, conv3(v_ref, 0, jslab) * conv3(x1_ref, 1, jslab))
        return carry

    lax.fori_loop(0, n_sig, pre, 0)
    for jz in range(n_sig, n_in):
        _store_slab(zbuf, jz, jnp.zeros((FFT_N2, W_GROUP), F32))

    _cross_slab([zbuf], n_in, f1h, f1l, [are, aim], k1p)

    def slab(k, carry):
        ar, ai = _twiddle(_load_slab(are, k), _load_slab(aim, k), tw_ref, k, conj=False)
        x = _dot_const(mfh, mfl, jnp.concatenate([ar, ai], axis=0))
        xr, xi = x[0:FFT_N2], x[FFT_N2:2 * FFT_N2]
        r0 = pl.multiple_of(k * FFT_N2, FFT_N2)
        kr = kf_ref[0, pl.ds(r0, FFT_N2), :]
        ki = kf_ref[1, pl.ds(r0, FFT_N2), :]
        yr = xr * kr - xi * ki
        yi = xr * ki + xi * kr
        bm = _dot_const(mih, mil, jnp.concatenate([yr, yi], axis=0))
        br, bi = _twiddle(bm[0:FFT_N2], bm[FFT_N2:2 * FFT_N2], tw_ref, k, conj=True)
        _store_slab(are, k, br)
        _store_slab(aim, k, bi)
        return carry

    lax.fori_loop(0, k1p, slab, 0)
    _cross_slab([are, aim], k1p, gh, gl, [ybuf], n_out)

    def post(jslab, carry):
        r = pl.multiple_of(row0 + jslab * FFT_N2, FFT_N2)
        y = _load_slab(ybuf, jslab) + _load_slab(zbuf, jslab) * d_ref[...]
        out_ref[0, pl.ds(r, FFT_N2), :] = conv3(x0_ref, 2, jslab) * y * _silu(sg_ref[0, pl.ds(r, FFT_N2), :])
        return carry

    lax.fori_loop(0, n_sig, post, 0)


def _hyena_kernel(v_ref, x1_ref, x0_ref, sg_ref, cw_ref, cb_ref, d_ref, kfc_ref, kfl_ref,
                  c_f1h, c_f1l, c_gh, c_gl, c_tw, l_f1h, l_f1l, l_gh, l_gl, l_tw,
                  mfh_ref, mfl_ref, mih_ref, mil_ref, out_ref,
                  zc, arc, aic, yc, zl, arl, ail, yl, *, ctx_len, lat_len):
    mats = (mfh_ref[...], mfl_ref[...], mih_ref[...], mil_ref[...])
    seg = functools.partial(_hyena_segment, v_ref, x1_ref, x0_ref, sg_ref, cw_ref, cb_ref, d_ref)
    seg(kfc_ref, (c_f1h[...], c_f1l[...], c_gh[...], c_gl[...]) + mats + (c_tw,), (zc, arc, aic, yc), out_ref,
        row0=0, seg_len=ctx_len, n1=2 * ctx_len // FFT_N2)
    seg(kfl_ref, (l_f1h[...], l_f1l[...], l_gh[...], l_gl[...]) + mats + (l_tw,), (zl, arl, ail, yl), out_ref,
        row0=ctx_len, seg_len=lat_len, n1=2 * lat_len // FFT_N2)


def _hyena(p, conv_w, conv_b, d, kf_ctx, kf_lat, fc_ctx, fc_lat, ctx_len):
    b_, lt, _ = p.shape
    lat_len = lt - ctx_len
    n1c, n1l = 2 * ctx_len // FFT_N2, 2 * lat_len // FFT_N2
    kc, kl = _FftPlan(n1c).k1p, _FftPlan(n1l).k1p
    blk = lambda j: pl.BlockSpec((1, lt, W_GROUP), lambda b: (b, 0, j))
    full = lambda a: pl.BlockSpec(a.shape, lambda b: (0,) * a.ndim)
    cst = [fc_ctx["f1_in"][0], fc_ctx["f1_in"][1], fc_ctx["g"][0], fc_ctx["g"][1], fc_ctx["tw"],
           fc_lat["f1_in"][0], fc_lat["f1_in"][1], fc_lat["g"][0], fc_lat["g"][1], fc_lat["tw"],
           fc_lat["m_fwd"][0], fc_lat["m_fwd"][1], fc_lat["m_inv"][0], fc_lat["m_inv"][1]]
    small = [conv_w, conv_b.reshape(1, 3 * W_GROUP), d.reshape(1, W_GROUP), kf_ctx, kf_lat]
    sbuf = lambda n: pltpu.VMEM((2, n * FFT_PITCH, 128), F32)
    kern = functools.partial(_hyena_kernel, ctx_len=ctx_len, lat_len=lat_len)
    return pl.pallas_call(
        kern,
        grid=(b_,),
        in_specs=[blk(HV), blk(HX1), blk(HX0), blk(HG)] + [full(a) for a in small] + [full(a) for a in cst],
        out_specs=pl.BlockSpec((1, lt, W_GROUP), lambda b: (b, 0, 0)),
        out_shape=jax.ShapeDtypeStruct((b_, lt, W_GROUP), F32),
        scratch_shapes=[sbuf(max(n1c // 2, 8)), sbuf(kc), sbuf(kc), sbuf(max(n1c // 2, 8)),
                        sbuf(n1l // 2), sbuf(kl), sbuf(kl), sbuf(n1l // 2)],
        compiler_params=pltpu.CompilerParams(vmem_limit_bytes=VMEM_LIMIT),
        name="hyena_mixer",
    )(p, p, p, p, *small, *cst)


def _rope_tables(ctx_len, lat_len):
    pos = np.arange(lat_len)
    row = (pos // GRID_W).astype(np.float32)
    col = (pos % GRID_W).astype(np.float32)
    inv = (np.float32(ROPE_BASE) ** (-np.arange(ROPE_AXIS_FREQS, dtype=np.float32) / np.float32(ROPE_AXIS_FREQS)))
    ang = np.concatenate([row[:, None] * inv, col[:, None] * inv], axis=-1).astype(np.float32)
    cos, sin = np.cos(ang), np.sin(ang)
    cos_h = np.concatenate([cos, cos], axis=-1)
    sin_h = np.concatenate([-sin, sin], axis=-1)
    cos_l = np.tile(cos_h, (1, N_HEADS))
    sin_l = np.tile(sin_h, (1, N_HEADS))
    cos_t = np.concatenate([np.ones((ctx_len, W_GROUP), np.float32), cos_l], axis=0)
    sin_t = np.concatenate([np.zeros((ctx_len, W_GROUP), np.float32), sin_l], axis=0)
    return jnp.asarray(cos_t, F32), jnp.asarray(sin_t, F32)


def _tri(n):
    t = np.tril(np.ones((n, n), np.float32))
    return jnp.asarray(t, BF16), jnp.asarray(t.T, BF16)


def _layout_w_in(w_in):
    a0 = 0
    h0 = 5 * W_GROUP + 16
    g0 = h0 + 4 * W_GROUP
    d0 = g0 + 4 * W_GROUP + 2 * GLA_RANK
    kv = W_GROUP // 2
    k_cols = w_in[..., d0 + W_GROUP:d0 + W_GROUP + kv]
    v_cols = w_in[..., d0 + W_GROUP + kv:d0 + W_GROUP + 2 * kv]

    def expand_kv(c):
        return jnp.concatenate([c[..., 0:HEAD_DIM], c[..., 0:HEAD_DIM],
                                c[..., HEAD_DIM:2 * HEAD_DIM], c[..., HEAD_DIM:2 * HEAD_DIM]], axis=-1)

    small = jnp.concatenate([w_in[..., 5 * W_GROUP:5 * W_GROUP + 16],
                             w_in[..., g0 + 4 * W_GROUP:g0 + 4 * W_GROUP + 2 * GLA_RANK]], axis=-1)
    pad = jnp.zeros(w_in.shape[:-1] + (128 - 48,), w_in.dtype)
    wide = jnp.concatenate([
        w_in[..., a0:a0 + 5 * W_GROUP],
        w_in[..., h0:h0 + 4 * W_GROUP],
        w_in[..., g0:g0 + 4 * W_GROUP],
        w_in[..., d0:d0 + W_GROUP], expand_kv(k_cols), expand_kv(v_cols),
        w_in[..., d0 + W_GROUP + 2 * kv:d0 + 2 * W_GROUP + 2 * kv],
        small, pad], axis=-1)
    w_st = jnp.swapaxes(w_in[..., 5 * W_GROUP:5 * W_GROUP + 16], -1, -2)
    return wide.astype(BF16), w_st.astype(BF16)


def kernel(x, c, ctx, c_ctx, w_ada, b_ada, g_pre, g_post, w_in, mlstm_gate_b, mlstm_norm_g, hyena_conv_w,
           hyena_conv_b, hyena_w1, hyena_b1, hyena_w2, hyena_b2, hyena_w3, hyena_b3, hyena_freq, hyena_d,
           gla_w_alpha, gla_b_alpha, gla_norm_g, attn_sink, w_out):
    b_, lat_len, _ = x.shape
    ctx_len = ctx.shape[1]
    n_ctx_tiles = ctx_len // ROW_TILE

    tril_m, triu_m = _tri(MLSTM_CHUNK)
    tril_g, triu_g = _tri(GLA_CHUNK)
    bd = np.kron(np.eye(N_HEADS, dtype=np.float32), np.ones((HEAD_DIM, HEAD_DIM), np.float32))
    rope_cos, rope_sin = _rope_tables(ctx_len, lat_len)
    consts = dict(tril_m=tril_m, triu_m=triu_m, tril_g=tril_g, triu_g=triu_g, bd=jnp.asarray(bd, BF16),
                  rope_cos=rope_cos, rope_sin=rope_sin)
    fc_ctx = _fft_constants(2 * ctx_len // FFT_N2, max(ctx_len // FFT_N2, 8), max(ctx_len // FFT_N2, 8))
    fc_lat = _fft_constants(2 * lat_len // FFT_N2, lat_len // FFT_N2, lat_len // FFT_N2)

    cc = jnp.zeros((16, D_MODEL), F32).at[0:b_].set(c).at[b_].set(c_ctx)
    mod = _modulation(cc, w_ada, b_ada)
    mod_ctx = jnp.broadcast_to(mod[:, b_:b_ + 1], (DEPTH, b_, 3 * D_MODEL))
    mod_all = jnp.stack([mod_ctx, mod[:, 0:b_]], axis=2).reshape(DEPTH, 2 * b_, 1, 3 * D_MODEL)

    w_p, w_st = _layout_w_in(w_in)
    w_o = w_out.astype(BF16)
    w1p = jnp.zeros((DEPTH, 64, HYENA_FFN), F32).at[:, 0:HYENA_EMB].set(hyena_w1)
    filt = (w1p, hyena_b1, hyena_w2, hyena_b2, hyena_w3, hyena_b3, hyena_freq)
    kf_ctx = _filter_spectra(ctx_len, *filt, fc_ctx)
    kf_lat = _filter_spectra(lat_len, *filt, fc_lat)

    xs = jnp.concatenate([ctx, x], axis=1)
    for l in range(DEPTH):
        last = l == DEPTH - 1
        p, gt = _inproj(xs, mod_all[l], g_pre[l], w_p[l], w_st[l], n_ctx_tiles)
        a = _mlstm(p, gt, mlstm_gate_b[l], mlstm_norm_g[l], consts, ctx_len)
        hy = _hyena(p, hyena_conv_w[l], hyena_conv_b[l], hyena_d[l], kf_ctx[l], kf_lat[l], fc_ctx, fc_lat, ctx_len)
        g = _gla(p, gla_w_alpha[l], gla_b_alpha[l], gla_norm_g[l], consts, ctx_len)
        d = _attn(p, attn_sink[l], consts, ctx_len)
        xs = _outproj((a, hy, g, d), w_o[l], g_post[l], mod_all[l], xs, n_ctx_tiles, skip_ctx=last)
    return xs
```

```python
import functools
import math

import numpy as np
import jax
import jax.numpy as jnp
from jax import lax
from jax.experimental import pallas as pl
from jax.experimental.pallas import tpu as pltpu

F32 = jnp.float32
BF16 = jnp.bfloat16

D_MODEL = 1024
DEPTH = 4
GRID_W = 64
W_GROUP = 256
HEAD_DIM = 64
N_HEADS = 4
ATT_KV_HEADS = 2
WINDOW = 128
ATT_BLOCK = 128
GLA_RANK = 16
GLA_NORMALIZER = 16.0
HYENA_BANDS = 16
HYENA_EMB = 1 + 2 * HYENA_BANDS
HYENA_FFN = 64
HYENA_MIN_DECAY = math.log(1e-2) / 1.5
HYENA_MAX_DECAY = math.log(1e-2) / 0.3
ROPE_BASE = 10000.0
ROPE_AXIS_FREQS = HEAD_DIM // 4
EPS = 1e-6

N_WIDE_BLOCKS = 17
SMALL_COL0 = N_WIDE_BLOCKS * W_GROUP
N_PROJ = SMALL_COL0 + 128
SMALL_BLOCK = SMALL_COL0 // 128
(MQ, MK, MV, MO, MG, HV, HX1, HX0, HG, GQ, GK, GV, GG, AQ, AK, AV, AG) = range(N_WIDE_BLOCKS)

ROW_TILE = 256
MLSTM_CHUNK = 256
GLA_CHUNK = 128
FFT_N2 = 64
FFT_PITCH = 72
FFT_SLAB_UNROLL = 4
VMEM_LIMIT = 56 * 1024 * 1024


def _dot(a, b):
    return jnp.dot(a, b, preferred_element_type=F32)


def _dot_nt(a, b):
    return lax.dot_general(a, b, (((1,), (1,)), ((), ())), preferred_element_type=F32)


def _dot_tn(a, b):
    return lax.dot_general(a, b, (((0,), (0,)), ((), ())), preferred_element_type=F32)


def _split2(x):
    hi = x.astype(BF16)
    lo = (x - hi.astype(F32)).astype(BF16)
    return hi, lo


def _split3(x):
    h1 = x.astype(BF16)
    r1 = x - h1.astype(F32)
    h2 = r1.astype(BF16)
    h3 = (r1 - h2.astype(F32)).astype(BF16)
    return h1, h2, h3


def _dot_const(ch, cl, x):
    xh, xl = _split2(x)
    return _dot(ch, xh) + _dot(ch, xl) + _dot(cl, xh)


def _dot_f32(a, b):
    ah, al = _split2(a)
    bh, bl = _split2(b)
    return _dot(ah, bh) + _dot(ah, bl) + _dot(al, bh)


def _dot_exact_lhs(c, x):
    x1, x2, x3 = _split3(x)
    return _dot(c, x1) + _dot(c, x2) + _dot(c, x3)


def _dot_exact_rhs(x, c):
    x1, x2, x3 = _split3(x)
    return _dot(x1, c) + _dot(x2, c) + _dot(x3, c)


def _log_sigmoid(x):
    return jnp.minimum(x, 0.0) - jnp.log(1.0 + jnp.exp(-jnp.abs(x)))


def _silu(x):
    return x * jax.nn.sigmoid(x)


def _head_masks():
    lane = lax.broadcasted_iota(jnp.int32, (1, W_GROUP), 1) // HEAD_DIM
    return [(lane == h).astype(F32) for h in range(N_HEADS)]


def _block_diag_mask():
    r = lax.broadcasted_iota(jnp.int32, (W_GROUP, W_GROUP), 0) // HEAD_DIM
    c = lax.broadcasted_iota(jnp.int32, (W_GROUP, W_GROUP), 1) // HEAD_DIM
    return (r == c).astype(F32)


def _spread(cols):
    lane = lax.broadcasted_iota(jnp.int32, (1, W_GROUP), 1)
    out = cols[N_HEADS - 1]
    for h in range(N_HEADS - 2, -1, -1):
        out = jnp.where(lane < (h + 1) * HEAD_DIM, cols[h], out)
    return out


def _head_rms_norm(h, bd_bf16, g_row):
    sq = h * h
    sh, sl = _split2(sq)
    ms = (_dot(sh, bd_bf16) + _dot(sl, bd_bf16)) * (1.0 / HEAD_DIM)
    return h * lax.rsqrt(ms + EPS) * g_row


def _mod_kernel(c_ref, w_ref, b_ref, o_ref):
    s = _silu(c_ref[...])
    o_ref[0] = _dot_f32(s, w_ref[0]) + b_ref[0]


def _modulation(cc, w_ada, b_ada):
    nblk = 3 * D_MODEL // 1024
    return pl.pallas_call(
        _mod_kernel,
        grid=(DEPTH, nblk),
        in_specs=[
            pl.BlockSpec((16, D_MODEL), lambda l, j: (0, 0)),
            pl.BlockSpec((1, D_MODEL, 1024), lambda l, j: (l, 0, j)),
            pl.BlockSpec((1, 1, 1024), lambda l, j: (l, 0, j)),
        ],
        out_specs=pl.BlockSpec((1, 16, 1024), lambda l, j: (l, 0, j)),
        out_shape=jax.ShapeDtypeStruct((DEPTH, 16, 3 * D_MODEL), F32),
        compiler_params=pltpu.CompilerParams(vmem_limit_bytes=VMEM_LIMIT),
        name="adaln_modulation",
    )(cc, w_ada, b_ada.reshape(DEPTH, 1, 3 * D_MODEL))


def _inproj_kernel(x_ref, mod_ref, g_ref, w_ref, wst_ref, p_ref, gt_ref):
    x = x_ref[0]
    ms = jnp.mean(x * x, axis=-1, keepdims=True)
    y = x * lax.rsqrt(ms + EPS) * g_ref[...]
    sh = mod_ref[0, 0, :, 0:D_MODEL]
    sc = mod_ref[0, 0, :, D_MODEL:2 * D_MODEL]
    hb = (y * (1.0 + sc) + sh).astype(BF16)
    p_ref[0] = _dot(hb, w_ref[0])
    gt_ref[0] = _dot_nt(wst_ref[0], hb)


def _inproj(xs, mod_all, layer, g_pre, w_p, w_st, n_ctx_tiles):
    b_, lt, _ = xs.shape
    nt = lt // ROW_TILE
    return pl.pallas_call(
        _inproj_kernel,
        grid=(b_, nt),
        in_specs=[
            pl.BlockSpec((1, ROW_TILE, D_MODEL), lambda b, i: (b, i, 0)),
            pl.BlockSpec((1, 1, 1, 3 * D_MODEL),
                         lambda b, i: (layer, 2 * b + (i >= n_ctx_tiles).astype(jnp.int32), 0, 0)),
            pl.BlockSpec((1, D_MODEL), lambda b, i: (0, 0)),
            pl.BlockSpec((1, D_MODEL, N_PROJ), lambda b, i: (layer, 0, 0)),
            pl.BlockSpec((1, 16, D_MODEL), lambda b, i: (layer, 0, 0)),
        ],
        out_specs=[
            pl.BlockSpec((1, ROW_TILE, N_PROJ), lambda b, i: (b, i, 0)),
            pl.BlockSpec((1, 16, ROW_TILE), lambda b, i: (b, 0, i)),
        ],
        out_shape=[
            jax.ShapeDtypeStruct((b_, lt, N_PROJ), F32),
            jax.ShapeDtypeStruct((b_, 16, lt), F32),
        ],
        compiler_params=pltpu.CompilerParams(vmem_limit_bytes=VMEM_LIMIT),
        name="inproj",
    )(xs, mod_all, g_pre.reshape(1, D_MODEL), w_p, w_st)


def _outproj_kernel(a_ref, h_ref, g_ref, d_ref, w_ref, gp_ref, mod_ref, x_ref, o_ref):
    mix = jnp.concatenate([a_ref[0].astype(BF16), h_ref[0].astype(BF16),
                           g_ref[0].astype(BF16), d_ref[0].astype(BF16)], axis=1)
    y = _dot(mix, w_ref[0])
    ms = jnp.mean(y * y, axis=-1, keepdims=True)
    yn = y * lax.rsqrt(ms + EPS) * gp_ref[...]
    gt = mod_ref[0, 0, :, 2 * D_MODEL:3 * D_MODEL]
    o_ref[0] = x_ref[0] + gt * yn


def _outproj(mix, w_o, layer, g_post, mod_all, xs, n_ctx_tiles, skip_ctx):
    b_, lt, _ = xs.shape
    nt = lt // ROW_TILE
    t0 = n_ctx_tiles if skip_ctx else 0
    rows = lambda b, i: (b, i + t0, 0)
    return pl.pallas_call(
        _outproj_kernel,
        grid=(b_, nt - t0),
        in_specs=[pl.BlockSpec((1, ROW_TILE, W_GROUP), rows)] * 4 + [
            pl.BlockSpec((1, D_MODEL, D_MODEL), lambda b, i: (layer, 0, 0)),
            pl.BlockSpec((1, D_MODEL), lambda b, i: (0, 0)),
            pl.BlockSpec((1, 1, 1, 3 * D_MODEL),
                         lambda b, i: (layer, 2 * b + (i + t0 >= n_ctx_tiles).astype(jnp.int32), 0, 0)),
            pl.BlockSpec((1, ROW_TILE, D_MODEL), rows),
        ],
        out_specs=pl.BlockSpec((1, ROW_TILE, D_MODEL), lambda b, i: (b, i, 0)),
        out_shape=jax.ShapeDtypeStruct((b_, lt - t0 * ROW_TILE, D_MODEL), F32),
        compiler_params=pltpu.CompilerParams(vmem_limit_bytes=VMEM_LIMIT),
        name="outproj",
    )(*mix, w_o, g_post.reshape(1, D_MODEL), mod_all, xs)


def _interleave(*gens):
    live = list(gens)
    while live:
        live = [g for g in live if next(g, StopIteration) is not StopIteration]


def _scan_chunk(s, reverse, n_ctx, n_tot):
    if not reverse:
        return s
    return jnp.where(s < n_ctx, n_ctx - 1 - s, n_tot - 1 - (s - n_ctx))


def _mlstm_kernel(q_ref, k_ref, v_ref, o_ref, sg_ref, sm_ref, gt_ref, bcol_ref, brow_ref, ng_ref,
                  tril_ref, triu_ref, bd_ref, mones_ref, out_ref, hf_ref, hb_ref, c_ref, n_ref, m_ref,
                  *, chunk, n_ctx, n_tot):
    t_ = chunk
    masks_bf = [m.astype(BF16) for m in _head_masks()]
    bdm = _block_diag_mask()
    row_i = lax.broadcasted_iota(jnp.int32, (t_, t_), 0)
    col_i = lax.broadcasted_iota(jnp.int32, (t_, t_), 1)
    neg_inf = jnp.float32(-jnp.inf)
    c_ref[...] = jnp.zeros_like(c_ref)
    n_ref[...] = jnp.zeros_like(n_ref)
    m_ref[...] = jnp.zeros_like(m_ref)

    def step(reverse, c_idx, h_ref):
        d_ = 1 if reverse else 0
        tri_c = triu_ref[...] if reverse else tril_ref[...]
        tri_r = tril_ref[...] if reverse else triu_ref[...]
        valid = (col_i >= row_i) if reverse else (col_i <= row_i)
        g0 = 8 * d_
        end_row = 0 if reverse else t_ - 1
        r0 = pl.multiple_of(c_idx * t_, t_)
        q = q_ref[0, pl.ds(r0, t_), :]
        k = k_ref[0, pl.ds(r0, t_), :] * (HEAD_DIM ** -0.5)
        v = v_ref[0, pl.ds(r0, t_), :]
        sm = sm_ref[0, pl.ds(r0, t_), :] + bcol_ref[...]
        gt = gt_ref[0, :, pl.ds(r0, t_)] + brow_ref[...]
        bc_all = _dot_exact_lhs(tri_c, _log_sigmoid(sm))
        br_all = _dot_exact_rhs(_log_sigmoid(gt), tri_r)
        qb = q.astype(BF16)
        kb = k.astype(BF16)
        vb = v.astype(BF16)
        c_prev = c_ref[d_]
        n_prev = n_ref[d_]
        yield
        s_cat, v_cat = [], []
        m_cols, wi_cols, ws_cols, a_cols = [], [], [], []
        for h in range(N_HEADS):
            i_c = sm[:, g0 + h:g0 + h + 1]
            b_c = bc_all[:, g0 + 4 + h:g0 + 5 + h]
            i_r = gt[g0 + h:g0 + h + 1, :]
            b_r = br_all[g0 + 4 + h:g0 + 5 + h, :]
            m_prev = m_ref[d_, h]
            r_r = i_r - b_r
            cm = jnp.max(jnp.where(valid, r_r, neg_inf), axis=-1, keepdims=True)
            inter = b_c + m_prev
            m_t = jnp.maximum(inter, b_c + cm)
            w_intra = jnp.where(valid, jnp.exp((b_c - m_t) + r_r), 0.0)
            s_cat.append((_dot_nt(qb * masks_bf[h], kb) * w_intra).astype(BF16))
            v_cat.append(vb * masks_bf[h])
            m_cols.append(m_t)
            wi_cols.append(jnp.exp(inter - m_t))
            b_end = b_c[end_row:end_row + 1, :]
            g_c = b_end - b_c + i_c
            m_new = jnp.maximum(b_end + m_prev, jnp.max(g_c, axis=0, keepdims=True))
            a_cols.append(jnp.exp(b_end + m_prev - m_new))
            ws_cols.append(jnp.exp(g_c - m_new))
            m_ref[d_, h] = m_new
        yield
        v_stack = jnp.concatenate([jnp.concatenate(v_cat, axis=0), mones_ref[...]], axis=1)
        acc = _dot(jnp.concatenate(s_cat, axis=1), v_stack)
        m_b = _spread(m_cols)
        wi_b = _spread(wi_cols)
        qc = _dot(qb, c_prev.astype(BF16))
        qn = _dot((q * n_prev).astype(BF16), bd_ref[...])
        yield
        num = wi_b * qc + acc[:, 0:W_GROUP]
        den = wi_b * qn + acc[:, W_GROUP:2 * W_GROUP]
        h_ref[pl.ds(r0, t_), :] = num / jnp.maximum(jnp.abs(den), jnp.exp(-m_b))
        a_b = _spread(a_cols)
        kw = k * _spread(ws_cols)
        c_ref[d_] = a_b * c_prev + bdm * _dot_tn(kw.astype(BF16), vb)
        n_ref[d_] = a_b * n_prev + jnp.sum(kw, axis=0, keepdims=True)

    def body(s, carry):
        _interleave(step(False, _scan_chunk(s, False, n_ctx, n_tot), hf_ref),
                    step(True, _scan_chunk(s, True, n_ctx, n_tot), hb_ref))
        return carry

    lax.fori_loop(0, n_tot, body, 0)

    def finish(c_idx, carry):
        r0 = pl.multiple_of(c_idx * t_, t_)
        hsum = hf_ref[pl.ds(r0, t_), :] + hb_ref[pl.ds(r0, t_), :]
        hsum = hsum * jax.nn.sigmoid(o_ref[0, pl.ds(r0, t_), :])
        hn = _head_rms_norm(hsum, bd_ref[...], ng_ref[...])
        out_ref[0, pl.ds(r0, t_), :] = hn * _silu(sg_ref[0, pl.ds(r0, t_), :])
        return carry

    lax.fori_loop(0, n_tot, finish, 0)


def _mlstm(p, gt, gate_b, norm_g, consts, ctx_len):
    b_, lt, _ = p.shape
    t_ = MLSTM_CHUNK
    bcol = jnp.zeros((1, 128), F32).at[0, 0:16].set(gate_b.reshape(16))
    brow = gate_b.reshape(16, 1)
    blk = lambda j: pl.BlockSpec((1, lt, W_GROUP), lambda b: (b, 0, j))
    full = lambda shape: pl.BlockSpec(shape, lambda b: (0,) * len(shape))
    kern = functools.partial(_mlstm_kernel, chunk=t_, n_ctx=ctx_len // t_, n_tot=lt // t_)
    return pl.pallas_call(
        kern,
        grid=(b_,),
        in_specs=[blk(MQ), blk(MK), blk(MV), blk(MO), blk(MG),
                  pl.BlockSpec((1, lt, 128), lambda b: (b, 0, SMALL_BLOCK)),
                  pl.BlockSpec((1, 16, lt), lambda b: (b, 0, 0)),
                  full((1, 128)), full((16, 1)), full((1, W_GROUP)),
                  full((t_, t_)), full((t_, t_)), full((W_GROUP, W_GROUP)), full((N_HEADS * t_, W_GROUP))],
        out_specs=pl.BlockSpec((1, lt, W_GROUP), lambda b: (b, 0, 0)),
        out_shape=jax.ShapeDtypeStruct((b_, lt, W_GROUP), F32),
        scratch_shapes=[pltpu.VMEM((lt, W_GROUP), F32), pltpu.VMEM((lt, W_GROUP), F32),
                        pltpu.VMEM((2, W_GROUP, W_GROUP), F32), pltpu.VMEM((2, 1, W_GROUP), F32),
                        pltpu.VMEM((2, N_HEADS, 1, 1), F32)],
        compiler_params=pltpu.CompilerParams(vmem_limit_bytes=VMEM_LIMIT),
        name="mlstm_mixer",
    )(p, p, p, p, p, p, gt, bcol, brow, norm_g.reshape(1, W_GROUP),
      consts["tril_m"], consts["triu_m"], consts["bd"], consts["mones_m"])


def _gla_kernel(q_ref, k_ref, v_ref, sg_ref, sm_ref, wa_ref, ba_ref, ng_ref, tril_ref, triu_ref, bd_ref,
                out_ref, hf_ref, hb_ref, st_ref, *, chunk, n_ctx, n_tot):
    t_ = chunk
    masks_bf = [m.astype(BF16) for m in _head_masks()]
    bdm = _block_diag_mask()
    row_i = lax.broadcasted_iota(jnp.int32, (t_, t_), 0)
    col_i = lax.broadcasted_iota(jnp.int32, (t_, t_), 1)
    st_ref[...] = jnp.zeros_like(st_ref)

    def step(reverse, c_idx, h_ref):
        d_ = 1 if reverse else 0
        tri_c = triu_ref[...] if reverse else tril_ref[...]
        valid = (col_i >= row_i) if reverse else (col_i <= row_i)
        end_row = 0 if reverse else t_ - 1
        r0 = pl.multiple_of(c_idx * t_, t_)
        q = q_ref[0, pl.ds(r0, t_), :] * (HEAD_DIM ** -0.5)
        k = k_ref[0, pl.ds(r0, t_), :]
        vb = v_ref[0, pl.ds(r0, t_), :].astype(BF16)
        sm = sm_ref[0, pl.ds(r0, t_), :]
        la = _log_sigmoid(_dot_f32(sm, wa_ref[d_]) + ba_ref[d_]) * (1.0 / GLA_NORMALIZER)
        yield
        bcum = _dot_exact_lhs(tri_c, la)
        yield
        b_end = bcum[end_row:end_row + 1, :]
        qdb = (q * jnp.exp(bcum)).astype(BF16)
        kd = (k * jnp.exp(-bcum)).astype(BF16)
        st_prev = st_ref[d_]
        att = [jnp.where(valid, _dot_nt(qdb * masks_bf[h], kd), 0.0).astype(BF16) for h in range(N_HEADS)]
        yield
        v_stack = jnp.concatenate([vb * masks_bf[h] for h in range(N_HEADS)], axis=0)
        h_ref[pl.ds(r0, t_), :] = (_dot_nt(qdb, st_prev.astype(BF16))
                                   + _dot(jnp.concatenate(att, axis=1), v_stack))
        kdec = (k * jnp.exp(b_end - bcum)).astype(BF16)
        st_ref[d_] = st_prev * jnp.exp(b_end) + bdm * _dot_tn(vb, kdec)

    def body(s, carry):
        _interleave(step(False, _scan_chunk(s, False, n_ctx, n_tot), hf_ref),
                    step(True, _scan_chunk(s, True, n_ctx, n_tot), hb_ref))
        return carry

    lax.fori_loop(0, n_tot, body, 0)

    def finish(c_idx, carry):
        r0 = pl.multiple_of(c_idx * t_, t_)
        hsum = hf_ref[pl.ds(r0, t_), :] + hb_ref[pl.ds(r0, t_), :]
        hn = _head_rms_norm(hsum, bd_ref[...], ng_ref[...])
        out_ref[0, pl.ds(r0, t_), :] = hn * _silu(sg_ref[0, pl.ds(r0, t_), :])
        return carry

    lax.fori_loop(0, n_tot, finish, 0)


def _gla(p, w_alpha, b_alpha, norm_g, consts, ctx_len):
    b_, lt, _ = p.shape
    t_ = GLA_CHUNK
    wa = jnp.zeros((2, 128, W_GROUP), F32)
    wa = wa.at[0, 16:32].set(w_alpha[0]).at[1, 32:48].set(w_alpha[1])
    blk = lambda j: pl.BlockSpec((1, lt, W_GROUP), lambda b: (b, 0, j))
    full = lambda shape: pl.BlockSpec(shape, lambda b: (0,) * len(shape))
    kern = functools.partial(_gla_kernel, chunk=t_, n_ctx=ctx_len // t_, n_tot=lt // t_)
    return pl.pallas_call(
        kern,
        grid=(b_,),
        in_specs=[blk(GQ), blk(GK), blk(GV), blk(GG),
                  pl.BlockSpec((1, lt, 128), lambda b: (b, 0, SMALL_BLOCK)),
                  full((2, 128, W_GROUP)), full((2, 1, W_GROUP)), full((1, W_GROUP)),
                  full((t_, t_)), full((t_, t_)), full((W_GROUP, W_GROUP))],
        out_specs=pl.BlockSpec((1, lt, W_GROUP), lambda b: (b, 0, 0)),
        out_shape=jax.ShapeDtypeStruct((b_, lt, W_GROUP), F32),
        scratch_shapes=[pltpu.VMEM((lt, W_GROUP), F32), pltpu.VMEM((lt, W_GROUP), F32),
                        pltpu.VMEM((2, W_GROUP, W_GROUP), F32)],
        compiler_params=pltpu.CompilerParams(vmem_limit_bytes=VMEM_LIMIT),
        name="gla_mixer",
    )(p, p, p, p, p, wa, b_alpha.reshape(2, 1, W_GROUP), norm_g.reshape(1, W_GROUP),
      consts["tril_g"], consts["triu_g"], consts["bd"])


def _attn_kernel(q_ref, k_ref, v_ref, sg_ref, cos_ref, sin_ref, sink_ref, out_ref, kb_ref, vb_ref,
                 *, ctx_len, lat_len):
    j = pl.program_id(1)
    lt = ctx_len + lat_len
    n_ctx_blk = ctx_len // ATT_BLOCK
    is_lat = j >= n_ctx_blk
    i_lat = jnp.maximum(j - n_ctx_blk, 0)
    nwin = 3 * ATT_BLOCK
    w0 = ctx_len + jnp.clip((i_lat - 1) * ATT_BLOCK, 0, lat_len - nwin)
    w0 = pl.multiple_of(w0, ATT_BLOCK)
    q0 = pl.multiple_of(j * ATT_BLOCK, ATT_BLOCK)
    masks = _head_masks()
    lane = lax.broadcasted_iota(jnp.int32, (1, W_GROUP), 1)
    first_half = (lane % HEAD_DIM) < (HEAD_DIM // 2)
    log2e = math.log2(math.e)

    def rope(x, r0, n):
        swapped = jnp.where(first_half, pltpu.roll(x, W_GROUP - HEAD_DIM // 2, 1), pltpu.roll(x, HEAD_DIM // 2, 1))
        return x * cos_ref[pl.ds(r0, n), :] + swapped * sin_ref[pl.ds(r0, n), :]

    @pl.when(j == 0)
    def _():
        def fill(i, carry):
            r0 = pl.multiple_of(i * ROW_TILE, ROW_TILE)
            kb_ref[pl.ds(r0, ROW_TILE), :] = rope(k_ref[0, pl.ds(r0, ROW_TILE), :], r0, ROW_TILE).astype(BF16)
            vb_ref[pl.ds(r0, ROW_TILE), :] = v_ref[0, pl.ds(r0, ROW_TILE), :].astype(BF16)
            return carry
        lax.fori_loop(0, lt // ROW_TILE, fill, 0)

    q = rope(q_ref[0], q0, ATT_BLOCK) * (HEAD_DIM ** -0.5 * log2e)
    k_win = kb_ref[pl.ds(w0, nwin), :]
    v_win = vb_ref[pl.ds(w0, nwin), :]
    k_ctx = kb_ref[0:ctx_len, :]
    v_ctx = vb_ref[0:ctx_len, :]
    rel = (lax.broadcasted_iota(jnp.int32, (ATT_BLOCK, nwin), 1)
           - lax.broadcasted_iota(jnp.int32, (ATT_BLOCK, nwin), 0)) + (w0 - q0)
    band = is_lat & (jnp.abs(rel) <= WINDOW)
    neg_inf = jnp.float32(-jnp.inf)
    q_stack = jnp.concatenate([(q * masks[h]).astype(BF16) for h in range(N_HEADS)], axis=0)
    band4 = jnp.concatenate([band] * N_HEADS, axis=0)
    sink = jnp.concatenate([jnp.broadcast_to(sink_ref[h] * log2e, (ATT_BLOCK, 1)) for h in range(N_HEADS)], axis=0)
    s_loc = jnp.where(band4, _dot_nt(q_stack, k_win), neg_inf)
    s_ctx = _dot_nt(q_stack, k_ctx)
    m = jnp.maximum(jnp.maximum(jnp.max(s_loc, axis=-1, keepdims=True),
                                jnp.max(s_ctx, axis=-1, keepdims=True)), sink)
    p_loc = jnp.exp2(s_loc - m)
    p_ctx = jnp.exp2(s_ctx - m)
    l = (jnp.sum(p_loc, axis=-1, keepdims=True) + jnp.sum(p_ctx, axis=-1, keepdims=True)
         + jnp.exp2(sink - m))
    o_all = (_dot(p_loc.astype(BF16), v_win) + _dot(p_ctx.astype(BF16), v_ctx)) * (1.0 / l)
    o = o_all[0:ATT_BLOCK] * masks[0]
    for h in range(1, N_HEADS):
        o = o + o_all[h * ATT_BLOCK:(h + 1) * ATT_BLOCK] * masks[h]
    out_ref[0] = o * _silu(sg_ref[0])


def _attn(p, sink, consts, ctx_len):
    b_, lt, _ = p.shape
    nb = lt // ATT_BLOCK
    kern = functools.partial(_attn_kernel, ctx_len=ctx_len, lat_len=lt - ctx_len)
    return pl.pallas_call(
        kern,
        grid=(b_, nb),
        in_specs=[pl.BlockSpec((1, ATT_BLOCK, W_GROUP), lambda b, j: (b, j, AQ)),
                  pl.BlockSpec((1, lt, W_GROUP), lambda b, j: (b, 0, AK)),
                  pl.BlockSpec((1, lt, W_GROUP), lambda b, j: (b, 0, AV)),
                  pl.BlockSpec((1, ATT_BLOCK, W_GROUP), lambda b, j: (b, j, AG)),
                  pl.BlockSpec((lt, W_GROUP), lambda b, j: (0, 0)),
                  pl.BlockSpec((lt, W_GROUP), lambda b, j: (0, 0)),
                  pl.BlockSpec((N_HEADS, 1, 1), lambda b, j: (0, 0, 0))],
        out_specs=pl.BlockSpec((1, ATT_BLOCK, W_GROUP), lambda b, j: (b, j, 0)),
        out_shape=jax.ShapeDtypeStruct((b_, lt, W_GROUP), F32),
        scratch_shapes=[pltpu.VMEM((lt, W_GROUP), BF16), pltpu.VMEM((lt, W_GROUP), BF16)],
        compiler_params=pltpu.CompilerParams(vmem_limit_bytes=VMEM_LIMIT,
                                             dimension_semantics=("arbitrary", "arbitrary")),
        name="window_attention",
    )(p, p, p, p, consts["rope_cos"], consts["rope_sin"], sink.reshape(N_HEADS, 1, 1))


class _FftPlan:
    def __init__(self, n1):
        self.n1 = n1
        self.n = n1 * FFT_N2
        self.k1p = -(-(n1 // 2 + 1) // 8) * 8


def _fft_constants(n1, n_in, n_out):
    plan = _FftPlan(n1)
    n, k1p, half = plan.n, plan.k1p, n1 // 2
    k1 = np.arange(k1p)[:, None].astype(np.float64)
    live = (np.arange(k1p) <= half)[:, None]
    i1 = np.arange(n1)[None, :].astype(np.float64)
    ang = 2.0 * np.pi * k1 * i1 / n1
    f1 = np.concatenate([np.where(live, np.cos(ang), 0.0), np.where(live, -np.sin(ang), 0.0)], axis=0)
    i2 = np.arange(FFT_N2)[None, :].astype(np.float64)
    phi = 2.0 * np.pi * k1 * i2 / n
    tw = np.stack([np.cos(phi), -np.sin(phi)], axis=0)
    tw = np.broadcast_to(tw.reshape(2, k1p * FFT_N2, 1), (2, k1p * FFT_N2, 128))
    th = 2.0 * np.pi * np.outer(np.arange(FFT_N2), np.arange(FFT_N2)) / FFT_N2
    c2, s2 = np.cos(th), np.sin(th)
    m_fwd = np.block([[c2, s2], [-s2, c2]])
    m_inv = np.block([[c2, -s2], [s2, c2]])
    wk = np.where(np.arange(k1p) <= half, 2.0, 0.0)
    wk[0] = 1.0
    wk[half] = 1.0
    psi = 2.0 * np.pi * np.outer(np.arange(n1), np.arange(k1p)) / n1
    g = np.concatenate([np.cos(psi) * wk[None, :], -np.sin(psi) * wk[None, :]], axis=1) / n

    def hl(a):
        a32 = jnp.asarray(a, F32)
        hi = a32.astype(BF16)
        lo = (a32 - hi.astype(F32)).astype(BF16)
        return hi, lo

    return dict(f1=hl(f1), f1_in=hl(f1[:, 0:n_in]), g=hl(g[0:n_out]), m_fwd=hl(m_fwd), m_inv=hl(m_inv),
                tw=jnp.asarray(tw, F32))


def _slab_row(k):
    return k * FFT_PITCH if isinstance(k, int) else pl.multiple_of(k * FFT_PITCH, 8)


def _load_slab(ref, k):
    r0 = _slab_row(k)
    return jnp.concatenate([ref[0, pl.ds(r0, FFT_N2), :], ref[1, pl.ds(r0, FFT_N2), :]], axis=1)


def _store_slab(ref, k, val):
    r0 = _slab_row(k)
    ref[0, pl.ds(r0, FFT_N2), :] = val[:, 0:128]
    ref[1, pl.ds(r0, FFT_N2), :] = val[:, 128:256]


FFT_GROUP = 8


def _cross_slab(src_refs, n_src, mat_h, mat_l, dst_refs, n_dst):
    def body(gidx, carry):
        cols = []
        for i in range(FFT_GROUP):
            i2 = gidx * FFT_GROUP + i
            parts = []
            for half in range(2):
                parts.append(jnp.concatenate(
                    [r[half, pl.ds(i2, n_src, stride=FFT_PITCH), :] for r in src_refs], axis=0))
            cols.append(jnp.concatenate(parts, axis=1))
        x = jnp.concatenate(cols, axis=1)
        y = _dot_const(mat_h, mat_l, x)
        for i in range(FFT_GROUP):
            i2 = gidx * FFT_GROUP + i
            for half in range(2):
                c0 = (2 * i + half) * 128
                for d, ref in enumerate(dst_refs):
                    ref[half, pl.ds(i2, n_dst, stride=FFT_PITCH), :] = y[d * n_dst:(d + 1) * n_dst, c0:c0 + 128]
        return carry

    lax.fori_loop(0, FFT_N2 // FFT_GROUP, body, 0)


def _twiddle(ar, ai, tw_ref, k, conj):
    r0 = pl.multiple_of(k * FFT_N2, FFT_N2)
    tr = tw_ref[0, pl.ds(r0, FFT_N2), :]
    ti = tw_ref[1, pl.ds(r0, FFT_N2), :]
    tr = jnp.concatenate([tr, tr], axis=1)
    ti = jnp.concatenate([ti, ti], axis=1)
    if conj:
        return ar * tr + ai * ti, ai * tr - ar * ti
    return ar * tr - ai * ti, ar * ti + ai * tr


def _filter_kernel(feat_ref, dec_ref, w1_ref, b1_ref, w2_ref, b2_ref, w3_ref, b3_ref, fq_ref,
                   f1h_ref, f1l_ref, mh_ref, ml_ref, tw_ref, out_ref, zbuf, are, aim, *, n1, k1p):
    half = n1 // 2

    u_ = FFT_SLAB_UNROLL
    rows = u_ * FFT_N2

    def fill(g, carry):
        r0 = pl.multiple_of(g * rows, rows)
        z = feat_ref[pl.ds(r0, rows), :]
        h = jnp.sin(fq_ref[0, 0:1, :] * (_dot_f32(z, w1_ref[0]) + b1_ref[0]))
        h = jnp.sin(fq_ref[0, 1:2, :] * (_dot_f32(h, w2_ref[0]) + b2_ref[0]))
        h = _dot_f32(h, w3_ref[0]) + b3_ref[0]
        hsel = jnp.where(g * u_ < half, h[:, 0:W_GROUP], h[:, W_GROUP:2 * W_GROUP])
        kern = hsel * dec_ref[pl.ds(r0, rows), :]
        for u in range(u_):
            _store_slab(zbuf, g * u_ + u, kern[u * FFT_N2:(u + 1) * FFT_N2])
        return carry

    lax.fori_loop(0, n1 // u_, fill, 0)
    _cross_slab([zbuf], n1, f1h_ref[...], f1l_ref[...], [are, aim], k1p)

    def slab(g, carry):
        cols = []
        for u in range(u_):
            k = g * u_ + u
            ar, ai = _twiddle(_load_slab(are, k), _load_slab(aim, k), tw_ref, k, conj=False)
            cols.append(jnp.concatenate([ar, ai], axis=0))
        x = _dot_const(mh_ref[...], ml_ref[...], jnp.concatenate(cols, axis=1))
        for u in range(u_):
            r0 = pl.multiple_of((g * u_ + u) * FFT_N2, FFT_N2)
            out_ref[0, 0, pl.ds(r0, FFT_N2), :] = x[0:FFT_N2, u * W_GROUP:(u + 1) * W_GROUP]
            out_ref[0, 1, pl.ds(r0, FFT_N2), :] = x[FFT_N2:2 * FFT_N2, u * W_GROUP:(u + 1) * W_GROUP]
        return carry

    lax.fori_loop(0, k1p // u_, slab, 0)


def _filter_tables(l_):
    pos = np.concatenate([np.arange(l_), l_ - np.arange(l_)]).astype(np.float32)
    pos[l_] = 0.0
    t = pos / np.float32(max(l_ - 1, 1))
    w = np.float32(2.0 * math.pi) * pos / np.float32(l_)
    f = np.linspace(1e-4, HYENA_BANDS - 1, HYENA_BANDS, dtype=np.float32)
    feat = np.zeros((2 * l_, 64), np.float32)
    feat[:, 0] = t
    feat[:, 1:1 + HYENA_BANDS] = np.cos(w[:, None] * f)
    feat[:, 1 + HYENA_BANDS:HYENA_EMB] = -np.sin(w[:, None] * f)
    deltas = np.abs(np.linspace(HYENA_MIN_DECAY, HYENA_MAX_DECAY, W_GROUP, dtype=np.float32))
    dec = np.exp(-t[:, None] * deltas[None, :]).astype(np.float32)
    dec[l_] = 0.0
    return jnp.asarray(feat), jnp.asarray(dec)


def _filter_spectra(l_, w1p, b1, w2, b2, w3, b3, freq, fc):
    n1 = 2 * l_ // FFT_N2
    plan = _FftPlan(n1)
    k1p = plan.k1p
    feat, dec = _filter_tables(l_)
    full = lambda shape: pl.BlockSpec(shape, lambda l: (0,) * len(shape))
    per = lambda shape: pl.BlockSpec((1,) + shape, lambda l: (l,) + (0,) * len(shape))
    kern = functools.partial(_filter_kernel, n1=n1, k1p=k1p)
    return pl.pallas_call(
        kern,
        grid=(DEPTH,),
        in_specs=[full((2 * l_, 64)), full((2 * l_, W_GROUP)),
                  per((64, HYENA_FFN)), per((1, HYENA_FFN)), per((HYENA_FFN, HYENA_FFN)), per((1, HYENA_FFN)),
                  per((HYENA_FFN, 2 * W_GROUP)), per((1, 2 * W_GROUP)), per((2, HYENA_FFN)),
                  full((2 * k1p, n1)), full((2 * k1p, n1)), full((128, 128)), full((128, 128)),
                  full((2, k1p * FFT_N2, 128))],
        out_specs=pl.BlockSpec((1, 2, k1p * FFT_N2, W_GROUP), lambda l: (l, 0, 0, 0)),
        out_shape=jax.ShapeDtypeStruct((DEPTH, 2, k1p * FFT_N2, W_GROUP), F32),
        scratch_shapes=[pltpu.VMEM((2, n1 * FFT_PITCH, 128), F32),
                        pltpu.VMEM((2, k1p * FFT_PITCH, 128), F32),
                        pltpu.VMEM((2, k1p * FFT_PITCH, 128), F32)],
        compiler_params=pltpu.CompilerParams(vmem_limit_bytes=VMEM_LIMIT),
        name="hyena_filter_spectrum_%d" % l_,
    )(feat, dec, w1p, b1.reshape(DEPTH, 1, HYENA_FFN), w2, b2.reshape(DEPTH, 1, HYENA_FFN),
      w3, b3.reshape(DEPTH, 1, 2 * W_GROUP), freq,
      fc["f1"][0], fc["f1"][1], fc["m_fwd"][0], fc["m_fwd"][1], fc["tw"])


def _hyena_segment(v_ref, x1_ref, x0_ref, sg_ref, cw_ref, cb_ref, d_ref, kf_ref, consts, scratch, out_ref,
                   *, row0, seg_len, n1):
    f1h, f1l, gh, gl, mfh, mfl, mih, mil, tw_ref = consts
    zbuf, are, aim, ybuf = scratch
    n_sig = seg_len // FFT_N2
    n_in = max(n_sig, 8)
    n_out = max(n_sig, 8)
    k1p = _FftPlan(n1).k1p
    row_i = lax.broadcasted_iota(jnp.int32, (FFT_N2, W_GROUP), 0)

    def conv3(ref, blk, jslab):
        r = pl.multiple_of(row0 + jslab * FFT_N2, FFT_N2)
        cur = ref[0, pl.ds(r, FFT_N2), :]
        pe = ref[0, pl.ds(jnp.maximum(r - 1, row0), 1), :] * (jslab > 0).astype(F32)
        ne = ref[0, pl.ds(jnp.minimum(r + FFT_N2, row0 + seg_len - 1), 1), :] * (jslab < n_sig - 1).astype(F32)
        prev = jnp.where(row_i == 0, pe, pltpu.roll(cur, 1, 0))
        nxt = jnp.where(row_i == FFT_N2 - 1, ne, pltpu.roll(cur, FFT_N2 - 1, 0))
        c0 = blk * W_GROUP
        return (cw_ref[0:1, c0:c0 + W_GROUP] * prev + cw_ref[1:2, c0:c0 + W_GROUP] * cur
                + cw_ref[2:3, c0:c0 + W_GROUP] * nxt + cb_ref[:, c0:c0 + W_GROUP])

    def pre(jslab, carry):
        _store_slab(zbuf, jslab, conv3(v_ref, 0, jslab) * conv3(x1_ref, 1, jslab))
        return carry

    lax.fori_loop(0, n_sig, pre, 0)
    for jz in range(n_sig, n_in):
        _store_slab(zbuf, jz, jnp.zeros((FFT_N2, W_GROUP), F32))

    _cross_slab([zbuf], n_in, f1h, f1l, [are, aim], k1p)

    u_ = FFT_SLAB_UNROLL

    def slab(g, carry):
        cols = []
        for u in range(u_):
            k = g * u_ + u
            ar, ai = _twiddle(_load_slab(are, k), _load_slab(aim, k), tw_ref, k, conj=False)
            cols.append(jnp.concatenate([ar, ai], axis=0))
        x = _dot_const(mfh, mfl, jnp.concatenate(cols, axis=1))
        cols = []
        for u in range(u_):
            xr = x[0:FFT_N2, u * W_GROUP:(u + 1) * W_GROUP]
            xi = x[FFT_N2:2 * FFT_N2, u * W_GROUP:(u + 1) * W_GROUP]
            r0 = pl.multiple_of((g * u_ + u) * FFT_N2, FFT_N2)
            kr = kf_ref[0, 0, pl.ds(r0, FFT_N2), :]
            ki = kf_ref[0, 1, pl.ds(r0, FFT_N2), :]
            cols.append(jnp.concatenate([xr * kr - xi * ki, xr * ki + xi * kr], axis=0))
        bm = _dot_const(mih, mil, jnp.concatenate(cols, axis=1))
        for u in range(u_):
            k = g * u_ + u
            br, bi = _twiddle(bm[0:FFT_N2, u * W_GROUP:(u + 1) * W_GROUP],
                              bm[FFT_N2:2 * FFT_N2, u * W_GROUP:(u + 1) * W_GROUP], tw_ref, k, conj=True)
            _store_slab(are, k, br)
            _store_slab(aim, k, bi)
        return carry

    lax.fori_loop(0, k1p // u_, slab, 0)
    _cross_slab([are, aim], k1p, gh, gl, [ybuf], n_out)

    def post(jslab, carry):
        r = pl.multiple_of(row0 + jslab * FFT_N2, FFT_N2)
        y = _load_slab(ybuf, jslab) + _load_slab(zbuf, jslab) * d_ref[...]
        out_ref[0, pl.ds(r, FFT_N2), :] = conv3(x0_ref, 2, jslab) * y * _silu(sg_ref[0, pl.ds(r, FFT_N2), :])
        return carry

    lax.fori_loop(0, n_sig, post, 0)


def _hyena_kernel(v_ref, x1_ref, x0_ref, sg_ref, cw_ref, cb_ref, d_ref, kfc_ref, kfl_ref,
                  c_f1h, c_f1l, c_gh, c_gl, c_tw, l_f1h, l_f1l, l_gh, l_gl, l_tw,
                  mfh_ref, mfl_ref, mih_ref, mil_ref, out_ref,
                  zc, arc, aic, yc, zl, arl, ail, yl, *, ctx_len, lat_len):
    mats = (mfh_ref[...], mfl_ref[...], mih_ref[...], mil_ref[...])
    seg = functools.partial(_hyena_segment, v_ref, x1_ref, x0_ref, sg_ref, cw_ref, cb_ref, d_ref)
    seg(kfc_ref, (c_f1h[...], c_f1l[...], c_gh[...], c_gl[...]) + mats + (c_tw,), (zc, arc, aic, yc), out_ref,
        row0=0, seg_len=ctx_len, n1=2 * ctx_len // FFT_N2)
    seg(kfl_ref, (l_f1h[...], l_f1l[...], l_gh[...], l_gl[...]) + mats + (l_tw,), (zl, arl, ail, yl), out_ref,
        row0=ctx_len, seg_len=lat_len, n1=2 * lat_len // FFT_N2)


def _hyena(p, conv_w, conv_b, d, kf_ctx, kf_lat, layer, fc_ctx, fc_lat, ctx_len):
    b_, lt, _ = p.shape
    lat_len = lt - ctx_len
    n1c, n1l = 2 * ctx_len // FFT_N2, 2 * lat_len // FFT_N2
    kc, kl = _FftPlan(n1c).k1p, _FftPlan(n1l).k1p
    blk = lambda j: pl.BlockSpec((1, lt, W_GROUP), lambda b: (b, 0, j))
    full = lambda a: pl.BlockSpec(a.shape, lambda b: (0,) * a.ndim)
    per_layer = lambda a: pl.BlockSpec((1,) + a.shape[1:], lambda b: (layer,) + (0,) * (a.ndim - 1))
    cst = [fc_ctx["f1_in"][0], fc_ctx["f1_in"][1], fc_ctx["g"][0], fc_ctx["g"][1], fc_ctx["tw"],
           fc_lat["f1_in"][0], fc_lat["f1_in"][1], fc_lat["g"][0], fc_lat["g"][1], fc_lat["tw"],
           fc_lat["m_fwd"][0], fc_lat["m_fwd"][1], fc_lat["m_inv"][0], fc_lat["m_inv"][1]]
    small = [conv_w, conv_b.reshape(1, 3 * W_GROUP), d.reshape(1, W_GROUP)]
    sbuf = lambda n: pltpu.VMEM((2, n * FFT_PITCH, 128), F32)
    kern = functools.partial(_hyena_kernel, ctx_len=ctx_len, lat_len=lat_len)
    return pl.pallas_call(
        kern,
        grid=(b_,),
        in_specs=([blk(HV), blk(HX1), blk(HX0), blk(HG)] + [full(a) for a in small]
                  + [per_layer(kf_ctx), per_layer(kf_lat)] + [full(a) for a in cst]),
        out_specs=pl.BlockSpec((1, lt, W_GROUP), lambda b: (b, 0, 0)),
        out_shape=jax.ShapeDtypeStruct((b_, lt, W_GROUP), F32),
        scratch_shapes=[sbuf(max(n1c // 2, 8)), sbuf(kc), sbuf(kc), sbuf(max(n1c // 2, 8)),
                        sbuf(n1l // 2), sbuf(kl), sbuf(kl), sbuf(n1l // 2)],
        compiler_params=pltpu.CompilerParams(vmem_limit_bytes=VMEM_LIMIT),
        name="hyena_mixer",
    )(p, p, p, p, *small, kf_ctx, kf_lat, *cst)


def _rope_tables(ctx_len, lat_len):
    pos = np.arange(lat_len)
    row = (pos // GRID_W).astype(np.float32)
    col = (pos % GRID_W).astype(np.float32)
    inv = (np.float32(ROPE_BASE) ** (-np.arange(ROPE_AXIS_FREQS, dtype=np.float32) / np.float32(ROPE_AXIS_FREQS)))
    ang = np.concatenate([row[:, None] * inv, col[:, None] * inv], axis=-1).astype(np.float32)
    cos, sin = np.cos(ang), np.sin(ang)
    cos_h = np.concatenate([cos, cos], axis=-1)
    sin_h = np.concatenate([-sin, sin], axis=-1)
    cos_l = np.tile(cos_h, (1, N_HEADS))
    sin_l = np.tile(sin_h, (1, N_HEADS))
    cos_t = np.concatenate([np.ones((ctx_len, W_GROUP), np.float32), cos_l], axis=0)
    sin_t = np.concatenate([np.zeros((ctx_len, W_GROUP), np.float32), sin_l], axis=0)
    return jnp.asarray(cos_t, F32), jnp.asarray(sin_t, F32)


def _tri(n):
    t = np.tril(np.ones((n, n), np.float32))
    return jnp.asarray(t, BF16), jnp.asarray(t.T, BF16)


def _layout_w_in(w_in):
    a0 = 0
    h0 = 5 * W_GROUP + 16
    g0 = h0 + 4 * W_GROUP
    d0 = g0 + 4 * W_GROUP + 2 * GLA_RANK
    kv = W_GROUP // 2
    k_cols = w_in[..., d0 + W_GROUP:d0 + W_GROUP + kv]
    v_cols = w_in[..., d0 + W_GROUP + kv:d0 + W_GROUP + 2 * kv]

    def expand_kv(c):
        return jnp.concatenate([c[..., 0:HEAD_DIM], c[..., 0:HEAD_DIM],
                                c[..., HEAD_DIM:2 * HEAD_DIM], c[..., HEAD_DIM:2 * HEAD_DIM]], axis=-1)

    small = jnp.concatenate([w_in[..., 5 * W_GROUP:5 * W_GROUP + 16],
                             w_in[..., g0 + 4 * W_GROUP:g0 + 4 * W_GROUP + 2 * GLA_RANK]], axis=-1)
    pad = jnp.zeros(w_in.shape[:-1] + (128 - 48,), w_in.dtype)
    wide = jnp.concatenate([
        w_in[..., a0:a0 + 5 * W_GROUP],
        w_in[..., h0:h0 + 4 * W_GROUP],
        w_in[..., g0:g0 + 4 * W_GROUP],
        w_in[..., d0:d0 + W_GROUP], expand_kv(k_cols), expand_kv(v_cols),
        w_in[..., d0 + W_GROUP + 2 * kv:d0 + 2 * W_GROUP + 2 * kv],
        small, pad], axis=-1)
    w_st = jnp.swapaxes(w_in[..., 5 * W_GROUP:5 * W_GROUP + 16], -1, -2)
    return wide.astype(BF16), w_st.astype(BF16)


def kernel(x, c, ctx, c_ctx, w_ada, b_ada, g_pre, g_post, w_in, mlstm_gate_b, mlstm_norm_g, hyena_conv_w,
           hyena_conv_b, hyena_w1, hyena_b1, hyena_w2, hyena_b2, hyena_w3, hyena_b3, hyena_freq, hyena_d,
           gla_w_alpha, gla_b_alpha, gla_norm_g, attn_sink, w_out):
    b_, lat_len, _ = x.shape
    ctx_len = ctx.shape[1]
    n_ctx_tiles = ctx_len // ROW_TILE

    tril_m, triu_m = _tri(MLSTM_CHUNK)
    tril_g, triu_g = _tri(GLA_CHUNK)
    bd = np.kron(np.eye(N_HEADS, dtype=np.float32), np.ones((HEAD_DIM, HEAD_DIM), np.float32))
    rope_cos, rope_sin = _rope_tables(ctx_len, lat_len)
    mones_m = np.kron(np.eye(N_HEADS, dtype=np.float32), np.ones((MLSTM_CHUNK, HEAD_DIM), np.float32))
    consts = dict(tril_m=tril_m, triu_m=triu_m, tril_g=tril_g, triu_g=triu_g, bd=jnp.asarray(bd, BF16),
                  mones_m=jnp.asarray(mones_m, BF16), rope_cos=rope_cos, rope_sin=rope_sin)
    fc_ctx = _fft_constants(2 * ctx_len // FFT_N2, max(ctx_len // FFT_N2, 8), max(ctx_len // FFT_N2, 8))
    fc_lat = _fft_constants(2 * lat_len // FFT_N2, lat_len // FFT_N2, lat_len // FFT_N2)

    cc = jnp.zeros((16, D_MODEL), F32).at[0:b_].set(c).at[b_].set(c_ctx)
    mod = _modulation(cc, w_ada, b_ada)
    mod_ctx = jnp.broadcast_to(mod[:, b_:b_ + 1], (DEPTH, b_, 3 * D_MODEL))
    mod_all = jnp.stack([mod_ctx, mod[:, 0:b_]], axis=2).reshape(DEPTH, 2 * b_, 1, 3 * D_MODEL)

    w_p, w_st = _layout_w_in(w_in)
    w_o = w_out.astype(BF16)
    w1p = jnp.zeros((DEPTH, 64, HYENA_FFN), F32).at[:, 0:HYENA_EMB].set(hyena_w1)
    filt = (w1p, hyena_b1, hyena_w2, hyena_b2, hyena_w3, hyena_b3, hyena_freq)
    kf_ctx = _filter_spectra(ctx_len, *filt, fc_ctx)
    kf_lat = _filter_spectra(lat_len, *filt, fc_lat)

    xs = jnp.concatenate([ctx, x], axis=1)
    for l in range(DEPTH):
        last = l == DEPTH - 1
        p, gt = _inproj(xs, mod_all, l, g_pre[l], w_p, w_st, n_ctx_tiles)
        a = _mlstm(p, gt, mlstm_gate_b[l], mlstm_norm_g[l], consts, ctx_len)
        hy = _hyena(p, hyena_conv_w[l], hyena_conv_b[l], hyena_d[l], kf_ctx, kf_lat, l, fc_ctx, fc_lat, ctx_len)
        g = _gla(p, gla_w_alpha[l], gla_b_alpha[l], gla_norm_g[l], consts, ctx_len)
        d = _attn(p, attn_sink[l], consts, ctx_len)
        xs = _outproj((a, hy, g, d), w_o, l, g_post[l], mod_all, xs, n_ctx_tiles, skip_ctx=last)
    return xs
```

```python
import functools
import math

import numpy as np
import jax
import jax.numpy as jnp
from jax import lax
from jax.experimental import pallas as pl
from jax.experimental.pallas import tpu as pltpu

F32 = jnp.float32
BF16 = jnp.bfloat16

D_MODEL = 1024
DEPTH = 4
GRID_W = 64
W_GROUP = 256
HEAD_DIM = 64
N_HEADS = 4
ATT_KV_HEADS = 2
WINDOW = 128
ATT_BLOCK = 128
GLA_RANK = 16
GLA_NORMALIZER = 16.0
HYENA_BANDS = 16
HYENA_EMB = 1 + 2 * HYENA_BANDS
HYENA_FFN = 64
HYENA_MIN_DECAY = math.log(1e-2) / 1.5
HYENA_MAX_DECAY = math.log(1e-2) / 0.3
ROPE_BASE = 10000.0
ROPE_AXIS_FREQS = HEAD_DIM // 4
EPS = 1e-6

N_WIDE_BLOCKS = 17
SMALL_COL0 = N_WIDE_BLOCKS * W_GROUP
N_PROJ = SMALL_COL0 + 128
SMALL_BLOCK = SMALL_COL0 // 128
(MQ, MK, MV, MO, MG, HV, HX1, HX0, HG, GQ, GK, GV, GG, AQ, AK, AV, AG) = range(N_WIDE_BLOCKS)

GATE_ROWS = 32
ROW_TILE = 256
MLSTM_CHUNK = 256
GLA_CHUNK = 128
ATT_STEP_BLOCKS = 3
FFT_N2 = 64
FFT_PITCH = 72
FFT_SLAB_UNROLL = 4
VMEM_LIMIT = 56 * 1024 * 1024


def _dot(a, b):
    return jnp.dot(a, b, preferred_element_type=F32)


def _dot_nt(a, b):
    return lax.dot_general(a, b, (((1,), (1,)), ((), ())), preferred_element_type=F32)


def _dot_tn(a, b):
    return lax.dot_general(a, b, (((0,), (0,)), ((), ())), preferred_element_type=F32)


def _split2(x):
    hi = x.astype(BF16)
    lo = (x - hi.astype(F32)).astype(BF16)
    return hi, lo


def _split3(x):
    h1 = x.astype(BF16)
    r1 = x - h1.astype(F32)
    h2 = r1.astype(BF16)
    h3 = (r1 - h2.astype(F32)).astype(BF16)
    return h1, h2, h3


def _dot_const(ch, cl, x):
    xh, xl = _split2(x)
    return _dot(ch, xh) + _dot(ch, xl) + _dot(cl, xh)


def _dot_f32(a, b):
    ah, al = _split2(a)
    bh, bl = _split2(b)
    return _dot(ah, bh) + _dot(ah, bl) + _dot(al, bh)


def _dot_exact_lhs(c, x):
    x1, x2, x3 = _split3(x)
    return _dot(c, x1) + _dot(c, x2) + _dot(c, x3)


def _dot_exact_rhs(x, c):
    x1, x2, x3 = _split3(x)
    return _dot(x1, c) + _dot(x2, c) + _dot(x3, c)


def _log_sigmoid(x):
    return jnp.minimum(x, 0.0) - jnp.log(1.0 + jnp.exp(-jnp.abs(x)))


def _silu(x):
    return x * jax.nn.sigmoid(x)


def _head_masks():
    lane = lax.broadcasted_iota(jnp.int32, (1, W_GROUP), 1) // HEAD_DIM
    return [(lane == h).astype(F32) for h in range(N_HEADS)]


def _block_diag_mask():
    r = lax.broadcasted_iota(jnp.int32, (W_GROUP, W_GROUP), 0) // HEAD_DIM
    c = lax.broadcasted_iota(jnp.int32, (W_GROUP, W_GROUP), 1) // HEAD_DIM
    return (r == c).astype(F32)


def _spread(cols):
    lane = lax.broadcasted_iota(jnp.int32, (1, W_GROUP), 1)
    out = cols[N_HEADS - 1]
    for h in range(N_HEADS - 2, -1, -1):
        out = jnp.where(lane < (h + 1) * HEAD_DIM, cols[h], out)
    return out


def _head_rms_norm(h, bd_bf16, g_row):
    sq = h * h
    sh, sl = _split2(sq)
    ms = (_dot(sh, bd_bf16) + _dot(sl, bd_bf16)) * (1.0 / HEAD_DIM)
    return h * lax.rsqrt(ms + EPS) * g_row


def _mod_kernel(c_ref, w_ref, b_ref, o_ref):
    s = _silu(c_ref[...])
    o_ref[0] = _dot_f32(s, w_ref[0]) + b_ref[0]


def _modulation(cc, w_ada, b_ada):
    nblk = 3 * D_MODEL // 1024
    return pl.pallas_call(
        _mod_kernel,
        grid=(DEPTH, nblk),
        in_specs=[
            pl.BlockSpec((16, D_MODEL), lambda l, j: (0, 0)),
            pl.BlockSpec((1, D_MODEL, 1024), lambda l, j: (l, 0, j)),
            pl.BlockSpec((1, 1, 1024), lambda l, j: (l, 0, j)),
        ],
        out_specs=pl.BlockSpec((1, 16, 1024), lambda l, j: (l, 0, j)),
        out_shape=jax.ShapeDtypeStruct((DEPTH, 16, 3 * D_MODEL), F32),
        compiler_params=pltpu.CompilerParams(vmem_limit_bytes=VMEM_LIMIT),
        name="adaln_modulation",
    )(cc, w_ada, b_ada.reshape(DEPTH, 1, 3 * D_MODEL))


def _inproj_kernel(x_ref, mod_ref, g_ref, w_ref, wst_ref, p_ref, gt_ref):
    x = x_ref[0]
    ms = jnp.mean(x * x, axis=-1, keepdims=True)
    y = x * lax.rsqrt(ms + EPS) * g_ref[...]
    sh = mod_ref[0, 0, :, 0:D_MODEL]
    sc = mod_ref[0, 0, :, D_MODEL:2 * D_MODEL]
    hb = (y * (1.0 + sc) + sh).astype(BF16)
    p_ref[0] = _dot(hb, w_ref[0])
    gt_ref[0] = _dot_nt(wst_ref[0], hb)


def _inproj(xs, mod_all, layer, g_pre, w_p, w_st, n_ctx_tiles):
    b_, lt, _ = xs.shape
    nt = lt // ROW_TILE
    return pl.pallas_call(
        _inproj_kernel,
        grid=(b_, nt),
        in_specs=[
            pl.BlockSpec((1, ROW_TILE, D_MODEL), lambda b, i: (b, i, 0)),
            pl.BlockSpec((1, 1, 1, 3 * D_MODEL),
                         lambda b, i: (layer, 2 * b + jnp.where(i >= n_ctx_tiles, 1, 0), 0, 0)),
            pl.BlockSpec((1, D_MODEL), lambda b, i: (0, 0)),
            pl.BlockSpec((1, D_MODEL, N_PROJ), lambda b, i: (layer, 0, 0)),
            pl.BlockSpec((1, GATE_ROWS, D_MODEL), lambda b, i: (layer, 0, 0)),
        ],
        out_specs=[
            pl.BlockSpec((1, ROW_TILE, N_PROJ), lambda b, i: (b, i, 0)),
            pl.BlockSpec((1, GATE_ROWS, ROW_TILE), lambda b, i: (b, 0, i)),
        ],
        out_shape=[
            jax.ShapeDtypeStruct((b_, lt, N_PROJ), F32),
            jax.ShapeDtypeStruct((b_, GATE_ROWS, lt), F32),
        ],
        compiler_params=pltpu.CompilerParams(vmem_limit_bytes=VMEM_LIMIT),
        name="inproj",
    )(xs, mod_all, g_pre.reshape(1, D_MODEL), w_p, w_st)


def _outproj_kernel(a_ref, h_ref, g_ref, d_ref, w_ref, gp_ref, mod_ref, x_ref, o_ref):
    mix = jnp.concatenate([a_ref[0].astype(BF16), h_ref[0].astype(BF16),
                           g_ref[0].astype(BF16), d_ref[0].astype(BF16)], axis=1)
    y = _dot(mix, w_ref[0])
    ms = jnp.mean(y * y, axis=-1, keepdims=True)
    yn = y * lax.rsqrt(ms + EPS) * gp_ref[...]
    gt = mod_ref[0, 0, :, 2 * D_MODEL:3 * D_MODEL]
    o_ref[0] = x_ref[0] + gt * yn


def _outproj(mix, w_o, layer, g_post, mod_all, xs, n_ctx_tiles, skip_ctx):
    b_, lt, _ = xs.shape
    nt = lt // ROW_TILE
    t0 = n_ctx_tiles if skip_ctx else 0
    rows = lambda b, i: (b, i + t0, 0)
    return pl.pallas_call(
        _outproj_kernel,
        grid=(b_, nt - t0),
        in_specs=[pl.BlockSpec((1, ROW_TILE, W_GROUP), rows)] * 4 + [
            pl.BlockSpec((1, D_MODEL, D_MODEL), lambda b, i: (layer, 0, 0)),
            pl.BlockSpec((1, D_MODEL), lambda b, i: (0, 0)),
            pl.BlockSpec((1, 1, 1, 3 * D_MODEL),
                         lambda b, i: (layer, 2 * b + jnp.where(i + t0 >= n_ctx_tiles, 1, 0), 0, 0)),
            pl.BlockSpec((1, ROW_TILE, D_MODEL), rows),
        ],
        out_specs=pl.BlockSpec((1, ROW_TILE, D_MODEL), lambda b, i: (b, i, 0)),
        out_shape=jax.ShapeDtypeStruct((b_, lt - t0 * ROW_TILE, D_MODEL), F32),
        compiler_params=pltpu.CompilerParams(vmem_limit_bytes=VMEM_LIMIT),
        name="outproj",
    )(*mix, w_o, g_post.reshape(1, D_MODEL), mod_all, xs)


def _interleave(*gens):
    live = list(gens)
    while live:
        live = [g for g in live if next(g, StopIteration) is not StopIteration]


def _scan_chunk(s, reverse, n_ctx, n_tot):
    if not reverse:
        return s
    return jnp.where(s < n_ctx, n_ctx - 1 - s, n_tot - 1 - (s - n_ctx))


def _mlstm_kernel(q_ref, k_ref, v_ref, o_ref, sg_ref, gt_ref, brow_ref, ng_ref,
                  bd_ref, mones_ref, eall_ref, out_ref, hf_ref, hb_ref, c_ref, n_ref, pt_ref, r8_ref, ab_ref,
                  *, chunk, n_ctx, n_tot):
    t_ = chunk
    masks = _head_masks()
    masks_bf = [m.astype(BF16) for m in masks]
    bdm = _block_diag_mask()
    row_i = lax.broadcasted_iota(jnp.int32, (t_, t_), 0)
    col_i = lax.broadcasted_iota(jnp.int32, (t_, t_), 1)
    lane_a = lax.broadcasted_iota(jnp.int32, (8 * n_tot, t_), 1)
    live = (lax.broadcasted_iota(jnp.int32, (8 * n_tot, t_), 0) % 8) < N_HEADS
    c_ref[...] = jnp.zeros_like(c_ref)
    n_ref[...] = jnp.zeros_like(n_ref)

    def scan(x, reverse, op):
        k = 1
        while k < t_:
            if reverse:
                x = jnp.where(lane_a < t_ - k, op(x, pltpu.roll(x, t_ - k, 1)), x)
            else:
                x = jnp.where(lane_a >= k, op(x, pltpu.roll(x, k, 1)), x)
            k *= 2
        return x

    def gates(reverse):
        d_ = 1 if reverse else 0
        end = 0 if reverse else t_ - 1
        order = list(range(n_tot))
        if reverse:
            order = order[:n_ctx][::-1] + order[n_ctx:][::-1]

        def rows_of(g):
            r_ = slice(8 * g, 8 * g + 8)
            return jnp.concatenate([gt_ref[0, r_, c * t_:(c + 1) * t_] for c in range(n_tot)], axis=0) + jnp.concatenate(
                [brow_ref[r_, :]] * n_tot, axis=0)

        i_all = jnp.where(live, rows_of(2 * d_), 0.0)
        lf_all = jnp.where(live, _log_sigmoid(rows_of(2 * d_ + 1)), 0.0)
        b_all = scan(lf_all, reverse, jnp.add)
        r_all = i_all - b_all
        cm_all = scan(r_all, reverse, jnp.maximum)
        yield
        m_prev = jnp.zeros((8, 1), F32)
        for c in order:
            cols = slice(c * t_, (c + 1) * t_)
            blk8 = slice(8 * c, 8 * c + 8)
            i8, b8, r8 = i_all[blk8], b_all[blk8], r_all[blk8]
            inter = b8 + m_prev
            m_t = jnp.maximum(inter, b8 + cm_all[blk8])
            b_end = b8[:, end:end + 1]
            g8 = b_end - b8 + i8
            m_new = jnp.maximum(b_end + m_prev, jnp.max(g8, axis=1, keepdims=True))
            a_prev = jnp.exp(b_end + m_prev - m_new)
            packed = jnp.concatenate([b8 - m_t, jnp.exp(inter - m_t), jnp.exp(g8 - m_new), jnp.exp(-m_t),
                                      jnp.zeros((128 - 32, t_), F32)], axis=0)
            pt_ref[d_, cols, :] = packed.T
            r8_ref[d_, :, cols] = r8
            a_b = a_prev[0:1] * masks[0]
            for h in range(1, N_HEADS):
                a_b = a_b + a_prev[h:h + 1] * masks[h]
            ab_ref[d_, c] = a_b
            m_prev = m_new
            yield

    _interleave(gates(False), gates(True))

    def step(reverse, c_idx, h_ref):
        d_ = 1 if reverse else 0
        valid = (col_i >= row_i) if reverse else (col_i <= row_i)
        r0 = pl.multiple_of(c_idx * t_, t_)
        q = q_ref[0, pl.ds(r0, t_), :]
        k = k_ref[0, pl.ds(r0, t_), :] * (HEAD_DIM ** -0.5)
        qb = q.astype(BF16)
        kb = k.astype(BF16)
        vb = v_ref[0, pl.ds(r0, t_), :].astype(BF16)
        pt = pt_ref[d_, pl.ds(r0, t_), :]
        r8 = r8_ref[d_, :, pl.ds(r0, t_)]
        c_prev = c_ref[d_]
        n_prev = n_ref[d_]
        spread = _dot(pt.astype(BF16), eall_ref[...])
        s_cat = []
        for h in range(N_HEADS):
            w_intra = jnp.where(valid, jnp.exp(pt[:, h:h + 1] + r8[h:h + 1, :]), 0.0)
            s_cat.append((_dot_nt(qb * masks_bf[h], kb) * w_intra).astype(BF16))
        yield
        v_stack = jnp.concatenate([jnp.concatenate([vb * masks_bf[h] for h in range(N_HEADS)], axis=0),
                                   mones_ref[...]], axis=1)
        acc = _dot(jnp.concatenate(s_cat, axis=1), v_stack)
        qc = _dot(qb, c_prev.astype(BF16))
        qn = _dot((q * n_prev).astype(BF16), bd_ref[...])
        yield
        wi_b = spread[:, 0:W_GROUP]
        num = wi_b * qc + acc[:, 0:W_GROUP]
        den = wi_b * qn + acc[:, W_GROUP:2 * W_GROUP]
        h_ref[pl.ds(r0, t_), :] = num / jnp.maximum(jnp.abs(den), spread[:, 2 * W_GROUP:3 * W_GROUP])
        a_b = ab_ref[d_, c_idx]
        kw = k * spread[:, W_GROUP:2 * W_GROUP]
        c_ref[d_] = a_b * c_prev + bdm * _dot_tn(kw.astype(BF16), vb)
        n_ref[d_] = a_b * n_prev + jnp.sum(kw, axis=0, keepdims=True)

    def body(s, carry):
        _interleave(step(False, _scan_chunk(s, False, n_ctx, n_tot), hf_ref),
                    step(True, _scan_chunk(s, True, n_ctx, n_tot), hb_ref))
        return carry

    lax.fori_loop(0, n_tot, body, 0)

    def finish(c_idx, carry):
        r0 = pl.multiple_of(c_idx * t_, t_)
        hsum = hf_ref[pl.ds(r0, t_), :] + hb_ref[pl.ds(r0, t_), :]
        hsum = hsum * jax.nn.sigmoid(o_ref[0, pl.ds(r0, t_), :])
        hn = _head_rms_norm(hsum, bd_ref[...], ng_ref[...])
        out_ref[0, pl.ds(r0, t_), :] = hn * _silu(sg_ref[0, pl.ds(r0, t_), :])
        return carry

    lax.fori_loop(0, n_tot, finish, 0)


def _mlstm(p, gt, gate_b, norm_g, consts, ctx_len):
    b_, lt, _ = p.shape
    t_ = MLSTM_CHUNK
    brow = jnp.pad(gate_b.reshape(4, N_HEADS), ((0, 0), (0, 8 - N_HEADS))).reshape(GATE_ROWS, 1)
    blk = lambda j: pl.BlockSpec((1, lt, W_GROUP), lambda b: (b, 0, j))
    full = lambda shape: pl.BlockSpec(shape, lambda b: (0,) * len(shape))
    kern = functools.partial(_mlstm_kernel, chunk=t_, n_ctx=ctx_len // t_, n_tot=lt // t_)
    return pl.pallas_call(
        kern,
        grid=(b_,),
        in_specs=[blk(MQ), blk(MK), blk(MV), blk(MO), blk(MG),
                  pl.BlockSpec((1, GATE_ROWS, lt), lambda b: (b, 0, 0)),
                  full((GATE_ROWS, 1)), full((1, W_GROUP)),
                  full((W_GROUP, W_GROUP)), full((N_HEADS * t_, W_GROUP)), full((128, 3 * W_GROUP))],
        out_specs=pl.BlockSpec((1, lt, W_GROUP), lambda b: (b, 0, 0)),
        out_shape=jax.ShapeDtypeStruct((b_, lt, W_GROUP), F32),
        scratch_shapes=[pltpu.VMEM((lt, W_GROUP), F32), pltpu.VMEM((lt, W_GROUP), F32),
                        pltpu.VMEM((2, W_GROUP, W_GROUP), F32), pltpu.VMEM((2, 1, W_GROUP), F32),
                        pltpu.VMEM((2, lt, 128), F32), pltpu.VMEM((2, 8, lt), F32),
                        pltpu.VMEM((2, lt // t_, 1, W_GROUP), F32)],
        compiler_params=pltpu.CompilerParams(vmem_limit_bytes=VMEM_LIMIT),
        name="mlstm_mixer",
    )(p, p, p, p, p, gt, brow, norm_g.reshape(1, W_GROUP),
      consts["bd"], consts["mones_m"], consts["eall_m"])


def _gla_kernel(q_ref, k_ref, v_ref, sg_ref, sm_ref, wa_ref, ba_ref, ng_ref, tril_ref, triu_ref, bd_ref,
                out_ref, hf_ref, hb_ref, st_ref, *, chunk, n_ctx, n_tot):
    t_ = chunk
    masks_bf = [m.astype(BF16) for m in _head_masks()]
    bdm = _block_diag_mask()
    row_i = lax.broadcasted_iota(jnp.int32, (t_, t_), 0)
    col_i = lax.broadcasted_iota(jnp.int32, (t_, t_), 1)
    st_ref[...] = jnp.zeros_like(st_ref)

    def step(reverse, c_idx, h_ref):
        d_ = 1 if reverse else 0
        tri_c = triu_ref[...] if reverse else tril_ref[...]
        valid = (col_i >= row_i) if reverse else (col_i <= row_i)
        end_row = 0 if reverse else t_ - 1
        r0 = pl.multiple_of(c_idx * t_, t_)
        q = q_ref[0, pl.ds(r0, t_), :] * (HEAD_DIM ** -0.5)
        k = k_ref[0, pl.ds(r0, t_), :]
        vb = v_ref[0, pl.ds(r0, t_), :].astype(BF16)
        sm = sm_ref[0, pl.ds(r0, t_), :]
        la = _log_sigmoid(_dot_f32(sm, wa_ref[d_]) + ba_ref[d_]) * (1.0 / GLA_NORMALIZER)
        yield
        bcum = _dot_exact_lhs(tri_c, la)
        yield
        b_end = bcum[end_row:end_row + 1, :]
        qdb = (q * jnp.exp(bcum)).astype(BF16)
        kd = (k * jnp.exp(-bcum)).astype(BF16)
        st_prev = st_ref[d_]
        att = [jnp.where(valid, _dot_nt(qdb * masks_bf[h], kd), 0.0).astype(BF16) for h in range(N_HEADS)]
        yield
        v_stack = jnp.concatenate([vb * masks_bf[h] for h in range(N_HEADS)], axis=0)
        h_ref[pl.ds(r0, t_), :] = (_dot_nt(qdb, st_prev.astype(BF16))
                                   + _dot(jnp.concatenate(att, axis=1), v_stack))
        kdec = (k * jnp.exp(b_end - bcum)).astype(BF16)
        st_ref[d_] = st_prev * jnp.exp(b_end) + bdm * _dot_tn(vb, kdec)

    def body(s, carry):
        _interleave(step(False, _scan_chunk(s, False, n_ctx, n_tot), hf_ref),
                    step(True, _scan_chunk(s, True, n_ctx, n_tot), hb_ref))
        return carry

    lax.fori_loop(0, n_tot, body, 0)

    def finish(c_idx, carry):
        r0 = pl.multiple_of(c_idx * t_, t_)
        hsum = hf_ref[pl.ds(r0, t_), :] + hb_ref[pl.ds(r0, t_), :]
        hn = _head_rms_norm(hsum, bd_ref[...], ng_ref[...])
        out_ref[0, pl.ds(r0, t_), :] = hn * _silu(sg_ref[0, pl.ds(r0, t_), :])
        return carry

    lax.fori_loop(0, n_tot, finish, 0)


def _gla(p, w_alpha, b_alpha, norm_g, consts, ctx_len):
    b_, lt, _ = p.shape
    t_ = GLA_CHUNK
    wa = jnp.zeros((2, 128, W_GROUP), F32)
    wa = wa.at[0, 16:32].set(w_alpha[0]).at[1, 32:48].set(w_alpha[1])
    blk = lambda j: pl.BlockSpec((1, lt, W_GROUP), lambda b: (b, 0, j))
    full = lambda shape: pl.BlockSpec(shape, lambda b: (0,) * len(shape))
    kern = functools.partial(_gla_kernel, chunk=t_, n_ctx=ctx_len // t_, n_tot=lt // t_)
    return pl.pallas_call(
        kern,
        grid=(b_,),
        in_specs=[blk(GQ), blk(GK), blk(GV), blk(GG),
                  pl.BlockSpec((1, lt, 128), lambda b: (b, 0, SMALL_BLOCK)),
                  full((2, 128, W_GROUP)), full((2, 1, W_GROUP)), full((1, W_GROUP)),
                  full((t_, t_)), full((t_, t_)), full((W_GROUP, W_GROUP))],
        out_specs=pl.BlockSpec((1, lt, W_GROUP), lambda b: (b, 0, 0)),
        out_shape=jax.ShapeDtypeStruct((b_, lt, W_GROUP), F32),
        scratch_shapes=[pltpu.VMEM((lt, W_GROUP), F32), pltpu.VMEM((lt, W_GROUP), F32),
                        pltpu.VMEM((2, W_GROUP, W_GROUP), F32)],
        compiler_params=pltpu.CompilerParams(vmem_limit_bytes=VMEM_LIMIT),
        name="gla_mixer",
    )(p, p, p, p, p, wa, b_alpha.reshape(2, 1, W_GROUP), norm_g.reshape(1, W_GROUP),
      consts["tril_g"], consts["triu_g"], consts["bd"])


def _attn_kernel(q_ref, k_ref, v_ref, sg_ref, cos_ref, sin_ref, sink_ref, pswap_ref, out_ref,
                 kb_ref, vb_ref, *, ctx_len, lat_len):
    g = pl.program_id(1)
    lt = ctx_len + lat_len
    n_ctx_blk = ctx_len // ATT_BLOCK
    nwin = 3 * ATT_BLOCK
    masks = _head_masks()
    log2e = math.log2(math.e)
    ones_lane = (lax.broadcasted_iota(jnp.int32, (1, W_GROUP), 1) % (2 * HEAD_DIM)) >= HEAD_DIM

    def rope(x, r0, n):
        swapped = _dot(x.astype(BF16), pswap_ref[...])
        return x * cos_ref[pl.ds(r0, n), :] + swapped * sin_ref[pl.ds(r0, n), :]

    @pl.when(g == 0)
    def _():
        def fill(i, carry):
            r0 = pl.multiple_of(i * ROW_TILE, ROW_TILE)
            kb_ref[pl.ds(r0, ROW_TILE), :] = rope(k_ref[0, pl.ds(r0, ROW_TILE), :], r0, ROW_TILE).astype(BF16)
            vb_ref[pl.ds(r0, ROW_TILE), :] = jnp.where(ones_lane, 1.0, v_ref[0, pl.ds(r0, ROW_TILE), :]).astype(BF16)
            return carry
        lax.fori_loop(0, lt // ROW_TILE, fill, 0)

    k_ctx = kb_ref[0:ctx_len, :]
    v_ctx = vb_ref[0:ctx_len, :]
    neg_inf = jnp.float32(-jnp.inf)
    rel0 = (lax.broadcasted_iota(jnp.int32, (ATT_BLOCK, nwin), 1)
            - lax.broadcasted_iota(jnp.int32, (ATT_BLOCK, nwin), 0))

    def block(u):
        j = g * ATT_STEP_BLOCKS + u
        rows = slice(u * ATT_BLOCK, (u + 1) * ATT_BLOCK)
        is_lat = j >= n_ctx_blk
        i_lat = jnp.maximum(j - n_ctx_blk, 0)
        w0 = pl.multiple_of(ctx_len + jnp.clip((i_lat - 1) * ATT_BLOCK, 0, lat_len - nwin), ATT_BLOCK)
        q0 = pl.multiple_of(j * ATT_BLOCK, ATT_BLOCK)
        q = rope(q_ref[0, rows, :], q0, ATT_BLOCK) * (HEAD_DIM ** -0.5 * log2e)
        k_win = kb_ref[pl.ds(w0, nwin), :]
        v_win = vb_ref[pl.ds(w0, nwin), :]
        band = is_lat & (jnp.abs(rel0 + (w0 - q0)) <= WINDOW)
        q_stack = jnp.concatenate([(q * masks[h]).astype(BF16) for h in range(N_HEADS)], axis=0)
        band4 = jnp.concatenate([band] * N_HEADS, axis=0)
        sink = jnp.concatenate([jnp.broadcast_to(sink_ref[h] * log2e, (ATT_BLOCK, 128)) for h in range(N_HEADS)],
                               axis=0)
        s_loc = jnp.where(band4, _dot_nt(q_stack, k_win), neg_inf)
        s_ctx = _dot_nt(q_stack, k_ctx)
        yield
        m = jnp.maximum(jnp.max(s_loc, axis=-1, keepdims=True), jnp.max(s_ctx, axis=-1, keepdims=True))
        m128 = jnp.maximum(jnp.broadcast_to(m, (N_HEADS * ATT_BLOCK, 128)), sink)
        p_loc = jnp.exp2(s_loc - jnp.concatenate([m128] * (nwin // 128), axis=1)).astype(BF16)
        p_ctx = jnp.exp2(s_ctx - jnp.concatenate([m128] * (ctx_len // 128), axis=1)).astype(BF16)
        yield
        o_all = _dot(p_loc, v_win) + _dot(p_ctx, v_ctx)
        e_sink = jnp.exp2(sink - m128)
        o = jnp.zeros((ATT_BLOCK, W_GROUP), F32)
        for h in range(N_HEADS):
            hr = slice(h * ATT_BLOCK, (h + 1) * ATT_BLOCK)
            if h % 2 == 0:
                num, den = o_all[hr], pltpu.roll(o_all[hr], W_GROUP - HEAD_DIM, 1)
            else:
                num, den = pltpu.roll(o_all[hr], HEAD_DIM, 1), o_all[hr]
            den = den + jnp.concatenate([e_sink[hr], e_sink[hr]], axis=1)
            o = jnp.where(masks[h] > 0.0, num / den, o)
        out_ref[0, rows, :] = o * _silu(sg_ref[0, rows, :])

    _interleave(*[block(u) for u in range(ATT_STEP_BLOCKS)])


def _attn(p, sink, consts, ctx_len):
    b_, lt, _ = p.shape
    step_rows = ATT_STEP_BLOCKS * ATT_BLOCK
    nb = lt // step_rows
    kern = functools.partial(_attn_kernel, ctx_len=ctx_len, lat_len=lt - ctx_len)
    return pl.pallas_call(
        kern,
        grid=(b_, nb),
        in_specs=[pl.BlockSpec((1, step_rows, W_GROUP), lambda b, j: (b, j, AQ)),
                  pl.BlockSpec((1, lt, W_GROUP), lambda b, j: (b, 0, AK)),
                  pl.BlockSpec((1, lt, W_GROUP), lambda b, j: (b, 0, AV)),
                  pl.BlockSpec((1, step_rows, W_GROUP), lambda b, j: (b, j, AG)),
                  pl.BlockSpec((lt, W_GROUP), lambda b, j: (0, 0)),
                  pl.BlockSpec((lt, W_GROUP), lambda b, j: (0, 0)),
                  pl.BlockSpec((N_HEADS, 1, 1), lambda b, j: (0, 0, 0)),
                  pl.BlockSpec((W_GROUP, W_GROUP), lambda b, j: (0, 0))],
        out_specs=pl.BlockSpec((1, step_rows, W_GROUP), lambda b, j: (b, j, 0)),
        out_shape=jax.ShapeDtypeStruct((b_, lt, W_GROUP), F32),
        scratch_shapes=[pltpu.VMEM((lt, W_GROUP), BF16), pltpu.VMEM((lt, W_GROUP), BF16)],
        compiler_params=pltpu.CompilerParams(vmem_limit_bytes=VMEM_LIMIT,
                                             dimension_semantics=("arbitrary", "arbitrary")),
        name="window_attention",
    )(p, p, p, p, consts["rope_cos"], consts["rope_sin"], sink.reshape(N_HEADS, 1, 1), consts["pswap"])


class _FftPlan:
    def __init__(self, n1):
        self.n1 = n1
        self.n = n1 * FFT_N2
        self.k1p = -(-(n1 // 2 + 1) // 8) * 8


def _fft_constants(n1, n_in, n_out):
    plan = _FftPlan(n1)
    n, k1p, half = plan.n, plan.k1p, n1 // 2
    k1 = np.arange(k1p)[:, None].astype(np.float64)
    live = (np.arange(k1p) <= half)[:, None]
    i1 = np.arange(n1)[None, :].astype(np.float64)
    ang = 2.0 * np.pi * k1 * i1 / n1
    f1 = np.concatenate([np.where(live, np.cos(ang), 0.0), np.where(live, -np.sin(ang), 0.0)], axis=0)
    i2 = np.arange(FFT_N2)[None, :].astype(np.float64)
    phi = 2.0 * np.pi * k1 * i2 / n
    tw = np.stack([np.cos(phi), -np.sin(phi)], axis=0)
    tw = np.broadcast_to(tw.reshape(2, k1p * FFT_N2, 1), (2, k1p * FFT_N2, 128))
    th = 2.0 * np.pi * np.outer(np.arange(FFT_N2), np.arange(FFT_N2)) / FFT_N2
    c2, s2 = np.cos(th), np.sin(th)
    m_fwd = np.block([[c2, s2], [-s2, c2]])
    m_inv = np.block([[c2, -s2], [s2, c2]])
    wk = np.where(np.arange(k1p) <= half, 2.0, 0.0)
    wk[0] = 1.0
    wk[half] = 1.0
    psi = 2.0 * np.pi * np.outer(np.arange(n1), np.arange(k1p)) / n1
    g = np.concatenate([np.cos(psi) * wk[None, :], -np.sin(psi) * wk[None, :]], axis=1) / n

    def hl(a):
        a32 = jnp.asarray(a, F32)
        hi = a32.astype(BF16)
        lo = (a32 - hi.astype(F32)).astype(BF16)
        return hi, lo

    return dict(f1=hl(f1), f1_in=hl(f1[:, 0:n_in]), g=hl(g[0:n_out]), m_fwd=hl(m_fwd), m_inv=hl(m_inv),
                tw=jnp.asarray(tw, F32))


def _slab_row(k):
    return k * FFT_PITCH if isinstance(k, int) else pl.multiple_of(k * FFT_PITCH, 8)


def _load_slab(ref, k):
    r0 = _slab_row(k)
    return jnp.concatenate([ref[0, pl.ds(r0, FFT_N2), :], ref[1, pl.ds(r0, FFT_N2), :]], axis=1)


def _store_slab(ref, k, val):
    r0 = _slab_row(k)
    ref[0, pl.ds(r0, FFT_N2), :] = val[:, 0:128]
    ref[1, pl.ds(r0, FFT_N2), :] = val[:, 128:256]


FFT_GROUP = 8


def _cross_slab(src_refs, n_src, mat_h, mat_l, dst_refs, n_dst):
    def body(gidx, carry):
        cols = []
        for i in range(FFT_GROUP):
            i2 = gidx * FFT_GROUP + i
            parts = []
            for half in range(2):
                parts.append(jnp.concatenate(
                    [r[half, pl.ds(i2, n_src, stride=FFT_PITCH), :] for r in src_refs], axis=0))
            cols.append(jnp.concatenate(parts, axis=1))
        x = jnp.concatenate(cols, axis=1)
        y = _dot_const(mat_h, mat_l, x)
        for i in range(FFT_GROUP):
            i2 = gidx * FFT_GROUP + i
            for half in range(2):
                c0 = (2 * i + half) * 128
                for d, ref in enumerate(dst_refs):
                    ref[half, pl.ds(i2, n_dst, stride=FFT_PITCH), :] = y[d * n_dst:(d + 1) * n_dst, c0:c0 + 128]
        return carry

    lax.fori_loop(0, FFT_N2 // FFT_GROUP, body, 0)


def _twiddle(ar, ai, tw_ref, k, conj):
    r0 = pl.multiple_of(k * FFT_N2, FFT_N2)
    tr = tw_ref[0, pl.ds(r0, FFT_N2), :]
    ti = tw_ref[1, pl.ds(r0, FFT_N2), :]
    tr = jnp.concatenate([tr, tr], axis=1)
    ti = jnp.concatenate([ti, ti], axis=1)
    if conj:
        return ar * tr + ai * ti, ai * tr - ar * ti
    return ar * tr - ai * ti, ar * ti + ai * tr


def _filter_kernel(feat_ref, dec_ref, w1_ref, b1_ref, w2_ref, b2_ref, w3_ref, b3_ref, fq_ref,
                   f1h_ref, f1l_ref, mh_ref, ml_ref, tw_ref, out_ref, zbuf, are, aim, *, n1, k1p):
    half = n1 // 2

    u_ = FFT_SLAB_UNROLL
    rows = u_ * FFT_N2

    def fill(g, carry):
        r0 = pl.multiple_of(g * rows, rows)
        z = feat_ref[pl.ds(r0, rows), :]
        h = jnp.sin(fq_ref[0, 0:1, :] * (_dot_f32(z, w1_ref[0]) + b1_ref[0]))
        h = jnp.sin(fq_ref[0, 1:2, :] * (_dot_f32(h, w2_ref[0]) + b2_ref[0]))
        h = _dot_f32(h, w3_ref[0]) + b3_ref[0]
        hsel = jnp.where(g * u_ < half, h[:, 0:W_GROUP], h[:, W_GROUP:2 * W_GROUP])
        kern = hsel * dec_ref[pl.ds(r0, rows), :]
        for u in range(u_):
            _store_slab(zbuf, g * u_ + u, kern[u * FFT_N2:(u + 1) * FFT_N2])
        return carry

    lax.fori_loop(0, n1 // u_, fill, 0)
    _cross_slab([zbuf], n1, f1h_ref[...], f1l_ref[...], [are, aim], k1p)

    def slab(g, carry):
        cols = []
        for u in range(u_):
            k = g * u_ + u
            ar, ai = _twiddle(_load_slab(are, k), _load_slab(aim, k), tw_ref, k, conj=False)
            cols.append(jnp.concatenate([ar, ai], axis=0))
        x = _dot_const(mh_ref[...], ml_ref[...], jnp.concatenate(cols, axis=1))
        for u in range(u_):
            r0 = pl.multiple_of((g * u_ + u) * FFT_N2, FFT_N2)
            out_ref[0, 0, pl.ds(r0, FFT_N2), :] = x[0:FFT_N2, u * W_GROUP:(u + 1) * W_GROUP]
            out_ref[0, 1, pl.ds(r0, FFT_N2), :] = x[FFT_N2:2 * FFT_N2, u * W_GROUP:(u + 1) * W_GROUP]
        return carry

    lax.fori_loop(0, k1p // u_, slab, 0)


def _filter_tables(l_):
    pos = np.concatenate([np.arange(l_), l_ - np.arange(l_)]).astype(np.float32)
    pos[l_] = 0.0
    t = pos / np.float32(max(l_ - 1, 1))
    w = np.float32(2.0 * math.pi) * pos / np.float32(l_)
    f = np.linspace(1e-4, HYENA_BANDS - 1, HYENA_BANDS, dtype=np.float32)
    feat = np.zeros((2 * l_, 64), np.float32)
    feat[:, 0] = t
    feat[:, 1:1 + HYENA_BANDS] = np.cos(w[:, None] * f)
    feat[:, 1 + HYENA_BANDS:HYENA_EMB] = -np.sin(w[:, None] * f)
    deltas = np.abs(np.linspace(HYENA_MIN_DECAY, HYENA_MAX_DECAY, W_GROUP, dtype=np.float32))
    dec = np.exp(-t[:, None] * deltas[None, :]).astype(np.float32)
    dec[l_] = 0.0
    return jnp.asarray(feat), jnp.asarray(dec)


def _filter_spectra(l_, w1p, b1, w2, b2, w3, b3, freq, fc):
    n1 = 2 * l_ // FFT_N2
    plan = _FftPlan(n1)
    k1p = plan.k1p
    feat, dec = _filter_tables(l_)
    full = lambda shape: pl.BlockSpec(shape, lambda l: (0,) * len(shape))
    per = lambda shape: pl.BlockSpec((1,) + shape, lambda l: (l,) + (0,) * len(shape))
    kern = functools.partial(_filter_kernel, n1=n1, k1p=k1p)
    return pl.pallas_call(
        kern,
        grid=(DEPTH,),
        in_specs=[full((2 * l_, 64)), full((2 * l_, W_GROUP)),
                  per((64, HYENA_FFN)), per((1, HYENA_FFN)), per((HYENA_FFN, HYENA_FFN)), per((1, HYENA_FFN)),
                  per((HYENA_FFN, 2 * W_GROUP)), per((1, 2 * W_GROUP)), per((2, HYENA_FFN)),
                  full((2 * k1p, n1)), full((2 * k1p, n1)), full((128, 128)), full((128, 128)),
                  full((2, k1p * FFT_N2, 128))],
        out_specs=pl.BlockSpec((1, 2, k1p * FFT_N2, W_GROUP), lambda l: (l, 0, 0, 0)),
        out_shape=jax.ShapeDtypeStruct((DEPTH, 2, k1p * FFT_N2, W_GROUP), F32),
        scratch_shapes=[pltpu.VMEM((2, n1 * FFT_PITCH, 128), F32),
                        pltpu.VMEM((2, k1p * FFT_PITCH, 128), F32),
                        pltpu.VMEM((2, k1p * FFT_PITCH, 128), F32)],
        compiler_params=pltpu.CompilerParams(vmem_limit_bytes=VMEM_LIMIT),
        name="hyena_filter_spectrum_%d" % l_,
    )(feat, dec, w1p, b1.reshape(DEPTH, 1, HYENA_FFN), w2, b2.reshape(DEPTH, 1, HYENA_FFN),
      w3, b3.reshape(DEPTH, 1, 2 * W_GROUP), freq,
      fc["f1"][0], fc["f1"][1], fc["m_fwd"][0], fc["m_fwd"][1], fc["tw"])


def _hyena_segment(v_ref, x1_ref, x0_ref, sg_ref, cw_ref, cb_ref, d_ref, kf_ref, consts, scratch, out_ref,
                   *, row0, seg_len, n1):
    f1h, f1l, gh, gl, mfh, mfl, mih, mil, tw_ref = consts
    zbuf, are, aim, ybuf = scratch
    n_sig = seg_len // FFT_N2
    n_in = max(n_sig, 8)
    n_out = max(n_sig, 8)
    k1p = _FftPlan(n1).k1p
    row_i = lax.broadcasted_iota(jnp.int32, (FFT_N2, W_GROUP), 0)

    def conv3(ref, blk, jslab):
        r = pl.multiple_of(row0 + jslab * FFT_N2, FFT_N2)
        cur = ref[0, pl.ds(r, FFT_N2), :]
        pe = ref[0, pl.ds(jnp.maximum(r - 1, row0), 1), :] * jnp.where(jslab > 0, 1.0, 0.0)
        ne = ref[0, pl.ds(jnp.minimum(r + FFT_N2, row0 + seg_len - 1), 1), :] * jnp.where(jslab < n_sig - 1, 1.0, 0.0)
        prev = jnp.where(row_i == 0, pe, pltpu.roll(cur, 1, 0))
        nxt = jnp.where(row_i == FFT_N2 - 1, ne, pltpu.roll(cur, FFT_N2 - 1, 0))
        c0 = blk * W_GROUP
        return (cw_ref[0:1, c0:c0 + W_GROUP] * prev + cw_ref[1:2, c0:c0 + W_GROUP] * cur
                + cw_ref[2:3, c0:c0 + W_GROUP] * nxt + cb_ref[:, c0:c0 + W_GROUP])

    def pre(jslab, carry):
        _store_slab(zbuf, jslab, conv3(v_ref, 0, jslab) * conv3(x1_ref, 1, jslab))
        return carry

    lax.fori_loop(0, n_sig, pre, 0)
    for jz in range(n_sig, n_in):
        _store_slab(zbuf, jz, jnp.zeros((FFT_N2, W_GROUP), F32))

    _cross_slab([zbuf], n_in, f1h, f1l, [are, aim], k1p)

    u_ = FFT_SLAB_UNROLL

    def slab(g, carry):
        cols = []
        for u in range(u_):
            k = g * u_ + u
            ar, ai = _twiddle(_load_slab(are, k), _load_slab(aim, k), tw_ref, k, conj=False)
            cols.append(jnp.concatenate([ar, ai], axis=0))
        x = _dot_const(mfh, mfl, jnp.concatenate(cols, axis=1))
        cols = []
        for u in range(u_):
            xr = x[0:FFT_N2, u * W_GROUP:(u + 1) * W_GROUP]
            xi = x[FFT_N2:2 * FFT_N2, u * W_GROUP:(u + 1) * W_GROUP]
            r0 = pl.multiple_of((g * u_ + u) * FFT_N2, FFT_N2)
            kr = kf_ref[0, 0, pl.ds(r0, FFT_N2), :]
            ki = kf_ref[0, 1, pl.ds(r0, FFT_N2), :]
            cols.append(jnp.concatenate([xr * kr - xi * ki, xr * ki + xi * kr], axis=0))
        bm = _dot_const(mih, mil, jnp.concatenate(cols, axis=1))
        for u in range(u_):
            k = g * u_ + u
            br, bi = _twiddle(bm[0:FFT_N2, u * W_GROUP:(u + 1) * W_GROUP],
                              bm[FFT_N2:2 * FFT_N2, u * W_GROUP:(u + 1) * W_GROUP], tw_ref, k, conj=True)
            _store_slab(are, k, br)
            _store_slab(aim, k, bi)
        return carry

    lax.fori_loop(0, k1p // u_, slab, 0)
    _cross_slab([are, aim], k1p, gh, gl, [ybuf], n_out)

    def post(jslab, carry):
        r = pl.multiple_of(row0 + jslab * FFT_N2, FFT_N2)
        y = _load_slab(ybuf, jslab) + _load_slab(zbuf, jslab) * d_ref[...]
        out_ref[0, pl.ds(r, FFT_N2), :] = conv3(x0_ref, 2, jslab) * y * _silu(sg_ref[0, pl.ds(r, FFT_N2), :])
        return carry

    lax.fori_loop(0, n_sig, post, 0)


def _hyena_kernel(v_ref, x1_ref, x0_ref, sg_ref, cw_ref, cb_ref, d_ref, kfc_ref, kfl_ref,
                  c_f1h, c_f1l, c_gh, c_gl, c_tw, l_f1h, l_f1l, l_gh, l_gl, l_tw,
                  mfh_ref, mfl_ref, mih_ref, mil_ref, out_ref,
                  zc, arc, aic, yc, zl, arl, ail, yl, *, ctx_len, lat_len):
    mats = (mfh_ref[...], mfl_ref[...], mih_ref[...], mil_ref[...])
    seg = functools.partial(_hyena_segment, v_ref, x1_ref, x0_ref, sg_ref, cw_ref, cb_ref, d_ref)
    seg(kfc_ref, (c_f1h[...], c_f1l[...], c_gh[...], c_gl[...]) + mats + (c_tw,), (zc, arc, aic, yc), out_ref,
        row0=0, seg_len=ctx_len, n1=2 * ctx_len // FFT_N2)
    seg(kfl_ref, (l_f1h[...], l_f1l[...], l_gh[...], l_gl[...]) + mats + (l_tw,), (zl, arl, ail, yl), out_ref,
        row0=ctx_len, seg_len=lat_len, n1=2 * lat_len // FFT_N2)


def _hyena(p, conv_w, conv_b, d, kf_ctx, kf_lat, layer, fc_ctx, fc_lat, ctx_len):
    b_, lt, _ = p.shape
    lat_len = lt - ctx_len
    n1c, n1l = 2 * ctx_len // FFT_N2, 2 * lat_len // FFT_N2
    kc, kl = _FftPlan(n1c).k1p, _FftPlan(n1l).k1p
    blk = lambda j: pl.BlockSpec((1, lt, W_GROUP), lambda b: (b, 0, j))
    full = lambda a: pl.BlockSpec(a.shape, lambda b: (0,) * a.ndim)
    per_layer = lambda a: pl.BlockSpec((1,) + a.shape[1:], lambda b: (layer,) + (0,) * (a.ndim - 1))
    cst = [fc_ctx["f1_in"][0], fc_ctx["f1_in"][1], fc_ctx["g"][0], fc_ctx["g"][1], fc_ctx["tw"],
           fc_lat["f1_in"][0], fc_lat["f1_in"][1], fc_lat["g"][0], fc_lat["g"][1], fc_lat["tw"],
           fc_lat["m_fwd"][0], fc_lat["m_fwd"][1], fc_lat["m_inv"][0], fc_lat["m_inv"][1]]
    small = [conv_w, conv_b.reshape(1, 3 * W_GROUP), d.reshape(1, W_GROUP)]
    sbuf = lambda n: pltpu.VMEM((2, n * FFT_PITCH, 128), F32)
    kern = functools.partial(_hyena_kernel, ctx_len=ctx_len, lat_len=lat_len)
    return pl.pallas_call(
        kern,
        grid=(b_,),
        in_specs=([blk(HV), blk(HX1), blk(HX0), blk(HG)] + [full(a) for a in small]
                  + [per_layer(kf_ctx), per_layer(kf_lat)] + [full(a) for a in cst]),
        out_specs=pl.BlockSpec((1, lt, W_GROUP), lambda b: (b, 0, 0)),
        out_shape=jax.ShapeDtypeStruct((b_, lt, W_GROUP), F32),
        scratch_shapes=[sbuf(max(n1c // 2, 8)), sbuf(kc), sbuf(kc), sbuf(max(n1c // 2, 8)),
                        sbuf(n1l // 2), sbuf(kl), sbuf(kl), sbuf(n1l // 2)],
        compiler_params=pltpu.CompilerParams(vmem_limit_bytes=VMEM_LIMIT),
        name="hyena_mixer",
    )(p, p, p, p, *small, kf_ctx, kf_lat, *cst)


def _rope_tables(ctx_len, lat_len):
    pos = np.arange(lat_len)
    row = (pos // GRID_W).astype(np.float32)
    col = (pos % GRID_W).astype(np.float32)
    inv = (np.float32(ROPE_BASE) ** (-np.arange(ROPE_AXIS_FREQS, dtype=np.float32) / np.float32(ROPE_AXIS_FREQS)))
    ang = np.concatenate([row[:, None] * inv, col[:, None] * inv], axis=-1).astype(np.float32)
    cos, sin = np.cos(ang), np.sin(ang)
    cos_h = np.concatenate([cos, cos], axis=-1)
    sin_h = np.concatenate([-sin, sin], axis=-1)
    cos_l = np.tile(cos_h, (1, N_HEADS))
    sin_l = np.tile(sin_h, (1, N_HEADS))
    cos_t = np.concatenate([np.ones((ctx_len, W_GROUP), np.float32), cos_l], axis=0)
    sin_t = np.concatenate([np.zeros((ctx_len, W_GROUP), np.float32), sin_l], axis=0)
    return jnp.asarray(cos_t, F32), jnp.asarray(sin_t, F32)


def _rope_partner():
    d = np.arange(W_GROUP)
    partner = np.where(d % HEAD_DIM < HEAD_DIM // 2, d + HEAD_DIM // 2, d - HEAD_DIM // 2)
    p = np.zeros((W_GROUP, W_GROUP), np.float32)
    p[partner, d] = 1.0
    return jnp.asarray(p, BF16)


def _mlstm_expansion():
    e = np.zeros((128, 3 * W_GROUP), np.float32)
    for j in range(3):
        for h in range(N_HEADS):
            e[8 * (j + 1) + h, j * W_GROUP + h * HEAD_DIM:j * W_GROUP + (h + 1) * HEAD_DIM] = 1.0
    return jnp.asarray(e, BF16)


def _tri(n):
    t = np.tril(np.ones((n, n), np.float32))
    return jnp.asarray(t, BF16), jnp.asarray(t.T, BF16)


def _layout_w_in(w_in):
    a0 = 0
    h0 = 5 * W_GROUP + 16
    g0 = h0 + 4 * W_GROUP
    d0 = g0 + 4 * W_GROUP + 2 * GLA_RANK
    kv = W_GROUP // 2
    k_cols = w_in[..., d0 + W_GROUP:d0 + W_GROUP + kv]
    v_cols = w_in[..., d0 + W_GROUP + kv:d0 + W_GROUP + 2 * kv]

    def expand_kv(c):
        return jnp.concatenate([c[..., 0:HEAD_DIM], c[..., 0:HEAD_DIM],
                                c[..., HEAD_DIM:2 * HEAD_DIM], c[..., HEAD_DIM:2 * HEAD_DIM]], axis=-1)

    small = jnp.concatenate([w_in[..., 5 * W_GROUP:5 * W_GROUP + 16],
                             w_in[..., g0 + 4 * W_GROUP:g0 + 4 * W_GROUP + 2 * GLA_RANK]], axis=-1)
    pad = jnp.zeros(w_in.shape[:-1] + (128 - 48,), w_in.dtype)
    wide = jnp.concatenate([
        w_in[..., a0:a0 + 5 * W_GROUP],
        w_in[..., h0:h0 + 4 * W_GROUP],
        w_in[..., g0:g0 + 4 * W_GROUP],
        w_in[..., d0:d0 + W_GROUP], expand_kv(k_cols), expand_kv(v_cols),
        w_in[..., d0 + W_GROUP + 2 * kv:d0 + 2 * W_GROUP + 2 * kv],
        small, pad], axis=-1)
    w_g = jnp.swapaxes(w_in[..., 5 * W_GROUP:5 * W_GROUP + 16], -1, -2)
    w_g = w_g.reshape(w_g.shape[:-2] + (4, N_HEADS, w_g.shape[-1]))
    w_st = jnp.pad(w_g, [(0, 0)] * (w_g.ndim - 2) + [(0, 8 - N_HEADS), (0, 0)])
    w_st = w_st.reshape(w_st.shape[:-3] + (GATE_ROWS, w_st.shape[-1]))
    return wide.astype(BF16), w_st.astype(BF16)


def kernel(x, c, ctx, c_ctx, w_ada, b_ada, g_pre, g_post, w_in, mlstm_gate_b, mlstm_norm_g, hyena_conv_w,
           hyena_conv_b, hyena_w1, hyena_b1, hyena_w2, hyena_b2, hyena_w3, hyena_b3, hyena_freq, hyena_d,
           gla_w_alpha, gla_b_alpha, gla_norm_g, attn_sink, w_out):
    b_, lat_len, _ = x.shape
    ctx_len = ctx.shape[1]
    n_ctx_tiles = ctx_len // ROW_TILE

    tril_m, triu_m = _tri(MLSTM_CHUNK)
    tril_g, triu_g = _tri(GLA_CHUNK)
    bd = np.kron(np.eye(N_HEADS, dtype=np.float32), np.ones((HEAD_DIM, HEAD_DIM), np.float32))
    rope_cos, rope_sin = _rope_tables(ctx_len, lat_len)
    mones_m = np.kron(np.eye(N_HEADS, dtype=np.float32), np.ones((MLSTM_CHUNK, HEAD_DIM), np.float32))
    consts = dict(tril_m=tril_m, triu_m=triu_m, tril_g=tril_g, triu_g=triu_g, bd=jnp.asarray(bd, BF16),
                  mones_m=jnp.asarray(mones_m, BF16), eall_m=_mlstm_expansion(),
                  rope_cos=rope_cos, rope_sin=rope_sin, pswap=_rope_partner())
    fc_ctx = _fft_constants(2 * ctx_len // FFT_N2, max(ctx_len // FFT_N2, 8), max(ctx_len // FFT_N2, 8))
    fc_lat = _fft_constants(2 * lat_len // FFT_N2, lat_len // FFT_N2, lat_len // FFT_N2)

    cc = jnp.zeros((16, D_MODEL), F32).at[0:b_].set(c).at[b_].set(c_ctx)
    mod = _modulation(cc, w_ada, b_ada)
    mod_ctx = jnp.broadcast_to(mod[:, b_:b_ + 1], (DEPTH, b_, 3 * D_MODEL))
    mod_all = jnp.stack([mod_ctx, mod[:, 0:b_]], axis=2).reshape(DEPTH, 2 * b_, 1, 3 * D_MODEL)

    w_p, w_st = _layout_w_in(w_in)
    w_o = w_out.astype(BF16)
    w1p = jnp.zeros((DEPTH, 64, HYENA_FFN), F32).at[:, 0:HYENA_EMB].set(hyena_w1)
    filt = (w1p, hyena_b1, hyena_w2, hyena_b2, hyena_w3, hyena_b3, hyena_freq)
    kf_ctx = _filter_spectra(ctx_len, *filt, fc_ctx)
    kf_lat = _filter_spectra(lat_len, *filt, fc_lat)

    xs = jnp.concatenate([ctx, x], axis=1)
    for l in range(DEPTH):
        last = l == DEPTH - 1
        p, gt = _inproj(xs, mod_all, l, g_pre[l], w_p, w_st, n_ctx_tiles)
        a = _mlstm(p, gt, mlstm_gate_b[l], mlstm_norm_g[l], consts, ctx_len)
        hy = _hyena(p, hyena_conv_w[l], hyena_conv_b[l], hyena_d[l], kf_ctx, kf_lat, l, fc_ctx, fc_lat, ctx_len)
        g = _gla(p, gla_w_alpha[l], gla_b_alpha[l], gla_norm_g[l], consts, ctx_len)
        d = _attn(p, attn_sink[l], consts, ctx_len)
        xs = _outproj((a, hy, g, d), w_o, l, g_post[l], mod_all, xs, n_ctx_tiles, skip_ctx=last)
    return xs
```

```python
import functools
import math

import numpy as np
import jax
import jax.numpy as jnp
from jax import lax
from jax.experimental import pallas as pl
from jax.experimental.pallas import tpu as pltpu

F32 = jnp.float32
BF16 = jnp.bfloat16

D_MODEL = 1024
DEPTH = 4
GRID_W = 64
W_GROUP = 256
HEAD_DIM = 64
N_HEADS = 4
ATT_KV_HEADS = 2
WINDOW = 128
ATT_BLOCK = 128
GLA_RANK = 16
GLA_NORMALIZER = 16.0
HYENA_BANDS = 16
HYENA_EMB = 1 + 2 * HYENA_BANDS
HYENA_FFN = 64
HYENA_MIN_DECAY = math.log(1e-2) / 1.5
HYENA_MAX_DECAY = math.log(1e-2) / 0.3
ROPE_BASE = 10000.0
ROPE_AXIS_FREQS = HEAD_DIM // 4
EPS = 1e-6

N_WIDE_BLOCKS = 16
SMALL_COL0 = N_WIDE_BLOCKS * W_GROUP
N_PROJ = SMALL_COL0 + 128
SMALL_BLOCK = SMALL_COL0 // 128
(MQ, MK, MV, MO, MG, HV, HX1, HX0, HG, GQ, GK, GV, GG, AQ, AKV, AG) = range(N_WIDE_BLOCKS)

GATE_ROWS = 32
ROW_TILE = 256
MLSTM_CHUNK = 256
GLA_CHUNK = 128
ATT_STEP_BLOCKS = 3
FFT_N2 = 64
FFT_PITCH = 72
FFT_SLAB_UNROLL = 4
VMEM_LIMIT = 56 * 1024 * 1024


def _dot(a, b):
    return jnp.dot(a, b, preferred_element_type=F32)


def _dot_nt(a, b):
    return lax.dot_general(a, b, (((1,), (1,)), ((), ())), preferred_element_type=F32)


def _dot_tn(a, b):
    return lax.dot_general(a, b, (((0,), (0,)), ((), ())), preferred_element_type=F32)


def _split2(x):
    hi = x.astype(BF16)
    lo = (x - hi.astype(F32)).astype(BF16)
    return hi, lo


def _split3(x):
    h1 = x.astype(BF16)
    r1 = x - h1.astype(F32)
    h2 = r1.astype(BF16)
    h3 = (r1 - h2.astype(F32)).astype(BF16)
    return h1, h2, h3


def _dot_const(ch, cl, x):
    xh, xl = _split2(x)
    return _dot(ch, xh) + _dot(ch, xl) + _dot(cl, xh)


def _dot_const2(ch, cl, x):
    xb = x.astype(BF16)
    return _dot(ch, xb) + _dot(cl, xb)


def _dot_f32(a, b):
    ah, al = _split2(a)
    bh, bl = _split2(b)
    return _dot(ah, bh) + _dot(ah, bl) + _dot(al, bh)


def _dot_exact_lhs(c, x):
    x1, x2, x3 = _split3(x)
    return _dot(c, x1) + _dot(c, x2) + _dot(c, x3)


def _dot_exact_rhs(x, c):
    x1, x2, x3 = _split3(x)
    return _dot(x1, c) + _dot(x2, c) + _dot(x3, c)


def _log_sigmoid(x):
    return jnp.minimum(x, 0.0) - jnp.log(1.0 + jnp.exp(-jnp.abs(x)))


def _silu(x):
    return x * jax.nn.sigmoid(x)


def _head_masks():
    lane = lax.broadcasted_iota(jnp.int32, (1, W_GROUP), 1) // HEAD_DIM
    return [(lane == h).astype(F32) for h in range(N_HEADS)]


def _block_diag_mask():
    r = lax.broadcasted_iota(jnp.int32, (W_GROUP, W_GROUP), 0) // HEAD_DIM
    c = lax.broadcasted_iota(jnp.int32, (W_GROUP, W_GROUP), 1) // HEAD_DIM
    return (r == c).astype(F32)


def _spread(cols):
    lane = lax.broadcasted_iota(jnp.int32, (1, W_GROUP), 1)
    out = cols[N_HEADS - 1]
    for h in range(N_HEADS - 2, -1, -1):
        out = jnp.where(lane < (h + 1) * HEAD_DIM, cols[h], out)
    return out


def _head_rms_norm(h, bd_bf16, g_row):
    sq = h * h
    sh, sl = _split2(sq)
    ms = (_dot(sh, bd_bf16) + _dot(sl, bd_bf16)) * (1.0 / HEAD_DIM)
    return h * lax.rsqrt(ms + EPS) * g_row


def _mod_kernel(c_ref, w_ref, b_ref, o_ref):
    s = _silu(c_ref[...])
    o_ref[0] = _dot_f32(s, w_ref[0]) + b_ref[0]


def _modulation(cc, w_ada, b_ada):
    nblk = 3 * D_MODEL // 1024
    return pl.pallas_call(
        _mod_kernel,
        grid=(DEPTH, nblk),
        in_specs=[
            pl.BlockSpec((16, D_MODEL), lambda l, j: (0, 0)),
            pl.BlockSpec((1, D_MODEL, 1024), lambda l, j: (l, 0, j)),
            pl.BlockSpec((1, 1, 1024), lambda l, j: (l, 0, j)),
        ],
        out_specs=pl.BlockSpec((1, 16, 1024), lambda l, j: (l, 0, j)),
        out_shape=jax.ShapeDtypeStruct((DEPTH, 16, 3 * D_MODEL), F32),
        compiler_params=pltpu.CompilerParams(vmem_limit_bytes=VMEM_LIMIT),
        name="adaln_modulation",
    )(cc, w_ada, b_ada.reshape(DEPTH, 1, 3 * D_MODEL))


def _token_specs(xs, n_ctx_tiles, t0):
    shape = (1, ROW_TILE, D_MODEL)
    if not isinstance(xs, tuple):
        return [pl.BlockSpec(shape, lambda b, i: (b, i + t0, 0))], [xs]
    return [pl.BlockSpec(shape, lambda b, i: (b, jnp.minimum(i + t0, n_ctx_tiles - 1), 0)),
            pl.BlockSpec(shape, lambda b, i: (b, jnp.maximum(i + t0 - n_ctx_tiles, 0), 0))], list(xs)


def _token_tile(tok_refs, n_ctx_tiles, t0):
    if len(tok_refs) == 1:
        return tok_refs[0][0]
    return jnp.where(pl.program_id(1) + t0 < n_ctx_tiles, tok_refs[0][0], tok_refs[1][0])


def _inproj_kernel(*refs, n_tok, n_ctx_tiles):
    mod_ref, g_ref, w_ref, wst_ref, p_ref, gt_ref = refs[n_tok:]
    x = _token_tile(refs[:n_tok], n_ctx_tiles, 0)
    ms = jnp.mean(x * x, axis=-1, keepdims=True)
    y = x * lax.rsqrt(ms + EPS) * g_ref[...]
    sh = mod_ref[0, 0, :, 0:D_MODEL]
    sc = mod_ref[0, 0, :, D_MODEL:2 * D_MODEL]
    hb = (y * (1.0 + sc) + sh).astype(BF16)
    p_ref[0] = _dot(hb, w_ref[0])
    gt_ref[0] = _dot_nt(wst_ref[0], hb)


def _stream_shape(xs):
    if isinstance(xs, tuple):
        return xs[1].shape[0], xs[0].shape[1] + xs[1].shape[1]
    return xs.shape[0], xs.shape[1]


def _inproj(xs, mod_all, layer, g_pre, w_p, w_st, n_ctx_tiles):
    b_, lt = _stream_shape(xs)
    nt = lt // ROW_TILE
    tok_specs, tok_args = _token_specs(xs, n_ctx_tiles, 0)
    return pl.pallas_call(
        functools.partial(_inproj_kernel, n_tok=len(tok_args), n_ctx_tiles=n_ctx_tiles),
        grid=(b_, nt),
        in_specs=tok_specs + [
            pl.BlockSpec((1, 1, 1, 3 * D_MODEL),
                         lambda b, i: (layer, 2 * b + jnp.where(i >= n_ctx_tiles, 1, 0), 0, 0)),
            pl.BlockSpec((1, D_MODEL), lambda b, i: (0, 0)),
            pl.BlockSpec((1, D_MODEL, N_PROJ), lambda b, i: (layer, 0, 0)),
            pl.BlockSpec((1, GATE_ROWS, D_MODEL), lambda b, i: (layer, 0, 0)),
        ],
        out_specs=[
            pl.BlockSpec((1, ROW_TILE, N_PROJ), lambda b, i: (b, i, 0)),
            pl.BlockSpec((1, GATE_ROWS, ROW_TILE), lambda b, i: (b, 0, i)),
        ],
        out_shape=[
            jax.ShapeDtypeStruct((b_, lt, N_PROJ), F32),
            jax.ShapeDtypeStruct((b_, GATE_ROWS, lt), F32),
        ],
        compiler_params=pltpu.CompilerParams(vmem_limit_bytes=VMEM_LIMIT),
        name="inproj",
    )(*tok_args, mod_all, g_pre.reshape(1, D_MODEL), w_p, w_st)


def _outproj_kernel(a_ref, h_ref, g_ref, d_ref, w_ref, gp_ref, mod_ref, *refs, n_tok, n_ctx_tiles, t0):
    o_ref = refs[n_tok]
    mix = jnp.concatenate([a_ref[0], h_ref[0], g_ref[0], d_ref[0]], axis=1)
    y = _dot(mix, w_ref[0])
    ms = jnp.mean(y * y, axis=-1, keepdims=True)
    yn = y * lax.rsqrt(ms + EPS) * gp_ref[...]
    gt = mod_ref[0, 0, :, 2 * D_MODEL:3 * D_MODEL]
    o_ref[0] = _token_tile(refs[:n_tok], n_ctx_tiles, t0) + gt * yn


def _outproj(mix, w_o, layer, g_post, mod_all, xs, n_ctx_tiles, skip_ctx):
    b_, lt = _stream_shape(xs)
    nt = lt // ROW_TILE
    t0 = n_ctx_tiles if skip_ctx else 0
    rows = lambda b, i: (b, i + t0, 0)
    tok_specs, tok_args = _token_specs(xs, n_ctx_tiles, t0)
    return pl.pallas_call(
        functools.partial(_outproj_kernel, n_tok=len(tok_args), n_ctx_tiles=n_ctx_tiles, t0=t0),
        grid=(b_, nt - t0),
        in_specs=[pl.BlockSpec((1, ROW_TILE, W_GROUP), rows)] * 4 + [
            pl.BlockSpec((1, D_MODEL, D_MODEL), lambda b, i: (layer, 0, 0)),
            pl.BlockSpec((1, D_MODEL), lambda b, i: (0, 0)),
            pl.BlockSpec((1, 1, 1, 3 * D_MODEL),
                         lambda b, i: (layer, 2 * b + jnp.where(i + t0 >= n_ctx_tiles, 1, 0), 0, 0)),
        ] + tok_specs,
        out_specs=pl.BlockSpec((1, ROW_TILE, D_MODEL), lambda b, i: (b, i, 0)),
        out_shape=jax.ShapeDtypeStruct((b_, lt - t0 * ROW_TILE, D_MODEL), F32),
        compiler_params=pltpu.CompilerParams(vmem_limit_bytes=VMEM_LIMIT),
        name="outproj",
    )(*mix, w_o, g_post.reshape(1, D_MODEL), mod_all, *tok_args)


def _interleave(*gens):
    live = list(gens)
    while live:
        live = [g for g in live if next(g, StopIteration) is not StopIteration]


def _scan_chunk(s, reverse, n_ctx, n_tot):
    if not reverse:
        return s
    return jnp.where(s < n_ctx, n_ctx - 1 - s, n_tot - 1 - (s - n_ctx))


def _mlstm_kernel(q_ref, k_ref, v_ref, o_ref, sg_ref, gt_ref, brow_ref, ng_ref,
                  bd_ref, mones_ref, eall_ref, out_ref, hf_ref, hb_ref, c_ref, n_ref, pt_ref, r8_ref, ab_ref,
                  *, chunk, n_ctx, n_tot):
    t_ = chunk
    masks = _head_masks()
    masks_bf = [m.astype(BF16) for m in masks]
    bdm = _block_diag_mask()
    row_i = lax.broadcasted_iota(jnp.int32, (t_, t_), 0)
    col_i = lax.broadcasted_iota(jnp.int32, (t_, t_), 1)
    lane_a = lax.broadcasted_iota(jnp.int32, (8 * n_tot, t_), 1)
    live = (lax.broadcasted_iota(jnp.int32, (8 * n_tot, t_), 0) % 8) < N_HEADS
    c_ref[...] = jnp.zeros_like(c_ref)
    n_ref[...] = jnp.zeros_like(n_ref)

    def scan(x, reverse, op):
        k = 1
        while k < t_:
            if reverse:
                x = jnp.where(lane_a < t_ - k, op(x, pltpu.roll(x, t_ - k, 1)), x)
            else:
                x = jnp.where(lane_a >= k, op(x, pltpu.roll(x, k, 1)), x)
            k *= 2
        return x

    def gates(reverse):
        d_ = 1 if reverse else 0
        end = 0 if reverse else t_ - 1
        order = list(range(n_tot))
        if reverse:
            order = order[:n_ctx][::-1] + order[n_ctx:][::-1]

        def rows_of(g):
            r_ = slice(8 * g, 8 * g + 8)
            return jnp.concatenate([gt_ref[0, r_, c * t_:(c + 1) * t_] for c in range(n_tot)], axis=0) + jnp.concatenate(
                [brow_ref[r_, :]] * n_tot, axis=0)

        i_all = jnp.where(live, rows_of(2 * d_), 0.0)
        lf_all = jnp.where(live, _log_sigmoid(rows_of(2 * d_ + 1)), 0.0)
        b_all = scan(lf_all, reverse, jnp.add)
        r_all = i_all - b_all
        cm_all = scan(r_all, reverse, jnp.maximum)
        yield
        m_prev = jnp.zeros((8, 1), F32)
        for c in order:
            cols = slice(c * t_, (c + 1) * t_)
            blk8 = slice(8 * c, 8 * c + 8)
            i8, b8, r8 = i_all[blk8], b_all[blk8], r_all[blk8]
            inter = b8 + m_prev
            m_t = jnp.maximum(inter, b8 + cm_all[blk8])
            b_end = b8[:, end:end + 1]
            g8 = b_end - b8 + i8
            m_new = jnp.maximum(b_end + m_prev, jnp.max(g8, axis=1, keepdims=True))
            a_prev = jnp.exp(b_end + m_prev - m_new)
            packed = jnp.concatenate([b8 - m_t, jnp.exp(inter - m_t), jnp.exp(g8 - m_new), jnp.exp(-m_t),
                                      jnp.zeros((128 - 32, t_), F32)], axis=0)
            pt_ref[d_, cols, :] = packed.T
            r8_ref[d_, :, cols] = r8
            a_b = a_prev[0:1] * masks[0]
            for h in range(1, N_HEADS):
                a_b = a_b + a_prev[h:h + 1] * masks[h]
            ab_ref[d_, c] = a_b
            m_prev = m_new
            yield

    _interleave(gates(False), gates(True))

    def step(reverse, c_idx, h_ref):
        d_ = 1 if reverse else 0
        valid = (col_i >= row_i) if reverse else (col_i <= row_i)
        r0 = pl.multiple_of(c_idx * t_, t_)
        q = q_ref[0, pl.ds(r0, t_), :]
        k = k_ref[0, pl.ds(r0, t_), :] * (HEAD_DIM ** -0.5)
        qb = q.astype(BF16)
        kb = k.astype(BF16)
        vb = v_ref[0, pl.ds(r0, t_), :].astype(BF16)
        pt = pt_ref[d_, pl.ds(r0, t_), :]
        r8 = r8_ref[d_, :, pl.ds(r0, t_)]
        c_prev = c_ref[d_]
        n_prev = n_ref[d_]
        spread = _dot(pt.astype(BF16), eall_ref[...])
        s_cat = []
        for h in range(N_HEADS):
            w_intra = jnp.where(valid, jnp.exp(pt[:, h:h + 1] + r8[h:h + 1, :]), 0.0)
            s_cat.append((_dot_nt(qb * masks_bf[h], kb) * w_intra).astype(BF16))
        yield
        v_stack = jnp.concatenate([jnp.concatenate([vb * masks_bf[h] for h in range(N_HEADS)], axis=0),
                                   mones_ref[...]], axis=1)
        acc = _dot(jnp.concatenate(s_cat, axis=1), v_stack)
        qc = _dot(qb, c_prev.astype(BF16))
        qn = _dot((q * n_prev).astype(BF16), bd_ref[...])
        yield
        wi_b = spread[:, 0:W_GROUP]
        num = wi_b * qc + acc[:, 0:W_GROUP]
        den = wi_b * qn + acc[:, W_GROUP:2 * W_GROUP]
        h_ref[pl.ds(r0, t_), :] = num / jnp.maximum(jnp.abs(den), spread[:, 2 * W_GROUP:3 * W_GROUP])
        a_b = ab_ref[d_, c_idx]
        kw = k * spread[:, W_GROUP:2 * W_GROUP]
        c_ref[d_] = a_b * c_prev + bdm * _dot_tn(kw.astype(BF16), vb)
        n_ref[d_] = a_b * n_prev + jnp.sum(kw, axis=0, keepdims=True)

    def body(s, carry):
        _interleave(step(False, _scan_chunk(s, False, n_ctx, n_tot), hf_ref),
                    step(True, _scan_chunk(s, True, n_ctx, n_tot), hb_ref))
        return carry

    lax.fori_loop(0, n_tot, body, 0)

    def finish(c_idx, carry):
        r0 = pl.multiple_of(c_idx * t_, t_)
        hsum = hf_ref[pl.ds(r0, t_), :] + hb_ref[pl.ds(r0, t_), :]
        hsum = hsum * jax.nn.sigmoid(o_ref[0, pl.ds(r0, t_), :])
        hn = _head_rms_norm(hsum, bd_ref[...], ng_ref[...])
        out_ref[0, pl.ds(r0, t_), :] = (hn * _silu(sg_ref[0, pl.ds(r0, t_), :])).astype(out_ref.dtype)
        return carry

    lax.fori_loop(0, n_tot, finish, 0)


def _mlstm(p, gt, gate_b, norm_g, consts, ctx_len):
    b_, lt, _ = p.shape
    t_ = MLSTM_CHUNK
    brow = jnp.pad(gate_b.reshape(4, N_HEADS), ((0, 0), (0, 8 - N_HEADS))).reshape(GATE_ROWS, 1)
    blk = lambda j: pl.BlockSpec((1, lt, W_GROUP), lambda b: (b, 0, j))
    full = lambda shape: pl.BlockSpec(shape, lambda b: (0,) * len(shape))
    kern = functools.partial(_mlstm_kernel, chunk=t_, n_ctx=ctx_len // t_, n_tot=lt // t_)
    return pl.pallas_call(
        kern,
        grid=(b_,),
        in_specs=[blk(MQ), blk(MK), blk(MV), blk(MO), blk(MG),
                  pl.BlockSpec((1, GATE_ROWS, lt), lambda b: (b, 0, 0)),
                  full((GATE_ROWS, 1)), full((1, W_GROUP)),
                  full((W_GROUP, W_GROUP)), full((N_HEADS * t_, W_GROUP)), full((128, 3 * W_GROUP))],
        out_specs=pl.BlockSpec((1, lt, W_GROUP), lambda b: (b, 0, 0)),
        out_shape=jax.ShapeDtypeStruct((b_, lt, W_GROUP), BF16),
        scratch_shapes=[pltpu.VMEM((lt, W_GROUP), F32), pltpu.VMEM((lt, W_GROUP), F32),
                        pltpu.VMEM((2, W_GROUP, W_GROUP), F32), pltpu.VMEM((2, 1, W_GROUP), F32),
                        pltpu.VMEM((2, lt, 128), F32), pltpu.VMEM((2, 8, lt), F32),
                        pltpu.VMEM((2, lt // t_, 1, W_GROUP), F32)],
        compiler_params=pltpu.CompilerParams(vmem_limit_bytes=VMEM_LIMIT),
        name="mlstm_mixer",
    )(p, p, p, p, p, gt, brow, norm_g.reshape(1, W_GROUP),
      consts["bd"], consts["mones_m"], consts["eall_m"])


def _gla_kernel(q_ref, k_ref, v_ref, sg_ref, sm_ref, wa_ref, ba_ref, ng_ref, tril_ref, triu_ref, bd_ref,
                out_ref, hf_ref, hb_ref, st_ref, qd_ref, kd_ref, kc_ref, eb_ref, *, chunk, n_ctx, n_tot):
    t_ = chunk
    masks_bf = [m.astype(BF16) for m in _head_masks()]
    bdm = _block_diag_mask()
    row_i = lax.broadcasted_iota(jnp.int32, (t_, t_), 0)
    col_i = lax.broadcasted_iota(jnp.int32, (t_, t_), 1)
    st_ref[...] = jnp.zeros_like(st_ref)

    def decay(reverse, c_idx):
        d_ = 1 if reverse else 0
        tri_c = triu_ref[...] if reverse else tril_ref[...]
        end_row = 0 if reverse else t_ - 1
        r0 = pl.multiple_of(c_idx * t_, t_)
        sm = sm_ref[0, pl.ds(r0, t_), :]
        la = _log_sigmoid(_dot_f32(sm, wa_ref[d_]) + ba_ref[d_]) * (1.0 / GLA_NORMALIZER)
        yield
        bcum = _dot_exact_lhs(tri_c, la)
        yield
        q = q_ref[0, pl.ds(r0, t_), :] * (HEAD_DIM ** -0.5)
        k = k_ref[0, pl.ds(r0, t_), :]
        b_end = bcum[end_row:end_row + 1, :]
        qd_ref[d_, pl.ds(r0, t_), :] = (q * jnp.exp(bcum)).astype(BF16)
        kd_ref[d_, pl.ds(r0, t_), :] = (k * jnp.exp(-bcum)).astype(BF16)
        kc_ref[d_, pl.ds(r0, t_), :] = (k * jnp.exp(b_end - bcum)).astype(BF16)
        eb_ref[d_, c_idx] = jnp.exp(b_end)

    def decay_body(s, carry):
        _interleave(*[decay(rev, 2 * s + u) for u in range(2) for rev in (False, True)])
        return carry

    lax.fori_loop(0, n_tot // 2, decay_body, 0)

    def step(reverse, c_idx, h_ref):
        d_ = 1 if reverse else 0
        valid = (col_i >= row_i) if reverse else (col_i <= row_i)
        r0 = pl.multiple_of(c_idx * t_, t_)
        vb = v_ref[0, pl.ds(r0, t_), :].astype(BF16)
        qdb = qd_ref[d_, pl.ds(r0, t_), :]
        kd = kd_ref[d_, pl.ds(r0, t_), :]
        st_prev = st_ref[d_]
        att = [jnp.where(valid, _dot_nt(qdb * masks_bf[h], kd), 0.0).astype(BF16) for h in range(N_HEADS)]
        yield
        v_stack = jnp.concatenate([vb * masks_bf[h] for h in range(N_HEADS)], axis=0)
        h_ref[pl.ds(r0, t_), :] = (_dot_nt(qdb, st_prev.astype(BF16))
                                   + _dot(jnp.concatenate(att, axis=1), v_stack))
        st_ref[d_] = st_prev * eb_ref[d_, c_idx] + bdm * _dot_tn(vb, kc_ref[d_, pl.ds(r0, t_), :])

    def body(s, carry):
        _interleave(step(False, _scan_chunk(s, False, n_ctx, n_tot), hf_ref),
                    step(True, _scan_chunk(s, True, n_ctx, n_tot), hb_ref))
        return carry

    lax.fori_loop(0, n_tot, body, 0)

    def finish(i, carry):
        r0 = pl.multiple_of(i * ROW_TILE, ROW_TILE)
        hsum = hf_ref[pl.ds(r0, ROW_TILE), :] + hb_ref[pl.ds(r0, ROW_TILE), :]
        hn = _head_rms_norm(hsum, bd_ref[...], ng_ref[...])
        out_ref[0, pl.ds(r0, ROW_TILE), :] = (hn * _silu(sg_ref[0, pl.ds(r0, ROW_TILE), :])).astype(out_ref.dtype)
        return carry

    lax.fori_loop(0, n_tot * t_ // ROW_TILE, finish, 0)


def _gla(p, w_alpha, b_alpha, norm_g, consts, ctx_len):
    b_, lt, _ = p.shape
    t_ = GLA_CHUNK
    wa = jnp.zeros((2, 128, W_GROUP), F32)
    wa = wa.at[0, 16:32].set(w_alpha[0]).at[1, 32:48].set(w_alpha[1])
    blk = lambda j: pl.BlockSpec((1, lt, W_GROUP), lambda b: (b, 0, j))
    full = lambda shape: pl.BlockSpec(shape, lambda b: (0,) * len(shape))
    kern = functools.partial(_gla_kernel, chunk=t_, n_ctx=ctx_len // t_, n_tot=lt // t_)
    return pl.pallas_call(
        kern,
        grid=(b_,),
        in_specs=[blk(GQ), blk(GK), blk(GV), blk(GG),
                  pl.BlockSpec((1, lt, 128), lambda b: (b, 0, SMALL_BLOCK)),
                  full((2, 128, W_GROUP)), full((2, 1, W_GROUP)), full((1, W_GROUP)),
                  full((t_, t_)), full((t_, t_)), full((W_GROUP, W_GROUP))],
        out_specs=pl.BlockSpec((1, lt, W_GROUP), lambda b: (b, 0, 0)),
        out_shape=jax.ShapeDtypeStruct((b_, lt, W_GROUP), BF16),
        scratch_shapes=[pltpu.VMEM((lt, W_GROUP), F32), pltpu.VMEM((lt, W_GROUP), F32),
                        pltpu.VMEM((2, W_GROUP, W_GROUP), F32),
                        pltpu.VMEM((2, lt, W_GROUP), BF16), pltpu.VMEM((2, lt, W_GROUP), BF16),
                        pltpu.VMEM((2, lt, W_GROUP), BF16), pltpu.VMEM((2, lt // t_, 1, W_GROUP), F32)],
        compiler_params=pltpu.CompilerParams(vmem_limit_bytes=VMEM_LIMIT),
        name="gla_mixer",
    )(p, p, p, p, p, wa, b_alpha.reshape(2, 1, W_GROUP), norm_g.reshape(1, W_GROUP),
      consts["tril_g"], consts["triu_g"], consts["bd"])


def _attn_kernel(q_ref, kv_ref, sg_ref, cos_ref, sin_ref, sink_ref, pswap_ref, out_ref,
                 kb_ref, vb_ref, *, ctx_len, lat_len):
    g = pl.program_id(1)
    lt = ctx_len + lat_len
    n_ctx_blk = ctx_len // ATT_BLOCK
    nwin = 3 * ATT_BLOCK
    masks = _head_masks()
    log2e = math.log2(math.e)
    low_half = lax.broadcasted_iota(jnp.int32, (1, 2 * HEAD_DIM), 1) < HEAD_DIM

    def rope(x, r0, n):
        swapped = _dot(x.astype(BF16), pswap_ref[...])
        return x * cos_ref[pl.ds(r0, n), :] + swapped * sin_ref[pl.ds(r0, n), :]

    @pl.when(g == 0)
    def _():
        def fill(i, carry):
            r0 = pl.multiple_of(i * ROW_TILE, ROW_TILE)
            k = kv_ref[0, pl.ds(r0, ROW_TILE), 0:2 * HEAD_DIM]
            v = kv_ref[0, pl.ds(r0, ROW_TILE), 2 * HEAD_DIM:4 * HEAD_DIM]
            k_sw = pltpu.roll(k, HEAD_DIM, 1)
            v_sw = pltpu.roll(v, HEAD_DIM, 1)
            k4 = jnp.concatenate([jnp.where(low_half, k, k_sw), jnp.where(low_half, k_sw, k)], axis=1)
            kb_ref[pl.ds(r0, ROW_TILE), :] = rope(k4, r0, ROW_TILE).astype(BF16)
            vb_ref[pl.ds(r0, ROW_TILE), :] = jnp.concatenate(
                [jnp.where(low_half, v, 1.0), jnp.where(low_half, v_sw, 1.0)], axis=1).astype(BF16)
            return carry
        lax.fori_loop(0, lt // ROW_TILE, fill, 0)

    k_ctx = kb_ref[0:ctx_len, :]
    v_ctx = vb_ref[0:ctx_len, :]
    neg_inf = jnp.float32(-jnp.inf)
    rel0 = (lax.broadcasted_iota(jnp.int32, (ATT_BLOCK, nwin), 1)
            - lax.broadcasted_iota(jnp.int32, (ATT_BLOCK, nwin), 0))

    def block(u):
        j = g * ATT_STEP_BLOCKS + u
        rows = slice(u * ATT_BLOCK, (u + 1) * ATT_BLOCK)
        is_lat = j >= n_ctx_blk
        i_lat = jnp.maximum(j - n_ctx_blk, 0)
        w0 = pl.multiple_of(ctx_len + jnp.clip((i_lat - 1) * ATT_BLOCK, 0, lat_len - nwin), ATT_BLOCK)
        q0 = pl.multiple_of(j * ATT_BLOCK, ATT_BLOCK)
        q = rope(q_ref[0, rows, :], q0, ATT_BLOCK) * (HEAD_DIM ** -0.5 * log2e)
        k_win = kb_ref[pl.ds(w0, nwin), :]
        v_win = vb_ref[pl.ds(w0, nwin), :]
        band = is_lat & (jnp.abs(rel0 + (w0 - q0)) <= WINDOW)
        q_stack = jnp.concatenate([(q * masks[h]).astype(BF16) for h in range(N_HEADS)], axis=0)
        band4 = jnp.concatenate([band] * N_HEADS, axis=0)
        sink = jnp.concatenate([jnp.broadcast_to(sink_ref[h] * log2e, (ATT_BLOCK, 128)) for h in range(N_HEADS)],
                               axis=0)
        s_loc = jnp.where(band4, _dot_nt(q_stack, k_win), neg_inf)
        s_ctx = _dot_nt(q_stack, k_ctx)
        yield
        m = jnp.maximum(jnp.max(s_loc, axis=-1, keepdims=True), jnp.max(s_ctx, axis=-1, keepdims=True))
        m128 = jnp.maximum(jnp.broadcast_to(m, (N_HEADS * ATT_BLOCK, 128)), sink)
        p_loc = jnp.exp2(s_loc - jnp.concatenate([m128] * (nwin // 128), axis=1)).astype(BF16)
        p_ctx = jnp.exp2(s_ctx - jnp.concatenate([m128] * (ctx_len // 128), axis=1)).astype(BF16)
        yield
        o_all = _dot(p_loc, v_win) + _dot(p_ctx, v_ctx)
        e_sink = jnp.exp2(sink - m128)
        o = jnp.zeros((ATT_BLOCK, W_GROUP), F32)
        for h in range(N_HEADS):
            hr = slice(h * ATT_BLOCK, (h + 1) * ATT_BLOCK)
            if h % 2 == 0:
                num, den = o_all[hr], pltpu.roll(o_all[hr], W_GROUP - HEAD_DIM, 1)
            else:
                num, den = pltpu.roll(o_all[hr], HEAD_DIM, 1), o_all[hr]
            den = den + jnp.concatenate([e_sink[hr], e_sink[hr]], axis=1)
            o = jnp.where(masks[h] > 0.0, num / den, o)
        out_ref[0, rows, :] = (o * _silu(sg_ref[0, rows, :])).astype(out_ref.dtype)

    _interleave(*[block(u) for u in range(ATT_STEP_BLOCKS)])


def _attn(p, sink, consts, ctx_len):
    b_, lt, _ = p.shape
    step_rows = ATT_STEP_BLOCKS * ATT_BLOCK
    nb = lt // step_rows
    kern = functools.partial(_attn_kernel, ctx_len=ctx_len, lat_len=lt - ctx_len)
    return pl.pallas_call(
        kern,
        grid=(b_, nb),
        in_specs=[pl.BlockSpec((1, step_rows, W_GROUP), lambda b, j: (b, j, AQ)),
                  pl.BlockSpec((1, lt, W_GROUP), lambda b, j: (b, 0, AKV)),
                  pl.BlockSpec((1, step_rows, W_GROUP), lambda b, j: (b, j, AG)),
                  pl.BlockSpec((lt, W_GROUP), lambda b, j: (0, 0)),
                  pl.BlockSpec((lt, W_GROUP), lambda b, j: (0, 0)),
                  pl.BlockSpec((N_HEADS, 1, 1), lambda b, j: (0, 0, 0)),
                  pl.BlockSpec((W_GROUP, W_GROUP), lambda b, j: (0, 0))],
        out_specs=pl.BlockSpec((1, step_rows, W_GROUP), lambda b, j: (b, j, 0)),
        out_shape=jax.ShapeDtypeStruct((b_, lt, W_GROUP), BF16),
        scratch_shapes=[pltpu.VMEM((lt, W_GROUP), BF16), pltpu.VMEM((lt, W_GROUP), BF16)],
        compiler_params=pltpu.CompilerParams(vmem_limit_bytes=VMEM_LIMIT,
                                             dimension_semantics=("arbitrary", "arbitrary")),
        name="window_attention",
    )(p, p, p, consts["rope_cos"], consts["rope_sin"], sink.reshape(N_HEADS, 1, 1), consts["pswap"])


class _FftPlan:
    def __init__(self, n1):
        self.n1 = n1
        self.n = n1 * FFT_N2
        self.k1p = -(-(n1 // 2 + 1) // 8) * 8


def _fft_constants(n1, n_in, n_out):
    plan = _FftPlan(n1)
    n, k1p, half = plan.n, plan.k1p, n1 // 2
    k1 = np.arange(k1p)[:, None].astype(np.float64)
    live = (np.arange(k1p) <= half)[:, None]
    i1 = np.arange(n1)[None, :].astype(np.float64)
    ang = 2.0 * np.pi * k1 * i1 / n1
    f1 = np.concatenate([np.where(live, np.cos(ang), 0.0), np.where(live, -np.sin(ang), 0.0)], axis=0)
    i2 = np.arange(FFT_N2)[None, :].astype(np.float64)
    phi = 2.0 * np.pi * k1 * i2 / n
    tw = np.stack([np.cos(phi), -np.sin(phi)], axis=0)
    tw = np.broadcast_to(tw.reshape(2, k1p * FFT_N2, 1), (2, k1p * FFT_N2, 128))
    th = 2.0 * np.pi * np.outer(np.arange(FFT_N2), np.arange(FFT_N2)) / FFT_N2
    c2, s2 = np.cos(th), np.sin(th)
    m_fwd = np.block([[c2, s2], [-s2, c2]])
    m_inv = np.block([[c2, -s2], [s2, c2]])
    wk = np.where(np.arange(k1p) <= half, 2.0, 0.0)
    wk[0] = 1.0
    wk[half] = 1.0
    psi = 2.0 * np.pi * np.outer(np.arange(n1), np.arange(k1p)) / n1
    g = np.concatenate([np.cos(psi) * wk[None, :], -np.sin(psi) * wk[None, :]], axis=1) / n

    def hl(a):
        a32 = jnp.asarray(a, F32)
        hi = a32.astype(BF16)
        lo = (a32 - hi.astype(F32)).astype(BF16)
        return hi, lo

    return dict(f1=hl(f1), f1_in=hl(f1[:, 0:n_in]), g=hl(g[0:n_out]), m_fwd=hl(m_fwd), m_inv=hl(m_inv),
                tw=jnp.asarray(tw, F32))


def _slab_row(k):
    return k * FFT_PITCH if isinstance(k, int) else pl.multiple_of(k * FFT_PITCH, 8)


def _load_slab(ref, k):
    r0 = _slab_row(k)
    return jnp.concatenate([ref[0, pl.ds(r0, FFT_N2), :], ref[1, pl.ds(r0, FFT_N2), :]], axis=1)


def _store_slab(ref, k, val):
    r0 = _slab_row(k)
    ref[0, pl.ds(r0, FFT_N2), :] = val[:, 0:128]
    ref[1, pl.ds(r0, FFT_N2), :] = val[:, 128:256]


FFT_GROUP = 8


def _cross_slab(src_refs, n_src, mat_h, mat_l, dst_refs, n_dst, dot=_dot_const):
    def body(gidx, carry):
        cols = []
        for i in range(FFT_GROUP):
            i2 = gidx * FFT_GROUP + i
            parts = []
            for half in range(2):
                parts.append(jnp.concatenate(
                    [r[half, pl.ds(i2, n_src, stride=FFT_PITCH), :] for r in src_refs], axis=0))
            cols.append(jnp.concatenate(parts, axis=1))
        x = jnp.concatenate(cols, axis=1)
        y = dot(mat_h, mat_l, x)
        for i in range(FFT_GROUP):
            i2 = gidx * FFT_GROUP + i
            for half in range(2):
                c0 = (2 * i + half) * 128
                for d, ref in enumerate(dst_refs):
                    ref[half, pl.ds(i2, n_dst, stride=FFT_PITCH), :] = y[d * n_dst:(d + 1) * n_dst, c0:c0 + 128]
        return carry

    lax.fori_loop(0, FFT_N2 // FFT_GROUP, body, 0)


def _twiddle(ar, ai, tw_ref, k, conj):
    r0 = pl.multiple_of(k * FFT_N2, FFT_N2)
    tr = tw_ref[0, pl.ds(r0, FFT_N2), :]
    ti = tw_ref[1, pl.ds(r0, FFT_N2), :]
    tr = jnp.concatenate([tr, tr], axis=1)
    ti = jnp.concatenate([ti, ti], axis=1)
    if conj:
        return ar * tr + ai * ti, ai * tr - ar * ti
    return ar * tr - ai * ti, ar * ti + ai * tr


def _filter_kernel(feat_ref, dec_ref, w1_ref, b1_ref, w2_ref, b2_ref, w3_ref, b3_ref, fq_ref,
                   f1h_ref, f1l_ref, mh_ref, ml_ref, tw_ref, out_ref, zbuf, are, aim, *, n1, k1p):
    half = n1 // 2

    u_ = FFT_SLAB_UNROLL
    rows = u_ * FFT_N2

    def fill(g, carry):
        r0 = pl.multiple_of(g * rows, rows)
        z = feat_ref[pl.ds(r0, rows), :]
        h = jnp.sin(fq_ref[0, 0:1, :] * (_dot_f32(z, w1_ref[0]) + b1_ref[0]))
        h = jnp.sin(fq_ref[0, 1:2, :] * (_dot_f32(h, w2_ref[0]) + b2_ref[0]))
        h = _dot_f32(h, w3_ref[0]) + b3_ref[0]
        hsel = jnp.where(g * u_ < half, h[:, 0:W_GROUP], h[:, W_GROUP:2 * W_GROUP])
        kern = hsel * dec_ref[pl.ds(r0, rows), :]
        for u in range(u_):
            _store_slab(zbuf, g * u_ + u, kern[u * FFT_N2:(u + 1) * FFT_N2])
        return carry

    lax.fori_loop(0, n1 // u_, fill, 0)
    _cross_slab([zbuf], n1, f1h_ref[...], f1l_ref[...], [are, aim], k1p)

    def slab(g, carry):
        cols = []
        for u in range(u_):
            k = g * u_ + u
            ar, ai = _twiddle(_load_slab(are, k), _load_slab(aim, k), tw_ref, k, conj=False)
            cols.append(jnp.concatenate([ar, ai], axis=0))
        x = _dot_const(mh_ref[...], ml_ref[...], jnp.concatenate(cols, axis=1))
        for u in range(u_):
            r0 = pl.multiple_of((g * u_ + u) * FFT_N2, FFT_N2)
            out_ref[0, 0, pl.ds(r0, FFT_N2), :] = x[0:FFT_N2, u * W_GROUP:(u + 1) * W_GROUP]
            out_ref[0, 1, pl.ds(r0, FFT_N2), :] = x[FFT_N2:2 * FFT_N2, u * W_GROUP:(u + 1) * W_GROUP]
        return carry

    lax.fori_loop(0, k1p // u_, slab, 0)


def _filter_tables(l_):
    pos = np.concatenate([np.arange(l_), l_ - np.arange(l_)]).astype(np.float32)
    pos[l_] = 0.0
    t = pos / np.float32(max(l_ - 1, 1))
    w = np.float32(2.0 * math.pi) * pos / np.float32(l_)
    f = np.linspace(1e-4, HYENA_BANDS - 1, HYENA_BANDS, dtype=np.float32)
    feat = np.zeros((2 * l_, 64), np.float32)
    feat[:, 0] = t
    feat[:, 1:1 + HYENA_BANDS] = np.cos(w[:, None] * f)
    feat[:, 1 + HYENA_BANDS:HYENA_EMB] = -np.sin(w[:, None] * f)
    deltas = np.abs(np.linspace(HYENA_MIN_DECAY, HYENA_MAX_DECAY, W_GROUP, dtype=np.float32))
    dec = np.exp(-t[:, None] * deltas[None, :]).astype(np.float32)
    dec[l_] = 0.0
    return jnp.asarray(feat), jnp.asarray(dec)


def _filter_spectra(l_, w1p, b1, w2, b2, w3, b3, freq, fc):
    n1 = 2 * l_ // FFT_N2
    plan = _FftPlan(n1)
    k1p = plan.k1p
    feat, dec = _filter_tables(l_)
    full = lambda shape: pl.BlockSpec(shape, lambda l: (0,) * len(shape))
    per = lambda shape: pl.BlockSpec((1,) + shape, lambda l: (l,) + (0,) * len(shape))
    kern = functools.partial(_filter_kernel, n1=n1, k1p=k1p)
    return pl.pallas_call(
        kern,
        grid=(DEPTH,),
        in_specs=[full((2 * l_, 64)), full((2 * l_, W_GROUP)),
                  per((64, HYENA_FFN)), per((1, HYENA_FFN)), per((HYENA_FFN, HYENA_FFN)), per((1, HYENA_FFN)),
                  per((HYENA_FFN, 2 * W_GROUP)), per((1, 2 * W_GROUP)), per((2, HYENA_FFN)),
                  full((2 * k1p, n1)), full((2 * k1p, n1)), full((128, 128)), full((128, 128)),
                  full((2, k1p * FFT_N2, 128))],
        out_specs=pl.BlockSpec((1, 2, k1p * FFT_N2, W_GROUP), lambda l: (l, 0, 0, 0)),
        out_shape=jax.ShapeDtypeStruct((DEPTH, 2, k1p * FFT_N2, W_GROUP), F32),
        scratch_shapes=[pltpu.VMEM((2, n1 * FFT_PITCH, 128), F32),
                        pltpu.VMEM((2, k1p * FFT_PITCH, 128), F32),
                        pltpu.VMEM((2, k1p * FFT_PITCH, 128), F32)],
        compiler_params=pltpu.CompilerParams(vmem_limit_bytes=VMEM_LIMIT),
        name="hyena_filter_spectrum_%d" % l_,
    )(feat, dec, w1p, b1.reshape(DEPTH, 1, HYENA_FFN), w2, b2.reshape(DEPTH, 1, HYENA_FFN),
      w3, b3.reshape(DEPTH, 1, 2 * W_GROUP), freq,
      fc["f1"][0], fc["f1"][1], fc["m_fwd"][0], fc["m_fwd"][1], fc["tw"])


def _hyena_segment(v_ref, x1_ref, x0_ref, sg_ref, cw_ref, cb_ref, d_ref, kf_ref, consts, scratch, out_ref,
                   *, row0, seg_len, n1):
    f1h, f1l, gh, gl, mfh, mfl, mih, mil, tw_ref = consts
    zbuf, are, aim, ybuf = scratch
    n_sig = seg_len // FFT_N2
    n_in = max(n_sig, 8)
    n_out = max(n_sig, 8)
    k1p = _FftPlan(n1).k1p
    row_i = lax.broadcasted_iota(jnp.int32, (FFT_N2, W_GROUP), 0)

    def conv3(ref, blk, jslab):
        r = pl.multiple_of(row0 + jslab * FFT_N2, FFT_N2)
        cur = ref[0, pl.ds(r, FFT_N2), :]
        pe = ref[0, pl.ds(jnp.maximum(r - 1, row0), 1), :] * jnp.where(jslab > 0, 1.0, 0.0)
        ne = ref[0, pl.ds(jnp.minimum(r + FFT_N2, row0 + seg_len - 1), 1), :] * jnp.where(jslab < n_sig - 1, 1.0, 0.0)
        prev = jnp.where(row_i == 0, pe, pltpu.roll(cur, 1, 0))
        nxt = jnp.where(row_i == FFT_N2 - 1, ne, pltpu.roll(cur, FFT_N2 - 1, 0))
        c0 = blk * W_GROUP
        return (cw_ref[0:1, c0:c0 + W_GROUP] * prev + cw_ref[1:2, c0:c0 + W_GROUP] * cur
                + cw_ref[2:3, c0:c0 + W_GROUP] * nxt + cb_ref[:, c0:c0 + W_GROUP])

    def pre(jslab, carry):
        _store_slab(zbuf, jslab, conv3(v_ref, 0, jslab) * conv3(x1_ref, 1, jslab))
        return carry

    lax.fori_loop(0, n_sig, pre, 0)
    for jz in range(n_sig, n_in):
        _store_slab(zbuf, jz, jnp.zeros((FFT_N2, W_GROUP), F32))

    _cross_slab([zbuf], n_in, f1h, f1l, [are, aim], k1p, dot=_dot_const2)

    u_ = FFT_SLAB_UNROLL

    def slab(g, carry):
        cols = []
        for u in range(u_):
            k = g * u_ + u
            ar, ai = _twiddle(_load_slab(are, k), _load_slab(aim, k), tw_ref, k, conj=False)
            cols.append(jnp.concatenate([ar, ai], axis=0))
        x = _dot_const2(mfh, mfl, jnp.concatenate(cols, axis=1))
        cols = []
        for u in range(u_):
            xr = x[0:FFT_N2, u * W_GROUP:(u + 1) * W_GROUP]
            xi = x[FFT_N2:2 * FFT_N2, u * W_GROUP:(u + 1) * W_GROUP]
            r0 = pl.multiple_of((g * u_ + u) * FFT_N2, FFT_N2)
            kr = kf_ref[0, 0, pl.ds(r0, FFT_N2), :]
            ki = kf_ref[0, 1, pl.ds(r0, FFT_N2), :]
            cols.append(jnp.concatenate([xr * kr - xi * ki, xr * ki + xi * kr], axis=0))
        bm = _dot_const2(mih, mil, jnp.concatenate(cols, axis=1))
        for u in range(u_):
            k = g * u_ + u
            br, bi = _twiddle(bm[0:FFT_N2, u * W_GROUP:(u + 1) * W_GROUP],
                              bm[FFT_N2:2 * FFT_N2, u * W_GROUP:(u + 1) * W_GROUP], tw_ref, k, conj=True)
            _store_slab(are, k, br)
            _store_slab(aim, k, bi)
        return carry

    lax.fori_loop(0, k1p // u_, slab, 0)
    _cross_slab([are, aim], k1p, gh, gl, [ybuf], n_out, dot=_dot_const2)

    def post(jslab, carry):
        r = pl.multiple_of(row0 + jslab * FFT_N2, FFT_N2)
        y = _load_slab(ybuf, jslab) + _load_slab(zbuf, jslab) * d_ref[...]
        out_ref[0, pl.ds(r, FFT_N2), :] = (conv3(x0_ref, 2, jslab) * y
                                           * _silu(sg_ref[0, pl.ds(r, FFT_N2), :])).astype(out_ref.dtype)
        return carry

    lax.fori_loop(0, n_sig, post, 0)


def _hyena_kernel(v_ref, x1_ref, x0_ref, sg_ref, cw_ref, cb_ref, d_ref, kfc_ref, kfl_ref,
                  c_f1h, c_f1l, c_gh, c_gl, c_tw, l_f1h, l_f1l, l_gh, l_gl, l_tw,
                  mfh_ref, mfl_ref, mih_ref, mil_ref, out_ref,
                  zc, arc, aic, yc, zl, arl, ail, yl, *, ctx_len, lat_len):
    mats = (mfh_ref[...], mfl_ref[...], mih_ref[...], mil_ref[...])
    seg = functools.partial(_hyena_segment, v_ref, x1_ref, x0_ref, sg_ref, cw_ref, cb_ref, d_ref)
    seg(kfc_ref, (c_f1h[...], c_f1l[...], c_gh[...], c_gl[...]) + mats + (c_tw,), (zc, arc, aic, yc), out_ref,
        row0=0, seg_len=ctx_len, n1=2 * ctx_len // FFT_N2)
    seg(kfl_ref, (l_f1h[...], l_f1l[...], l_gh[...], l_gl[...]) + mats + (l_tw,), (zl, arl, ail, yl), out_ref,
        row0=ctx_len, seg_len=lat_len, n1=2 * lat_len // FFT_N2)


def _hyena(p, conv_w, conv_b, d, kf_ctx, kf_lat, layer, fc_ctx, fc_lat, ctx_len):
    b_, lt, _ = p.shape
    lat_len = lt - ctx_len
    n1c, n1l = 2 * ctx_len // FFT_N2, 2 * lat_len // FFT_N2
    kc, kl = _FftPlan(n1c).k1p, _FftPlan(n1l).k1p
    blk = lambda j: pl.BlockSpec((1, lt, W_GROUP), lambda b: (b, 0, j))
    full = lambda a: pl.BlockSpec(a.shape, lambda b: (0,) * a.ndim)
    per_layer = lambda a: pl.BlockSpec((1,) + a.shape[1:], lambda b: (layer,) + (0,) * (a.ndim - 1))
    cst = [fc_ctx["f1_in"][0], fc_ctx["f1_in"][1], fc_ctx["g"][0], fc_ctx["g"][1], fc_ctx["tw"],
           fc_lat["f1_in"][0], fc_lat["f1_in"][1], fc_lat["g"][0], fc_lat["g"][1], fc_lat["tw"],
           fc_lat["m_fwd"][0], fc_lat["m_fwd"][1], fc_lat["m_inv"][0], fc_lat["m_inv"][1]]
    small = [conv_w, conv_b.reshape(1, 3 * W_GROUP), d.reshape(1, W_GROUP)]
    sbuf = lambda n: pltpu.VMEM((2, n * FFT_PITCH, 128), F32)
    kern = functools.partial(_hyena_kernel, ctx_len=ctx_len, lat_len=lat_len)
    return pl.pallas_call(
        kern,
        grid=(b_,),
        in_specs=([blk(HV), blk(HX1), blk(HX0), blk(HG)] + [full(a) for a in small]
                  + [per_layer(kf_ctx), per_layer(kf_lat)] + [full(a) for a in cst]),
        out_specs=pl.BlockSpec((1, lt, W_GROUP), lambda b: (b, 0, 0)),
        out_shape=jax.ShapeDtypeStruct((b_, lt, W_GROUP), BF16),
        scratch_shapes=[sbuf(max(n1c // 2, 8)), sbuf(kc), sbuf(kc), sbuf(max(n1c // 2, 8)),
                        sbuf(n1l // 2), sbuf(kl), sbuf(kl), sbuf(n1l // 2)],
        compiler_params=pltpu.CompilerParams(vmem_limit_bytes=VMEM_LIMIT),
        name="hyena_mixer",
    )(p, p, p, p, *small, kf_ctx, kf_lat, *cst)


def _rope_tables(ctx_len, lat_len):
    pos = np.arange(lat_len)
    row = (pos // GRID_W).astype(np.float32)
    col = (pos % GRID_W).astype(np.float32)
    inv = (np.float32(ROPE_BASE) ** (-np.arange(ROPE_AXIS_FREQS, dtype=np.float32) / np.float32(ROPE_AXIS_FREQS)))
    ang = np.concatenate([row[:, None] * inv, col[:, None] * inv], axis=-1).astype(np.float32)
    cos, sin = np.cos(ang), np.sin(ang)
    cos_h = np.concatenate([cos, cos], axis=-1)
    sin_h = np.concatenate([-sin, sin], axis=-1)
    cos_l = np.tile(cos_h, (1, N_HEADS))
    sin_l = np.tile(sin_h, (1, N_HEADS))
    cos_t = np.concatenate([np.ones((ctx_len, W_GROUP), np.float32), cos_l], axis=0)
    sin_t = np.concatenate([np.zeros((ctx_len, W_GROUP), np.float32), sin_l], axis=0)
    return jnp.asarray(cos_t, F32), jnp.asarray(sin_t, F32)


def _rope_partner():
    d = np.arange(W_GROUP)
    partner = np.where(d % HEAD_DIM < HEAD_DIM // 2, d + HEAD_DIM // 2, d - HEAD_DIM // 2)
    p = np.zeros((W_GROUP, W_GROUP), np.float32)
    p[partner, d] = 1.0
    return jnp.asarray(p, BF16)


def _mlstm_expansion():
    e = np.zeros((128, 3 * W_GROUP), np.float32)
    for j in range(3):
        for h in range(N_HEADS):
            e[8 * (j + 1) + h, j * W_GROUP + h * HEAD_DIM:j * W_GROUP + (h + 1) * HEAD_DIM] = 1.0
    return jnp.asarray(e, BF16)


def _tri(n):
    t = np.tril(np.ones((n, n), np.float32))
    return jnp.asarray(t, BF16), jnp.asarray(t.T, BF16)


def _w_layout_kernel(w_ref, o_ref):
    h0 = 5 * W_GROUP + 16
    g0 = h0 + 4 * W_GROUP
    d0 = g0 + 4 * W_GROUP + 2 * GLA_RANK
    w = w_ref[0]
    o_ref[0] = jnp.concatenate([
        w[:, 0:5 * W_GROUP],
        w[:, h0:h0 + 4 * W_GROUP],
        w[:, g0:g0 + 4 * W_GROUP],
        w[:, d0:d0 + 3 * W_GROUP],
        w[:, 5 * W_GROUP:5 * W_GROUP + 16],
        w[:, g0 + 4 * W_GROUP:g0 + 4 * W_GROUP + 2 * GLA_RANK],
        jnp.zeros((w.shape[0], 128 - 48), F32)], axis=1).astype(BF16)


def _layout_w_in(w_in):
    n_in = w_in.shape[-1]
    wide = pl.pallas_call(
        _w_layout_kernel,
        grid=(DEPTH, D_MODEL // ROW_TILE),
        in_specs=[pl.BlockSpec((1, ROW_TILE, n_in), lambda l, i: (l, i, 0))],
        out_specs=pl.BlockSpec((1, ROW_TILE, N_PROJ), lambda l, i: (l, i, 0)),
        out_shape=jax.ShapeDtypeStruct((DEPTH, D_MODEL, N_PROJ), BF16),
        compiler_params=pltpu.CompilerParams(vmem_limit_bytes=VMEM_LIMIT),
        name="inproj_weight_layout",
    )(w_in)
    w_g = jnp.swapaxes(w_in[..., 5 * W_GROUP:5 * W_GROUP + 16], -1, -2)
    w_g = w_g.reshape(w_g.shape[:-2] + (4, N_HEADS, w_g.shape[-1]))
    w_st = jnp.pad(w_g, [(0, 0)] * (w_g.ndim - 2) + [(0, 8 - N_HEADS), (0, 0)])
    w_st = w_st.reshape(w_st.shape[:-3] + (GATE_ROWS, w_st.shape[-1]))
    return wide.astype(BF16), w_st.astype(BF16)


def kernel(x, c, ctx, c_ctx, w_ada, b_ada, g_pre, g_post, w_in, mlstm_gate_b, mlstm_norm_g, hyena_conv_w,
           hyena_conv_b, hyena_w1, hyena_b1, hyena_w2, hyena_b2, hyena_w3, hyena_b3, hyena_freq, hyena_d,
           gla_w_alpha, gla_b_alpha, gla_norm_g, attn_sink, w_out):
    b_, lat_len, _ = x.shape
    ctx_len = ctx.shape[1]
    n_ctx_tiles = ctx_len // ROW_TILE

    tril_m, triu_m = _tri(MLSTM_CHUNK)
    tril_g, triu_g = _tri(GLA_CHUNK)
    bd = np.kron(np.eye(N_HEADS, dtype=np.float32), np.ones((HEAD_DIM, HEAD_DIM), np.float32))
    rope_cos, rope_sin = _rope_tables(ctx_len, lat_len)
    mones_m = np.kron(np.eye(N_HEADS, dtype=np.float32), np.ones((MLSTM_CHUNK, HEAD_DIM), np.float32))
    consts = dict(tril_m=tril_m, triu_m=triu_m, tril_g=tril_g, triu_g=triu_g, bd=jnp.asarray(bd, BF16),
                  mones_m=jnp.asarray(mones_m, BF16), eall_m=_mlstm_expansion(),
                  rope_cos=rope_cos, rope_sin=rope_sin, pswap=_rope_partner())
    fc_ctx = _fft_constants(2 * ctx_len // FFT_N2, max(ctx_len // FFT_N2, 8), max(ctx_len // FFT_N2, 8))
    fc_lat = _fft_constants(2 * lat_len // FFT_N2, lat_len // FFT_N2, lat_len // FFT_N2)

    cc = jnp.zeros((16, D_MODEL), F32).at[0:b_].set(c).at[b_].set(c_ctx)
    mod = _modulation(cc, w_ada, b_ada)
    mod_ctx = jnp.broadcast_to(mod[:, b_:b_ + 1], (DEPTH, b_, 3 * D_MODEL))
    mod_all = jnp.stack([mod_ctx, mod[:, 0:b_]], axis=2).reshape(DEPTH, 2 * b_, 1, 3 * D_MODEL)

    w_p, w_st = _layout_w_in(w_in)
    w_o = w_out.astype(BF16)
    w1p = jnp.zeros((DEPTH, 64, HYENA_FFN), F32).at[:, 0:HYENA_EMB].set(hyena_w1)
    filt = (w1p, hyena_b1, hyena_w2, hyena_b2, hyena_w3, hyena_b3, hyena_freq)
    kf_ctx = _filter_spectra(ctx_len, *filt, fc_ctx)
    kf_lat = _filter_spectra(lat_len, *filt, fc_lat)

    xs = (ctx, x)
    for l in range(DEPTH):
        last = l == DEPTH - 1
        p, gt = _inproj(xs, mod_all, l, g_pre[l], w_p, w_st, n_ctx_tiles)
        a = _mlstm(p, gt, mlstm_gate_b[l], mlstm_norm_g[l], consts, ctx_len)
        hy = _hyena(p, hyena_conv_w[l], hyena_conv_b[l], hyena_d[l], kf_ctx, kf_lat, l, fc_ctx, fc_lat, ctx_len)
        g = _gla(p, gla_w_alpha[l], gla_b_alpha[l], gla_norm_g[l], consts, ctx_len)
        d = _attn(p, attn_sink[l], consts, ctx_len)
        xs = _outproj((a, hy, g, d), w_o, l, g_post[l], mod_all, xs, n_ctx_tiles, skip_ctx=last)
    return xs
```

```python
import functools
import math

import numpy as np
import jax
import jax.numpy as jnp
from jax import lax
from jax.experimental import pallas as pl
from jax.experimental.pallas import tpu as pltpu

F32 = jnp.float32
BF16 = jnp.bfloat16

D_MODEL = 1024
DEPTH = 4
GRID_W = 64
W_GROUP = 256
HEAD_DIM = 64
N_HEADS = 4
ATT_KV_HEADS = 2
WINDOW = 128
ATT_BLOCK = 128
GLA_RANK = 16
GLA_NORMALIZER = 16.0
HYENA_BANDS = 16
HYENA_EMB = 1 + 2 * HYENA_BANDS
HYENA_FFN = 64
HYENA_MIN_DECAY = math.log(1e-2) / 1.5
HYENA_MAX_DECAY = math.log(1e-2) / 0.3
ROPE_BASE = 10000.0
ROPE_AXIS_FREQS = HEAD_DIM // 4
EPS = 1e-6

N_WIDE_BLOCKS = 16
SMALL_COL0 = N_WIDE_BLOCKS * W_GROUP
N_PROJ = SMALL_COL0 + 128
SMALL_BLOCK = SMALL_COL0 // 128
(MQ, MK, MV, MO, MG, HV, HX1, HX0, HG, GQ, GK, GV, GG, AQ, AKV, AG) = range(N_WIDE_BLOCKS)

GATE_ROWS = 32
ROW_TILE = 256
MLSTM_CHUNK = 256
GLA_CHUNK = 128
ATT_STEP_BLOCKS = 3
FFT_N2 = 64
FFT_PITCH = 72
FFT_SLAB_UNROLL = 4
VMEM_LIMIT = 56 * 1024 * 1024


def _dot(a, b):
    return jnp.dot(a, b, preferred_element_type=F32)


def _dot_nt(a, b):
    return lax.dot_general(a, b, (((1,), (1,)), ((), ())), preferred_element_type=F32)


def _dot_tn(a, b):
    return lax.dot_general(a, b, (((0,), (0,)), ((), ())), preferred_element_type=F32)


def _split2(x):
    hi = x.astype(BF16)
    lo = (x - hi.astype(F32)).astype(BF16)
    return hi, lo


def _split3(x):
    h1 = x.astype(BF16)
    r1 = x - h1.astype(F32)
    h2 = r1.astype(BF16)
    h3 = (r1 - h2.astype(F32)).astype(BF16)
    return h1, h2, h3


def _dot_const(ch, cl, x):
    xh, xl = _split2(x)
    return _dot(ch, xh) + _dot(ch, xl) + _dot(cl, xh)


def _dot_const2(ch, cl, x):
    xb = x.astype(BF16)
    return _dot(ch, xb) + _dot(cl, xb)


def _dot_f32(a, b):
    ah, al = _split2(a)
    bh, bl = _split2(b)
    return _dot(ah, bh) + _dot(ah, bl) + _dot(al, bh)


def _dot_exact_lhs(c, x):
    x1, x2, x3 = _split3(x)
    return _dot(c, x1) + _dot(c, x2) + _dot(c, x3)


def _dot_exact_rhs(x, c):
    x1, x2, x3 = _split3(x)
    return _dot(x1, c) + _dot(x2, c) + _dot(x3, c)


def _log_sigmoid(x):
    return jnp.minimum(x, 0.0) - jnp.log(1.0 + jnp.exp(-jnp.abs(x)))


def _silu(x):
    return x * jax.nn.sigmoid(x)


def _head_masks():
    lane = lax.broadcasted_iota(jnp.int32, (1, W_GROUP), 1) // HEAD_DIM
    return [(lane == h).astype(F32) for h in range(N_HEADS)]


def _block_diag_mask():
    r = lax.broadcasted_iota(jnp.int32, (W_GROUP, W_GROUP), 0) // HEAD_DIM
    c = lax.broadcasted_iota(jnp.int32, (W_GROUP, W_GROUP), 1) // HEAD_DIM
    return (r == c).astype(F32)


def _spread(cols):
    lane = lax.broadcasted_iota(jnp.int32, (1, W_GROUP), 1)
    out = cols[N_HEADS - 1]
    for h in range(N_HEADS - 2, -1, -1):
        out = jnp.where(lane < (h + 1) * HEAD_DIM, cols[h], out)
    return out


def _head_rms_norm(h, bd_bf16, g_row):
    sq = h * h
    sh, sl = _split2(sq)
    ms = (_dot(sh, bd_bf16) + _dot(sl, bd_bf16)) * (1.0 / HEAD_DIM)
    return h * lax.rsqrt(ms + EPS) * g_row


def _mod_kernel(c_ref, w_ref, b_ref, o_ref):
    s = _silu(c_ref[...])
    o_ref[0] = _dot_f32(s, w_ref[0]) + b_ref[0]


def _modulation(cc, w_ada, b_ada):
    nblk = 3 * D_MODEL // 1024
    return pl.pallas_call(
        _mod_kernel,
        grid=(DEPTH, nblk),
        in_specs=[
            pl.BlockSpec((16, D_MODEL), lambda l, j: (0, 0)),
            pl.BlockSpec((1, D_MODEL, 1024), lambda l, j: (l, 0, j)),
            pl.BlockSpec((1, 1, 1024), lambda l, j: (l, 0, j)),
        ],
        out_specs=pl.BlockSpec((1, 16, 1024), lambda l, j: (l, 0, j)),
        out_shape=jax.ShapeDtypeStruct((DEPTH, 16, 3 * D_MODEL), F32),
        compiler_params=pltpu.CompilerParams(vmem_limit_bytes=VMEM_LIMIT),
        name="adaln_modulation",
    )(cc, w_ada, b_ada.reshape(DEPTH, 1, 3 * D_MODEL))


def _token_specs(xs, n_ctx_tiles, t0):
    shape = (1, ROW_TILE, D_MODEL)
    if not isinstance(xs, tuple):
        return [pl.BlockSpec(shape, lambda b, i: (b, i + t0, 0))], [xs]
    return [pl.BlockSpec(shape, lambda b, i: (b, jnp.minimum(i + t0, n_ctx_tiles - 1), 0)),
            pl.BlockSpec(shape, lambda b, i: (b, jnp.maximum(i + t0 - n_ctx_tiles, 0), 0))], list(xs)


def _token_tile(tok_refs, n_ctx_tiles, t0):
    if len(tok_refs) == 1:
        return tok_refs[0][0]
    return jnp.where(pl.program_id(1) + t0 < n_ctx_tiles, tok_refs[0][0], tok_refs[1][0])


def _inproj_kernel(*refs, n_tok, n_ctx_tiles):
    mod_ref, g_ref, w_ref, wst_ref, p_ref, gt_ref = refs[n_tok:]
    x = _token_tile(refs[:n_tok], n_ctx_tiles, 0)
    ms = jnp.mean(x * x, axis=-1, keepdims=True)
    y = x * lax.rsqrt(ms + EPS) * g_ref[...]
    sh = mod_ref[0, 0, :, 0:D_MODEL]
    sc = mod_ref[0, 0, :, D_MODEL:2 * D_MODEL]
    hb = (y * (1.0 + sc) + sh).astype(BF16)
    p_ref[0] = _dot(hb, w_ref[0])
    gt_ref[0] = _dot_nt(wst_ref[0], hb)


def _stream_shape(xs):
    if isinstance(xs, tuple):
        return xs[1].shape[0], xs[0].shape[1] + xs[1].shape[1]
    return xs.shape[0], xs.shape[1]


def _inproj(xs, mod_all, layer, g_pre, w_p, w_st, n_ctx_tiles):
    b_, lt = _stream_shape(xs)
    nt = lt // ROW_TILE
    tok_specs, tok_args = _token_specs(xs, n_ctx_tiles, 0)
    return pl.pallas_call(
        functools.partial(_inproj_kernel, n_tok=len(tok_args), n_ctx_tiles=n_ctx_tiles),
        grid=(b_, nt),
        in_specs=tok_specs + [
            pl.BlockSpec((1, 1, 1, 3 * D_MODEL),
                         lambda b, i: (layer, 2 * b + jnp.where(i >= n_ctx_tiles, 1, 0), 0, 0)),
            pl.BlockSpec((1, D_MODEL), lambda b, i: (0, 0)),
            pl.BlockSpec((1, D_MODEL, N_PROJ), lambda b, i: (layer, 0, 0)),
            pl.BlockSpec((1, GATE_ROWS, D_MODEL), lambda b, i: (layer, 0, 0)),
        ],
        out_specs=[
            pl.BlockSpec((1, ROW_TILE, N_PROJ), lambda b, i: (b, i, 0)),
            pl.BlockSpec((1, GATE_ROWS, ROW_TILE), lambda b, i: (b, 0, i)),
        ],
        out_shape=[
            jax.ShapeDtypeStruct((b_, lt, N_PROJ), F32),
            jax.ShapeDtypeStruct((b_, GATE_ROWS, lt), F32),
        ],
        compiler_params=pltpu.CompilerParams(vmem_limit_bytes=VMEM_LIMIT),
        name="inproj",
    )(*tok_args, mod_all, g_pre.reshape(1, D_MODEL), w_p, w_st)


def _outproj_kernel(a_ref, h_ref, g_ref, d_ref, w_ref, gp_ref, mod_ref, *refs, n_tok, n_ctx_tiles, t0):
    o_ref = refs[n_tok]
    mix = jnp.concatenate([a_ref[0], h_ref[0], g_ref[0], d_ref[0]], axis=1)
    y = _dot(mix, w_ref[0])
    ms = jnp.mean(y * y, axis=-1, keepdims=True)
    yn = y * lax.rsqrt(ms + EPS) * gp_ref[...]
    gt = mod_ref[0, 0, :, 2 * D_MODEL:3 * D_MODEL]
    o_ref[0] = _token_tile(refs[:n_tok], n_ctx_tiles, t0) + gt * yn


def _outproj(mix, w_o, layer, g_post, mod_all, xs, n_ctx_tiles, skip_ctx):
    b_, lt = _stream_shape(xs)
    nt = lt // ROW_TILE
    t0 = n_ctx_tiles if skip_ctx else 0
    rows = lambda b, i: (b, i + t0, 0)
    tok_specs, tok_args = _token_specs(xs, n_ctx_tiles, t0)
    return pl.pallas_call(
        functools.partial(_outproj_kernel, n_tok=len(tok_args), n_ctx_tiles=n_ctx_tiles, t0=t0),
        grid=(b_, nt - t0),
        in_specs=[pl.BlockSpec((1, ROW_TILE, W_GROUP), rows)] * 4 + [
            pl.BlockSpec((1, D_MODEL, D_MODEL), lambda b, i: (layer, 0, 0)),
            pl.BlockSpec((1, D_MODEL), lambda b, i: (0, 0)),
            pl.BlockSpec((1, 1, 1, 3 * D_MODEL),
                         lambda b, i: (layer, 2 * b + jnp.where(i + t0 >= n_ctx_tiles, 1, 0), 0, 0)),
        ] + tok_specs,
        out_specs=pl.BlockSpec((1, ROW_TILE, D_MODEL), lambda b, i: (b, i, 0)),
        out_shape=jax.ShapeDtypeStruct((b_, lt - t0 * ROW_TILE, D_MODEL), F32),
        compiler_params=pltpu.CompilerParams(vmem_limit_bytes=VMEM_LIMIT),
        name="outproj",
    )(*mix, w_o, g_post.reshape(1, D_MODEL), mod_all, *tok_args)


def _outin_kernel(a_ref, h_ref, g_ref, d_ref, w_ref, gp_ref, mod_ref, *refs, n_tok, n_ctx_tiles):
    modn_ref, gn_ref, wp_ref, wst_ref, xs_ref, p_ref, gt_ref = refs[n_tok:]
    mix = jnp.concatenate([a_ref[0], h_ref[0], g_ref[0], d_ref[0]], axis=1)
    y = _dot(mix, w_ref[0])
    ms = jnp.mean(y * y, axis=-1, keepdims=True)
    yn = y * lax.rsqrt(ms + EPS) * gp_ref[...]
    x = _token_tile(refs[:n_tok], n_ctx_tiles, 0) + mod_ref[0, 0, :, 2 * D_MODEL:3 * D_MODEL] * yn
    xs_ref[0] = x
    ms = jnp.mean(x * x, axis=-1, keepdims=True)
    yy = x * lax.rsqrt(ms + EPS) * gn_ref[...]
    hb = (yy * (1.0 + modn_ref[0, 0, :, D_MODEL:2 * D_MODEL]) + modn_ref[0, 0, :, 0:D_MODEL]).astype(BF16)
    p_ref[0] = _dot(hb, wp_ref[0])
    gt_ref[0] = _dot_nt(wst_ref[0], hb)


def _outin(mix, w_o, layer, g_post, g_pre_next, mod_all, xs, w_p, w_st, n_ctx_tiles):
    b_, lt = _stream_shape(xs)
    nt = lt // ROW_TILE
    rows = lambda b, i: (b, i, 0)
    tok_specs, tok_args = _token_specs(xs, n_ctx_tiles, 0)
    mod_spec = lambda lyr: pl.BlockSpec((1, 1, 1, 3 * D_MODEL),
                                        lambda b, i: (lyr, 2 * b + jnp.where(i >= n_ctx_tiles, 1, 0), 0, 0))
    return pl.pallas_call(
        functools.partial(_outin_kernel, n_tok=len(tok_args), n_ctx_tiles=n_ctx_tiles),
        grid=(b_, nt),
        in_specs=[pl.BlockSpec((1, ROW_TILE, W_GROUP), rows)] * 4 + [
            pl.BlockSpec((1, D_MODEL, D_MODEL), lambda b, i: (layer, 0, 0)),
            pl.BlockSpec((1, D_MODEL), lambda b, i: (0, 0)),
            mod_spec(layer)] + tok_specs + [
            mod_spec(layer + 1),
            pl.BlockSpec((1, D_MODEL), lambda b, i: (0, 0)),
            pl.BlockSpec((1, D_MODEL, N_PROJ), lambda b, i: (layer + 1, 0, 0)),
            pl.BlockSpec((1, GATE_ROWS, D_MODEL), lambda b, i: (layer + 1, 0, 0)),
        ],
        out_specs=[pl.BlockSpec((1, ROW_TILE, D_MODEL), rows),
                   pl.BlockSpec((1, ROW_TILE, N_PROJ), rows),
                   pl.BlockSpec((1, GATE_ROWS, ROW_TILE), lambda b, i: (b, 0, i))],
        out_shape=[jax.ShapeDtypeStruct((b_, lt, D_MODEL), F32),
                   jax.ShapeDtypeStruct((b_, lt, N_PROJ), F32),
                   jax.ShapeDtypeStruct((b_, GATE_ROWS, lt), F32)],
        compiler_params=pltpu.CompilerParams(vmem_limit_bytes=VMEM_LIMIT),
        name="outproj_inproj",
    )(*mix, w_o, g_post.reshape(1, D_MODEL), mod_all, *tok_args, mod_all, g_pre_next.reshape(1, D_MODEL), w_p, w_st)


def _interleave(*gens):
    live = list(gens)
    while live:
        live = [g for g in live if next(g, StopIteration) is not StopIteration]


def _scan_chunk(s, reverse, n_ctx, n_tot):
    if not reverse:
        return s
    return jnp.where(s < n_ctx, n_ctx - 1 - s, n_tot - 1 - (s - n_ctx))


def _mlstm_kernel(q_ref, k_ref, v_ref, o_ref, sg_ref, gt_ref, brow_ref, ng_ref,
                  bd_ref, mones_ref, eall_ref, out_ref, hf_ref, hb_ref, c_ref, n_ref, pt_ref, r8_ref, ab_ref,
                  *, chunk, n_ctx, n_tot):
    t_ = chunk
    masks = _head_masks()
    masks_bf = [m.astype(BF16) for m in masks]
    bdm = _block_diag_mask()
    row_i = lax.broadcasted_iota(jnp.int32, (t_, t_), 0)
    col_i = lax.broadcasted_iota(jnp.int32, (t_, t_), 1)
    lane_a = lax.broadcasted_iota(jnp.int32, (8 * n_tot, t_), 1)
    live = (lax.broadcasted_iota(jnp.int32, (8 * n_tot, t_), 0) % 8) < N_HEADS
    c_ref[...] = jnp.zeros_like(c_ref)
    n_ref[...] = jnp.zeros_like(n_ref)

    def scan(x, reverse, op):
        k = 1
        while k < t_:
            if reverse:
                x = jnp.where(lane_a < t_ - k, op(x, pltpu.roll(x, t_ - k, 1)), x)
            else:
                x = jnp.where(lane_a >= k, op(x, pltpu.roll(x, k, 1)), x)
            k *= 2
        return x

    def gates(reverse):
        d_ = 1 if reverse else 0
        end = 0 if reverse else t_ - 1
        order = list(range(n_tot))
        if reverse:
            order = order[:n_ctx][::-1] + order[n_ctx:][::-1]

        def rows_of(g):
            r_ = slice(8 * g, 8 * g + 8)
            return jnp.concatenate([gt_ref[0, r_, c * t_:(c + 1) * t_] for c in range(n_tot)], axis=0) + jnp.concatenate(
                [brow_ref[r_, :]] * n_tot, axis=0)

        i_all = jnp.where(live, rows_of(2 * d_), 0.0)
        lf_all = jnp.where(live, _log_sigmoid(rows_of(2 * d_ + 1)), 0.0)
        b_all = scan(lf_all, reverse, jnp.add)
        r_all = i_all - b_all
        cm_all = scan(r_all, reverse, jnp.maximum)
        yield
        m_prev = jnp.zeros((8, 1), F32)
        for c in order:
            cols = slice(c * t_, (c + 1) * t_)
            blk8 = slice(8 * c, 8 * c + 8)
            i8, b8, r8 = i_all[blk8], b_all[blk8], r_all[blk8]
            inter = b8 + m_prev
            m_t = jnp.maximum(inter, b8 + cm_all[blk8])
            b_end = b8[:, end:end + 1]
            g8 = b_end - b8 + i8
            m_new = jnp.maximum(b_end + m_prev, jnp.max(g8, axis=1, keepdims=True))
            a_prev = jnp.exp(b_end + m_prev - m_new)
            packed = jnp.concatenate([b8 - m_t, jnp.exp(inter - m_t), jnp.exp(g8 - m_new), jnp.exp(-m_t),
                                      jnp.zeros((128 - 32, t_), F32)], axis=0)
            pt_ref[d_, cols, :] = packed.T
            r8_ref[d_, :, cols] = r8
            a_b = a_prev[0:1] * masks[0]
            for h in range(1, N_HEADS):
                a_b = a_b + a_prev[h:h + 1] * masks[h]
            ab_ref[d_, c] = a_b
            m_prev = m_new
            yield

    _interleave(gates(False), gates(True))

    def step(reverse, c_idx, h_ref):
        d_ = 1 if reverse else 0
        valid = (col_i >= row_i) if reverse else (col_i <= row_i)
        r0 = pl.multiple_of(c_idx * t_, t_)
        q = q_ref[0, pl.ds(r0, t_), :]
        k = k_ref[0, pl.ds(r0, t_), :] * (HEAD_DIM ** -0.5)
        qb = q.astype(BF16)
        kb = k.astype(BF16)
        vb = v_ref[0, pl.ds(r0, t_), :].astype(BF16)
        pt = pt_ref[d_, pl.ds(r0, t_), :]
        r8 = r8_ref[d_, :, pl.ds(r0, t_)]
        c_prev = c_ref[d_]
        n_prev = n_ref[d_]
        spread = _dot(pt.astype(BF16), eall_ref[...])
        s_cat = []
        for h in range(N_HEADS):
            w_intra = jnp.where(valid, jnp.exp(pt[:, h:h + 1] + r8[h:h + 1, :]), 0.0)
            s_cat.append((_dot_nt(qb * masks_bf[h], kb) * w_intra).astype(BF16))
        yield
        v_stack = jnp.concatenate([jnp.concatenate([vb * masks_bf[h] for h in range(N_HEADS)], axis=0),
                                   mones_ref[...]], axis=1)
        acc = _dot(jnp.concatenate(s_cat, axis=1), v_stack)
        qc = _dot(qb, c_prev.astype(BF16))
        qn = _dot((q * n_prev).astype(BF16), bd_ref[...])
        yield
        wi_b = spread[:, 0:W_GROUP]
        num = wi_b * qc + acc[:, 0:W_GROUP]
        den = wi_b * qn + acc[:, W_GROUP:2 * W_GROUP]
        h_ref[pl.ds(r0, t_), :] = num / jnp.maximum(jnp.abs(den), spread[:, 2 * W_GROUP:3 * W_GROUP])
        a_b = ab_ref[d_, c_idx]
        kw = k * spread[:, W_GROUP:2 * W_GROUP]
        c_ref[d_] = a_b * c_prev + bdm * _dot_tn(kw.astype(BF16), vb)
        n_ref[d_] = a_b * n_prev + jnp.sum(kw, axis=0, keepdims=True)

    def body(s, carry):
        _interleave(step(False, _scan_chunk(s, False, n_ctx, n_tot), hf_ref),
                    step(True, _scan_chunk(s, True, n_ctx, n_tot), hb_ref))
        return carry

    lax.fori_loop(0, n_tot, body, 0)

    def finish(c_idx, carry):
        r0 = pl.multiple_of(c_idx * t_, t_)
        hsum = hf_ref[pl.ds(r0, t_), :] + hb_ref[pl.ds(r0, t_), :]
        hsum = hsum * jax.nn.sigmoid(o_ref[0, pl.ds(r0, t_), :])
        hn = _head_rms_norm(hsum, bd_ref[...], ng_ref[...])
        out_ref[0, pl.ds(r0, t_), :] = (hn * _silu(sg_ref[0, pl.ds(r0, t_), :])).astype(out_ref.dtype)
        return carry

    lax.fori_loop(0, n_tot, finish, 0)


def _mlstm(p, gt, gate_b, norm_g, consts, ctx_len):
    b_, lt, _ = p.shape
    t_ = MLSTM_CHUNK
    brow = jnp.pad(gate_b.reshape(4, N_HEADS), ((0, 0), (0, 8 - N_HEADS))).reshape(GATE_ROWS, 1)
    blk = lambda j: pl.BlockSpec((1, lt, W_GROUP), lambda b: (b, 0, j))
    full = lambda shape: pl.BlockSpec(shape, lambda b: (0,) * len(shape))
    kern = functools.partial(_mlstm_kernel, chunk=t_, n_ctx=ctx_len // t_, n_tot=lt // t_)
    return pl.pallas_call(
        kern,
        grid=(b_,),
        in_specs=[blk(MQ), blk(MK), blk(MV), blk(MO), blk(MG),
                  pl.BlockSpec((1, GATE_ROWS, lt), lambda b: (b, 0, 0)),
                  full((GATE_ROWS, 1)), full((1, W_GROUP)),
                  full((W_GROUP, W_GROUP)), full((N_HEADS * t_, W_GROUP)), full((128, 3 * W_GROUP))],
        out_specs=pl.BlockSpec((1, lt, W_GROUP), lambda b: (b, 0, 0)),
        out_shape=jax.ShapeDtypeStruct((b_, lt, W_GROUP), BF16),
        scratch_shapes=[pltpu.VMEM((lt, W_GROUP), F32), pltpu.VMEM((lt, W_GROUP), F32),
                        pltpu.VMEM((2, W_GROUP, W_GROUP), F32), pltpu.VMEM((2, 1, W_GROUP), F32),
                        pltpu.VMEM((2, lt, 128), F32), pltpu.VMEM((2, 8, lt), F32),
                        pltpu.VMEM((2, lt // t_, 1, W_GROUP), F32)],
        compiler_params=pltpu.CompilerParams(vmem_limit_bytes=VMEM_LIMIT),
        name="mlstm_mixer",
    )(p, p, p, p, p, gt, brow, norm_g.reshape(1, W_GROUP),
      consts["bd"], consts["mones_m"], consts["eall_m"])


def _gla_kernel(q_ref, k_ref, v_ref, sg_ref, sm_ref, wa_ref, ba_ref, ng_ref, tril_ref, triu_ref, bd_ref,
                out_ref, hf_ref, hb_ref, st_ref, qd_ref, kd_ref, kc_ref, eb_ref, *, chunk, n_ctx, n_tot):
    t_ = chunk
    masks_bf = [m.astype(BF16) for m in _head_masks()]
    bdm = _block_diag_mask()
    row_i = lax.broadcasted_iota(jnp.int32, (t_, t_), 0)
    col_i = lax.broadcasted_iota(jnp.int32, (t_, t_), 1)
    st_ref[...] = jnp.zeros_like(st_ref)

    def decay(reverse, c_idx):
        d_ = 1 if reverse else 0
        tri_c = triu_ref[...] if reverse else tril_ref[...]
        end_row = 0 if reverse else t_ - 1
        r0 = pl.multiple_of(c_idx * t_, t_)
        sm = sm_ref[0, pl.ds(r0, t_), :]
        la = _log_sigmoid(_dot_f32(sm, wa_ref[d_]) + ba_ref[d_]) * (1.0 / GLA_NORMALIZER)
        yield
        bcum = _dot_exact_lhs(tri_c, la)
        yield
        q = q_ref[0, pl.ds(r0, t_), :] * (HEAD_DIM ** -0.5)
        k = k_ref[0, pl.ds(r0, t_), :]
        b_end = bcum[end_row:end_row + 1, :]
        qd_ref[d_, pl.ds(r0, t_), :] = (q * jnp.exp(bcum)).astype(BF16)
        kd_ref[d_, pl.ds(r0, t_), :] = (k * jnp.exp(-bcum)).astype(BF16)
        kc_ref[d_, pl.ds(r0, t_), :] = (k * jnp.exp(b_end - bcum)).astype(BF16)
        eb_ref[d_, c_idx] = jnp.exp(b_end)

    def decay_body(s, carry):
        _interleave(*[decay(rev, 2 * s + u) for u in range(2) for rev in (False, True)])
        return carry

    lax.fori_loop(0, n_tot // 2, decay_body, 0)

    def step(reverse, c_idx, h_ref):
        d_ = 1 if reverse else 0
        valid = (col_i >= row_i) if reverse else (col_i <= row_i)
        r0 = pl.multiple_of(c_idx * t_, t_)
        vb = v_ref[0, pl.ds(r0, t_), :].astype(BF16)
        qdb = qd_ref[d_, pl.ds(r0, t_), :]
        kd = kd_ref[d_, pl.ds(r0, t_), :]
        st_prev = st_ref[d_]
        att = [jnp.where(valid, _dot_nt(qdb * masks_bf[h], kd), 0.0).astype(BF16) for h in range(N_HEADS)]
        yield
        v_stack = jnp.concatenate([vb * masks_bf[h] for h in range(N_HEADS)], axis=0)
        h_ref[pl.ds(r0, t_), :] = (_dot_nt(qdb, st_prev.astype(BF16))
                                   + _dot(jnp.concatenate(att, axis=1), v_stack))
        st_ref[d_] = st_prev * eb_ref[d_, c_idx] + bdm * _dot_tn(vb, kc_ref[d_, pl.ds(r0, t_), :])

    def body(s, carry):
        _interleave(step(False, _scan_chunk(s, False, n_ctx, n_tot), hf_ref),
                    step(True, _scan_chunk(s, True, n_ctx, n_tot), hb_ref))
        return carry

    lax.fori_loop(0, n_tot, body, 0)

    def finish(i, carry):
        r0 = pl.multiple_of(i * ROW_TILE, ROW_TILE)
        hsum = hf_ref[pl.ds(r0, ROW_TILE), :] + hb_ref[pl.ds(r0, ROW_TILE), :]
        hn = _head_rms_norm(hsum, bd_ref[...], ng_ref[...])
        out_ref[0, pl.ds(r0, ROW_TILE), :] = (hn * _silu(sg_ref[0, pl.ds(r0, ROW_TILE), :])).astype(out_ref.dtype)
        return carry

    lax.fori_loop(0, n_tot * t_ // ROW_TILE, finish, 0)


def _gla(p, w_alpha, b_alpha, norm_g, consts, ctx_len):
    b_, lt, _ = p.shape
    t_ = GLA_CHUNK
    wa = jnp.zeros((2, 128, W_GROUP), F32)
    wa = wa.at[0, 16:32].set(w_alpha[0]).at[1, 32:48].set(w_alpha[1])
    blk = lambda j: pl.BlockSpec((1, lt, W_GROUP), lambda b: (b, 0, j))
    full = lambda shape: pl.BlockSpec(shape, lambda b: (0,) * len(shape))
    kern = functools.partial(_gla_kernel, chunk=t_, n_ctx=ctx_len // t_, n_tot=lt // t_)
    return pl.pallas_call(
        kern,
        grid=(b_,),
        in_specs=[blk(GQ), blk(GK), blk(GV), blk(GG),
                  pl.BlockSpec((1, lt, 128), lambda b: (b, 0, SMALL_BLOCK)),
                  full((2, 128, W_GROUP)), full((2, 1, W_GROUP)), full((1, W_GROUP)),
                  full((t_, t_)), full((t_, t_)), full((W_GROUP, W_GROUP))],
        out_specs=pl.BlockSpec((1, lt, W_GROUP), lambda b: (b, 0, 0)),
        out_shape=jax.ShapeDtypeStruct((b_, lt, W_GROUP), BF16),
        scratch_shapes=[pltpu.VMEM((lt, W_GROUP), F32), pltpu.VMEM((lt, W_GROUP), F32),
                        pltpu.VMEM((2, W_GROUP, W_GROUP), F32),
                        pltpu.VMEM((2, lt, W_GROUP), BF16), pltpu.VMEM((2, lt, W_GROUP), BF16),
                        pltpu.VMEM((2, lt, W_GROUP), BF16), pltpu.VMEM((2, lt // t_, 1, W_GROUP), F32)],
        compiler_params=pltpu.CompilerParams(vmem_limit_bytes=VMEM_LIMIT),
        name="gla_mixer",
    )(p, p, p, p, p, wa, b_alpha.reshape(2, 1, W_GROUP), norm_g.reshape(1, W_GROUP),
      consts["tril_g"], consts["triu_g"], consts["bd"])


def _attn_kernel(q_ref, kv_ref, sg_ref, cos_ref, sin_ref, sink_ref, pswap_ref, out_ref,
                 kb_ref, vb_ref, *, ctx_len, lat_len):
    g = pl.program_id(1)
    lt = ctx_len + lat_len
    n_ctx_blk = ctx_len // ATT_BLOCK
    nwin = 3 * ATT_BLOCK
    masks = _head_masks()
    log2e = math.log2(math.e)
    low_half = lax.broadcasted_iota(jnp.int32, (1, 2 * HEAD_DIM), 1) < HEAD_DIM

    def rope(x, r0, n):
        swapped = _dot(x.astype(BF16), pswap_ref[...])
        return x * cos_ref[pl.ds(r0, n), :] + swapped * sin_ref[pl.ds(r0, n), :]

    @pl.when(g == 0)
    def _():
        def fill(i, carry):
            r0 = pl.multiple_of(i * ROW_TILE, ROW_TILE)
            k = kv_ref[0, pl.ds(r0, ROW_TILE), 0:2 * HEAD_DIM]
            v = kv_ref[0, pl.ds(r0, ROW_TILE), 2 * HEAD_DIM:4 * HEAD_DIM]
            k_sw = pltpu.roll(k, HEAD_DIM, 1)
            v_sw = pltpu.roll(v, HEAD_DIM, 1)
            k4 = jnp.concatenate([jnp.where(low_half, k, k_sw), jnp.where(low_half, k_sw, k)], axis=1)
            kb_ref[pl.ds(r0, ROW_TILE), :] = rope(k4, r0, ROW_TILE).astype(BF16)
            vb_ref[pl.ds(r0, ROW_TILE), :] = jnp.concatenate(
                [jnp.where(low_half, v, 1.0), jnp.where(low_half, v_sw, 1.0)], axis=1).astype(BF16)
            return carry
        lax.fori_loop(0, lt // ROW_TILE, fill, 0)

    k_ctx = kb_ref[0:ctx_len, :]
    v_ctx = vb_ref[0:ctx_len, :]
    neg_inf = jnp.float32(-jnp.inf)
    rel0 = (lax.broadcasted_iota(jnp.int32, (ATT_BLOCK, nwin), 1)
            - lax.broadcasted_iota(jnp.int32, (ATT_BLOCK, nwin), 0))

    def block(u):
        j = g * ATT_STEP_BLOCKS + u
        rows = slice(u * ATT_BLOCK, (u + 1) * ATT_BLOCK)
        is_lat = j >= n_ctx_blk
        i_lat = jnp.maximum(j - n_ctx_blk, 0)
        w0 = pl.multiple_of(ctx_len + jnp.clip((i_lat - 1) * ATT_BLOCK, 0, lat_len - nwin), ATT_BLOCK)
        q0 = pl.multiple_of(j * ATT_BLOCK, ATT_BLOCK)
        q = rope(q_ref[0, rows, :], q0, ATT_BLOCK) * (HEAD_DIM ** -0.5 * log2e)
        k_win = kb_ref[pl.ds(w0, nwin), :]
        v_win = vb_ref[pl.ds(w0, nwin), :]
        band = is_lat & (jnp.abs(rel0 + (w0 - q0)) <= WINDOW)
        q_stack = jnp.concatenate([(q * masks[h]).astype(BF16) for h in range(N_HEADS)], axis=0)
        band4 = jnp.concatenate([band] * N_HEADS, axis=0)
        sink = jnp.concatenate([jnp.broadcast_to(sink_ref[h] * log2e, (ATT_BLOCK, 128)) for h in range(N_HEADS)],
                               axis=0)
        s_loc = jnp.where(band4, _dot_nt(q_stack, k_win), neg_inf)
        s_ctx = _dot_nt(q_stack, k_ctx)
        yield
        m = jnp.maximum(jnp.max(s_loc, axis=-1, keepdims=True), jnp.max(s_ctx, axis=-1, keepdims=True))
        m128 = jnp.maximum(jnp.broadcast_to(m, (N_HEADS * ATT_BLOCK, 128)), sink)
        p_loc = jnp.exp2(s_loc - jnp.concatenate([m128] * (nwin // 128), axis=1)).astype(BF16)
        p_ctx = jnp.exp2(s_ctx - jnp.concatenate([m128] * (ctx_len // 128), axis=1)).astype(BF16)
        yield
        o_all = _dot(p_loc, v_win) + _dot(p_ctx, v_ctx)
        e_sink = jnp.exp2(sink - m128)
        o = jnp.zeros((ATT_BLOCK, W_GROUP), F32)
        for h in range(N_HEADS):
            hr = slice(h * ATT_BLOCK, (h + 1) * ATT_BLOCK)
            if h % 2 == 0:
                num, den = o_all[hr], pltpu.roll(o_all[hr], W_GROUP - HEAD_DIM, 1)
            else:
                num, den = pltpu.roll(o_all[hr], HEAD_DIM, 1), o_all[hr]
            den = den + jnp.concatenate([e_sink[hr], e_sink[hr]], axis=1)
            o = jnp.where(masks[h] > 0.0, num / den, o)
        out_ref[0, rows, :] = (o * _silu(sg_ref[0, rows, :])).astype(out_ref.dtype)

    _interleave(*[block(u) for u in range(ATT_STEP_BLOCKS)])


def _attn(p, sink, consts, ctx_len):
    b_, lt, _ = p.shape
    step_rows = ATT_STEP_BLOCKS * ATT_BLOCK
    nb = lt // step_rows
    kern = functools.partial(_attn_kernel, ctx_len=ctx_len, lat_len=lt - ctx_len)
    return pl.pallas_call(
        kern,
        grid=(b_, nb),
        in_specs=[pl.BlockSpec((1, step_rows, W_GROUP), lambda b, j: (b, j, AQ)),
                  pl.BlockSpec((1, lt, W_GROUP), lambda b, j: (b, 0, AKV)),
                  pl.BlockSpec((1, step_rows, W_GROUP), lambda b, j: (b, j, AG)),
                  pl.BlockSpec((lt, W_GROUP), lambda b, j: (0, 0)),
                  pl.BlockSpec((lt, W_GROUP), lambda b, j: (0, 0)),
                  pl.BlockSpec((N_HEADS, 1, 1), lambda b, j: (0, 0, 0)),
                  pl.BlockSpec((W_GROUP, W_GROUP), lambda b, j: (0, 0))],
        out_specs=pl.BlockSpec((1, step_rows, W_GROUP), lambda b, j: (b, j, 0)),
        out_shape=jax.ShapeDtypeStruct((b_, lt, W_GROUP), BF16),
        scratch_shapes=[pltpu.VMEM((lt, W_GROUP), BF16), pltpu.VMEM((lt, W_GROUP), BF16)],
        compiler_params=pltpu.CompilerParams(vmem_limit_bytes=VMEM_LIMIT,
                                             dimension_semantics=("arbitrary", "arbitrary")),
        name="window_attention",
    )(p, p, p, consts["rope_cos"], consts["rope_sin"], sink.reshape(N_HEADS, 1, 1), consts["pswap"])


class _FftPlan:
    def __init__(self, n1):
        self.n1 = n1
        self.n = n1 * FFT_N2
        self.k1p = -(-(n1 // 2 + 1) // 8) * 8


def _fft_constants(n1, n_in, n_out):
    plan = _FftPlan(n1)
    n, k1p, half = plan.n, plan.k1p, n1 // 2
    k1 = np.arange(k1p)[:, None].astype(np.float64)
    live = (np.arange(k1p) <= half)[:, None]
    i1 = np.arange(n1)[None, :].astype(np.float64)
    ang = 2.0 * np.pi * k1 * i1 / n1
    f1 = np.concatenate([np.where(live, np.cos(ang), 0.0), np.where(live, -np.sin(ang), 0.0)], axis=0)
    i2 = np.arange(FFT_N2)[None, :].astype(np.float64)
    phi = 2.0 * np.pi * k1 * i2 / n
    tw = np.stack([np.cos(phi), -np.sin(phi)], axis=0)
    tw = np.broadcast_to(tw.reshape(2, k1p * FFT_N2, 1), (2, k1p * FFT_N2, 128))
    th = 2.0 * np.pi * np.outer(np.arange(FFT_N2), np.arange(FFT_N2)) / FFT_N2
    c2, s2 = np.cos(th), np.sin(th)
    m_fwd = np.block([[c2, s2], [-s2, c2]])
    m_inv = np.block([[c2, -s2], [s2, c2]])
    wk = np.where(np.arange(k1p) <= half, 2.0, 0.0)
    wk[0] = 1.0
    wk[half] = 1.0
    psi = 2.0 * np.pi * np.outer(np.arange(n1), np.arange(k1p)) / n1
    g = np.concatenate([np.cos(psi) * wk[None, :], -np.sin(psi) * wk[None, :]], axis=1) / n

    def hl(a):
        a32 = jnp.asarray(a, F32)
        hi = a32.astype(BF16)
        lo = (a32 - hi.astype(F32)).astype(BF16)
        return hi, lo

    return dict(f1=hl(f1), f1_in=hl(f1[:, 0:n_in]), g=hl(g[0:n_out]), m_fwd=hl(m_fwd), m_inv=hl(m_inv),
                tw=jnp.asarray(tw, F32))


def _slab_row(k):
    return k * FFT_PITCH if isinstance(k, int) else pl.multiple_of(k * FFT_PITCH, 8)


def _load_slab(ref, k):
    r0 = _slab_row(k)
    return jnp.concatenate([ref[0, pl.ds(r0, FFT_N2), :], ref[1, pl.ds(r0, FFT_N2), :]], axis=1)


def _store_slab(ref, k, val):
    r0 = _slab_row(k)
    ref[0, pl.ds(r0, FFT_N2), :] = val[:, 0:128]
    ref[1, pl.ds(r0, FFT_N2), :] = val[:, 128:256]


FFT_GROUP = 8


def _cross_slab(src_refs, n_src, mat_h, mat_l, dst_refs, n_dst, dot=_dot_const):
    def body(gidx, carry):
        cols = []
        for i in range(FFT_GROUP):
            i2 = gidx * FFT_GROUP + i
            parts = []
            for half in range(2):
                parts.append(jnp.concatenate(
                    [r[half, pl.ds(i2, n_src, stride=FFT_PITCH), :] for r in src_refs], axis=0))
            cols.append(jnp.concatenate(parts, axis=1))
        x = jnp.concatenate(cols, axis=1)
        y = dot(mat_h, mat_l, x)
        for i in range(FFT_GROUP):
            i2 = gidx * FFT_GROUP + i
            for half in range(2):
                c0 = (2 * i + half) * 128
                for d, ref in enumerate(dst_refs):
                    ref[half, pl.ds(i2, n_dst, stride=FFT_PITCH), :] = y[d * n_dst:(d + 1) * n_dst, c0:c0 + 128]
        return carry

    lax.fori_loop(0, FFT_N2 // FFT_GROUP, body, 0)


def _twiddle(ar, ai, tw_ref, k, conj):
    r0 = pl.multiple_of(k * FFT_N2, FFT_N2)
    tr = tw_ref[0, pl.ds(r0, FFT_N2), :]
    ti = tw_ref[1, pl.ds(r0, FFT_N2), :]
    tr = jnp.concatenate([tr, tr], axis=1)
    ti = jnp.concatenate([ti, ti], axis=1)
    if conj:
        return ar * tr + ai * ti, ai * tr - ar * ti
    return ar * tr - ai * ti, ar * ti + ai * tr


def _filter_kernel(feat_ref, dec_ref, w1_ref, b1_ref, w2_ref, b2_ref, w3_ref, b3_ref, fq_ref,
                   f1h_ref, f1l_ref, mh_ref, ml_ref, tw_ref, out_ref, zbuf, are, aim, *, n1, k1p):
    half = n1 // 2

    u_ = FFT_SLAB_UNROLL
    rows = u_ * FFT_N2

    def fill(g, carry):
        r0 = pl.multiple_of(g * rows, rows)
        z = feat_ref[pl.ds(r0, rows), :]
        h = jnp.sin(fq_ref[0, 0:1, :] * (_dot_f32(z, w1_ref[0]) + b1_ref[0]))
        h = jnp.sin(fq_ref[0, 1:2, :] * (_dot_f32(h, w2_ref[0]) + b2_ref[0]))
        h = _dot_f32(h, w3_ref[0]) + b3_ref[0]
        hsel = jnp.where(g * u_ < half, h[:, 0:W_GROUP], h[:, W_GROUP:2 * W_GROUP])
        kern = hsel * dec_ref[pl.ds(r0, rows), :]
        for u in range(u_):
            _store_slab(zbuf, g * u_ + u, kern[u * FFT_N2:(u + 1) * FFT_N2])
        return carry

    lax.fori_loop(0, n1 // u_, fill, 0)
    _cross_slab([zbuf], n1, f1h_ref[...], f1l_ref[...], [are, aim], k1p)

    def slab(g, carry):
        cols = []
        for u in range(u_):
            k = g * u_ + u
            ar, ai = _twiddle(_load_slab(are, k), _load_slab(aim, k), tw_ref, k, conj=False)
            cols.append(jnp.concatenate([ar, ai], axis=0))
        x = _dot_const(mh_ref[...], ml_ref[...], jnp.concatenate(cols, axis=1))
        for u in range(u_):
            r0 = pl.multiple_of((g * u_ + u) * FFT_N2, FFT_N2)
            out_ref[0, 0, pl.ds(r0, FFT_N2), :] = x[0:FFT_N2, u * W_GROUP:(u + 1) * W_GROUP]
            out_ref[0, 1, pl.ds(r0, FFT_N2), :] = x[FFT_N2:2 * FFT_N2, u * W_GROUP:(u + 1) * W_GROUP]
        return carry

    lax.fori_loop(0, k1p // u_, slab, 0)


def _filter_tables(l_):
    pos = np.concatenate([np.arange(l_), l_ - np.arange(l_)]).astype(np.float32)
    pos[l_] = 0.0
    t = pos / np.float32(max(l_ - 1, 1))
    w = np.float32(2.0 * math.pi) * pos / np.float32(l_)
    f = np.linspace(1e-4, HYENA_BANDS - 1, HYENA_BANDS, dtype=np.float32)
    feat = np.zeros((2 * l_, 64), np.float32)
    feat[:, 0] = t
    feat[:, 1:1 + HYENA_BANDS] = np.cos(w[:, None] * f)
    feat[:, 1 + HYENA_BANDS:HYENA_EMB] = -np.sin(w[:, None] * f)
    deltas = np.abs(np.linspace(HYENA_MIN_DECAY, HYENA_MAX_DECAY, W_GROUP, dtype=np.float32))
    dec = np.exp(-t[:, None] * deltas[None, :]).astype(np.float32)
    dec[l_] = 0.0
    return jnp.asarray(feat), jnp.asarray(dec)


def _filter_spectra(l_, w1p, b1, w2, b2, w3, b3, freq, fc):
    n1 = 2 * l_ // FFT_N2
    plan = _FftPlan(n1)
    k1p = plan.k1p
    feat, dec = _filter_tables(l_)
    full = lambda shape: pl.BlockSpec(shape, lambda l: (0,) * len(shape))
    per = lambda shape: pl.BlockSpec((1,) + shape, lambda l: (l,) + (0,) * len(shape))
    kern = functools.partial(_filter_kernel, n1=n1, k1p=k1p)
    return pl.pallas_call(
        kern,
        grid=(DEPTH,),
        in_specs=[full((2 * l_, 64)), full((2 * l_, W_GROUP)),
                  per((64, HYENA_FFN)), per((1, HYENA_FFN)), per((HYENA_FFN, HYENA_FFN)), per((1, HYENA_FFN)),
                  per((HYENA_FFN, 2 * W_GROUP)), per((1, 2 * W_GROUP)), per((2, HYENA_FFN)),
                  full((2 * k1p, n1)), full((2 * k1p, n1)), full((128, 128)), full((128, 128)),
                  full((2, k1p * FFT_N2, 128))],
        out_specs=pl.BlockSpec((1, 2, k1p * FFT_N2, W_GROUP), lambda l: (l, 0, 0, 0)),
        out_shape=jax.ShapeDtypeStruct((DEPTH, 2, k1p * FFT_N2, W_GROUP), F32),
        scratch_shapes=[pltpu.VMEM((2, n1 * FFT_PITCH, 128), F32),
                        pltpu.VMEM((2, k1p * FFT_PITCH, 128), F32),
                        pltpu.VMEM((2, k1p * FFT_PITCH, 128), F32)],
        compiler_params=pltpu.CompilerParams(vmem_limit_bytes=VMEM_LIMIT),
        name="hyena_filter_spectrum_%d" % l_,
    )(feat, dec, w1p, b1.reshape(DEPTH, 1, HYENA_FFN), w2, b2.reshape(DEPTH, 1, HYENA_FFN),
      w3, b3.reshape(DEPTH, 1, 2 * W_GROUP), freq,
      fc["f1"][0], fc["f1"][1], fc["m_fwd"][0], fc["m_fwd"][1], fc["tw"])


def _short_dft_constants(l_):
    n = 2 * l_
    kp = -(-(l_ + 1) // 8) * 8
    k = np.arange(kp)[:, None].astype(np.float64)
    live = (np.arange(kp) <= l_)[:, None]
    pos = np.arange(n)[None, :].astype(np.float64)
    ang = 2.0 * np.pi * k * pos / n
    flt = np.concatenate([np.where(live, np.cos(ang), 0.0), np.where(live, -np.sin(ang), 0.0)], axis=0)
    wk = np.where(np.arange(kp) <= l_, 2.0, 0.0)
    wk[0] = 1.0
    wk[l_] = 1.0
    psi = 2.0 * np.pi * np.outer(np.arange(l_), np.arange(kp)) / n
    inv = np.concatenate([np.cos(psi) * wk[None, :], -np.sin(psi) * wk[None, :]], axis=1) / n

    def hl(a):
        a32 = jnp.asarray(a, F32)
        hi = a32.astype(BF16)
        return jnp.stack([hi, (a32 - hi.astype(F32)).astype(BF16)], axis=0)

    return dict(fwd=hl(flt[:, 0:l_]), flt=hl(flt), inv=hl(inv), kp=kp)


def _short_filter_kernel(feat_ref, dec_ref, w1_ref, b1_ref, w2_ref, b2_ref, w3_ref, b3_ref, fq_ref, flt_ref,
                         out_ref, *, l_):
    kp = out_ref.shape[2]
    halves = []
    for part in range(2):
        z = feat_ref[part * l_:(part + 1) * l_, :]
        h = jnp.sin(fq_ref[0, 0:1, :] * (_dot_f32(z, w1_ref[0]) + b1_ref[0]))
        h = jnp.sin(fq_ref[0, 1:2, :] * (_dot_f32(h, w2_ref[0]) + b2_ref[0]))
        h = _dot_f32(h, w3_ref[0]) + b3_ref[0]
        halves.append(h[:, part * W_GROUP:(part + 1) * W_GROUP] * dec_ref[part * l_:(part + 1) * l_, :])
    spec = _dot_const(flt_ref[0], flt_ref[1], jnp.concatenate(halves, axis=0))
    out_ref[0, 0] = spec[0:kp]
    out_ref[0, 1] = spec[kp:2 * kp]


def _short_filter_spectra(l_, w1p, b1, w2, b2, w3, b3, freq, sc):
    kp = sc["kp"]
    feat, dec = _filter_tables(l_)
    full = lambda shape: pl.BlockSpec(shape, lambda l: (0,) * len(shape))
    per = lambda shape: pl.BlockSpec((1,) + shape, lambda l: (l,) + (0,) * len(shape))
    return pl.pallas_call(
        functools.partial(_short_filter_kernel, l_=l_),
        grid=(DEPTH,),
        in_specs=[full((2 * l_, 64)), full((2 * l_, W_GROUP)),
                  per((64, HYENA_FFN)), per((1, HYENA_FFN)), per((HYENA_FFN, HYENA_FFN)), per((1, HYENA_FFN)),
                  per((HYENA_FFN, 2 * W_GROUP)), per((1, 2 * W_GROUP)), per((2, HYENA_FFN)),
                  full((2, 2 * kp, 2 * l_))],
        out_specs=pl.BlockSpec((1, 2, kp, W_GROUP), lambda l: (l, 0, 0, 0)),
        out_shape=jax.ShapeDtypeStruct((DEPTH, 2, kp, W_GROUP), F32),
        compiler_params=pltpu.CompilerParams(vmem_limit_bytes=VMEM_LIMIT),
        name="hyena_filter_spectrum_%d" % l_,
    )(feat, dec, w1p, b1.reshape(DEPTH, 1, HYENA_FFN), w2, b2.reshape(DEPTH, 1, HYENA_FFN),
      w3, b3.reshape(DEPTH, 1, 2 * W_GROUP), freq, sc["flt"])


def _hyena_segment(v_ref, x1_ref, x0_ref, sg_ref, cw_ref, cb_ref, d_ref, kf_ref, consts, scratch, out_ref,
                   *, row0, seg_len, n1):
    f1h, f1l, gh, gl, mfh, mfl, mih, mil, tw_ref = consts
    zbuf, are, aim, ybuf = scratch
    n_sig = seg_len // FFT_N2
    n_in = max(n_sig, 8)
    n_out = max(n_sig, 8)
    k1p = _FftPlan(n1).k1p
    row_i = lax.broadcasted_iota(jnp.int32, (FFT_N2, W_GROUP), 0)

    def conv3(ref, blk, jslab):
        r = pl.multiple_of(row0 + jslab * FFT_N2, FFT_N2)
        cur = ref[0, pl.ds(r, FFT_N2), :]
        pe = ref[0, pl.ds(jnp.maximum(r - 1, row0), 1), :] * jnp.where(jslab > 0, 1.0, 0.0)
        ne = ref[0, pl.ds(jnp.minimum(r + FFT_N2, row0 + seg_len - 1), 1), :] * jnp.where(jslab < n_sig - 1, 1.0, 0.0)
        prev = jnp.where(row_i == 0, pe, pltpu.roll(cur, 1, 0))
        nxt = jnp.where(row_i == FFT_N2 - 1, ne, pltpu.roll(cur, FFT_N2 - 1, 0))
        c0 = blk * W_GROUP
        return (cw_ref[0:1, c0:c0 + W_GROUP] * prev + cw_ref[1:2, c0:c0 + W_GROUP] * cur
                + cw_ref[2:3, c0:c0 + W_GROUP] * nxt + cb_ref[:, c0:c0 + W_GROUP])

    def pre(jslab, carry):
        _store_slab(zbuf, jslab, conv3(v_ref, 0, jslab) * conv3(x1_ref, 1, jslab))
        return carry

    lax.fori_loop(0, n_sig, pre, 0)
    for jz in range(n_sig, n_in):
        _store_slab(zbuf, jz, jnp.zeros((FFT_N2, W_GROUP), F32))

    _cross_slab([zbuf], n_in, f1h, f1l, [are, aim], k1p, dot=_dot_const2)

    u_ = FFT_SLAB_UNROLL

    def slab(g, carry):
        cols = []
        for u in range(u_):
            k = g * u_ + u
            ar, ai = _twiddle(_load_slab(are, k), _load_slab(aim, k), tw_ref, k, conj=False)
            cols.append(jnp.concatenate([ar, ai], axis=0))
        x = _dot_const2(mfh, mfl, jnp.concatenate(cols, axis=1))
        cols = []
        for u in range(u_):
            xr = x[0:FFT_N2, u * W_GROUP:(u + 1) * W_GROUP]
            xi = x[FFT_N2:2 * FFT_N2, u * W_GROUP:(u + 1) * W_GROUP]
            r0 = pl.multiple_of((g * u_ + u) * FFT_N2, FFT_N2)
            kr = kf_ref[0, 0, pl.ds(r0, FFT_N2), :]
            ki = kf_ref[0, 1, pl.ds(r0, FFT_N2), :]
            cols.append(jnp.concatenate([xr * kr - xi * ki, xr * ki + xi * kr], axis=0))
        bm = _dot_const2(mih, mil, jnp.concatenate(cols, axis=1))
        for u in range(u_):
            k = g * u_ + u
            br, bi = _twiddle(bm[0:FFT_N2, u * W_GROUP:(u + 1) * W_GROUP],
                              bm[FFT_N2:2 * FFT_N2, u * W_GROUP:(u + 1) * W_GROUP], tw_ref, k, conj=True)
            _store_slab(are, k, br)
            _store_slab(aim, k, bi)
        return carry

    lax.fori_loop(0, k1p // u_, slab, 0)
    _cross_slab([are, aim], k1p, gh, gl, [ybuf], n_out, dot=_dot_const2)

    def post(jslab, carry):
        r = pl.multiple_of(row0 + jslab * FFT_N2, FFT_N2)
        y = _load_slab(ybuf, jslab) + _load_slab(zbuf, jslab) * d_ref[...]
        out_ref[0, pl.ds(r, FFT_N2), :] = (conv3(x0_ref, 2, jslab) * y
                                           * _silu(sg_ref[0, pl.ds(r, FFT_N2), :])).astype(out_ref.dtype)
        return carry

    lax.fori_loop(0, n_sig, post, 0)


def _hyena_short_segment(v_ref, x1_ref, x0_ref, sg_ref, cw_ref, cb_ref, d_ref, kf_ref, fwd_ref, inv_ref,
                         zbuf, ybuf, out_ref, *, seg_len):
    n_sig = seg_len // FFT_N2
    kp = kf_ref.shape[2]
    row_i = lax.broadcasted_iota(jnp.int32, (FFT_N2, W_GROUP), 0)

    def conv3(ref, blk, jslab):
        r = pl.multiple_of(jslab * FFT_N2, FFT_N2)
        cur = ref[0, pl.ds(r, FFT_N2), :]
        pe = ref[0, pl.ds(jnp.maximum(r - 1, 0), 1), :] * jnp.where(jslab > 0, 1.0, 0.0)
        ne = ref[0, pl.ds(jnp.minimum(r + FFT_N2, seg_len - 1), 1), :] * jnp.where(jslab < n_sig - 1, 1.0, 0.0)
        prev = jnp.where(row_i == 0, pe, pltpu.roll(cur, 1, 0))
        nxt = jnp.where(row_i == FFT_N2 - 1, ne, pltpu.roll(cur, FFT_N2 - 1, 0))
        c0 = blk * W_GROUP
        return (cw_ref[0:1, c0:c0 + W_GROUP] * prev + cw_ref[1:2, c0:c0 + W_GROUP] * cur
                + cw_ref[2:3, c0:c0 + W_GROUP] * nxt + cb_ref[:, c0:c0 + W_GROUP])

    def pre(jslab, carry):
        r = pl.multiple_of(jslab * FFT_N2, FFT_N2)
        zbuf[pl.ds(r, FFT_N2), :] = conv3(v_ref, 0, jslab) * conv3(x1_ref, 1, jslab)
        return carry

    lax.fori_loop(0, n_sig, pre, 0)
    x = _dot_const2(fwd_ref[0], fwd_ref[1], zbuf[...])
    xr, xi = x[0:kp], x[kp:2 * kp]
    kr, ki = kf_ref[0, 0], kf_ref[0, 1]
    y = jnp.concatenate([xr * kr - xi * ki, xr * ki + xi * kr], axis=0)
    ybuf[...] = _dot_const2(inv_ref[0], inv_ref[1], y)

    def post(jslab, carry):
        r = pl.multiple_of(jslab * FFT_N2, FFT_N2)
        y_ = ybuf[pl.ds(r, FFT_N2), :] + zbuf[pl.ds(r, FFT_N2), :] * d_ref[...]
        out_ref[0, pl.ds(r, FFT_N2), :] = (conv3(x0_ref, 2, jslab) * y_
                                           * _silu(sg_ref[0, pl.ds(r, FFT_N2), :])).astype(out_ref.dtype)
        return carry

    lax.fori_loop(0, n_sig, post, 0)


def _hyena_kernel(v_ref, x1_ref, x0_ref, sg_ref, cw_ref, cb_ref, d_ref, kfc_ref, kfl_ref,
                  c_fwd, c_inv, l_f1h, l_f1l, l_gh, l_gl, l_tw,
                  mfh_ref, mfl_ref, mih_ref, mil_ref, out_ref,
                  zc, yc, zl, arl, ail, yl, *, ctx_len, lat_len):
    mats = (mfh_ref[...], mfl_ref[...], mih_ref[...], mil_ref[...])
    _hyena_short_segment(v_ref, x1_ref, x0_ref, sg_ref, cw_ref, cb_ref, d_ref, kfc_ref, c_fwd, c_inv,
                         zc, yc, out_ref, seg_len=ctx_len)
    _hyena_segment(v_ref, x1_ref, x0_ref, sg_ref, cw_ref, cb_ref, d_ref,
                   kfl_ref, (l_f1h[...], l_f1l[...], l_gh[...], l_gl[...]) + mats + (l_tw,), (zl, arl, ail, yl), out_ref,
                   row0=ctx_len, seg_len=lat_len, n1=2 * lat_len // FFT_N2)


def _hyena(p, conv_w, conv_b, d, kf_ctx, kf_lat, layer, fc_ctx, fc_lat, ctx_len):
    b_, lt, _ = p.shape
    lat_len = lt - ctx_len
    n1l = 2 * lat_len // FFT_N2
    kl = _FftPlan(n1l).k1p
    blk = lambda j: pl.BlockSpec((1, lt, W_GROUP), lambda b: (b, 0, j))
    full = lambda a: pl.BlockSpec(a.shape, lambda b: (0,) * a.ndim)
    per_layer = lambda a: pl.BlockSpec((1,) + a.shape[1:], lambda b: (layer,) + (0,) * (a.ndim - 1))
    cst = [fc_ctx["fwd"], fc_ctx["inv"],
           fc_lat["f1_in"][0], fc_lat["f1_in"][1], fc_lat["g"][0], fc_lat["g"][1], fc_lat["tw"],
           fc_lat["m_fwd"][0], fc_lat["m_fwd"][1], fc_lat["m_inv"][0], fc_lat["m_inv"][1]]
    small = [conv_w, conv_b.reshape(1, 3 * W_GROUP), d.reshape(1, W_GROUP)]
    sbuf = lambda n: pltpu.VMEM((2, n * FFT_PITCH, 128), F32)
    kern = functools.partial(_hyena_kernel, ctx_len=ctx_len, lat_len=lat_len)
    return pl.pallas_call(
        kern,
        grid=(b_,),
        in_specs=([blk(HV), blk(HX1), blk(HX0), blk(HG)] + [full(a) for a in small]
                  + [per_layer(kf_ctx), per_layer(kf_lat)] + [full(a) for a in cst]),
        out_specs=pl.BlockSpec((1, lt, W_GROUP), lambda b: (b, 0, 0)),
        out_shape=jax.ShapeDtypeStruct((b_, lt, W_GROUP), BF16),
        scratch_shapes=[pltpu.VMEM((ctx_len, W_GROUP), F32), pltpu.VMEM((ctx_len, W_GROUP), F32),
                        sbuf(n1l // 2), sbuf(kl), sbuf(kl), sbuf(n1l // 2)],
        compiler_params=pltpu.CompilerParams(vmem_limit_bytes=VMEM_LIMIT),
        name="hyena_mixer",
    )(p, p, p, p, *small, kf_ctx, kf_lat, *cst)


def _rope_tables(ctx_len, lat_len):
    pos = np.arange(lat_len)
    row = (pos // GRID_W).astype(np.float32)
    col = (pos % GRID_W).astype(np.float32)
    inv = (np.float32(ROPE_BASE) ** (-np.arange(ROPE_AXIS_FREQS, dtype=np.float32) / np.float32(ROPE_AXIS_FREQS)))
    ang = np.concatenate([row[:, None] * inv, col[:, None] * inv], axis=-1).astype(np.float32)
    cos, sin = np.cos(ang), np.sin(ang)
    cos_h = np.concatenate([cos, cos], axis=-1)
    sin_h = np.concatenate([-sin, sin], axis=-1)
    cos_l = np.tile(cos_h, (1, N_HEADS))
    sin_l = np.tile(sin_h, (1, N_HEADS))
    cos_t = np.concatenate([np.ones((ctx_len, W_GROUP), np.float32), cos_l], axis=0)
    sin_t = np.concatenate([np.zeros((ctx_len, W_GROUP), np.float32), sin_l], axis=0)
    return jnp.asarray(cos_t, F32), jnp.asarray(sin_t, F32)


def _rope_partner():
    d = np.arange(W_GROUP)
    partner = np.where(d % HEAD_DIM < HEAD_DIM // 2, d + HEAD_DIM // 2, d - HEAD_DIM // 2)
    p = np.zeros((W_GROUP, W_GROUP), np.float32)
    p[partner, d] = 1.0
    return jnp.asarray(p, BF16)


def _mlstm_expansion():
    e = np.zeros((128, 3 * W_GROUP), np.float32)
    for j in range(3):
        for h in range(N_HEADS):
            e[8 * (j + 1) + h, j * W_GROUP + h * HEAD_DIM:j * W_GROUP + (h + 1) * HEAD_DIM] = 1.0
    return jnp.asarray(e, BF16)


def _tri(n):
    t = np.tril(np.ones((n, n), np.float32))
    return jnp.asarray(t, BF16), jnp.asarray(t.T, BF16)


def _w_layout_kernel(w_ref, o_ref, ot_ref):
    h0 = 5 * W_GROUP + 16
    g0 = h0 + 4 * W_GROUP
    d0 = g0 + 4 * W_GROUP + 2 * GLA_RANK
    w = w_ref[0]
    gates_t = w[:, 5 * W_GROUP:5 * W_GROUP + 128].T
    zero4 = jnp.zeros((8 - N_HEADS, w.shape[0]), F32)
    ot_ref[0] = jnp.concatenate(
        [piece for g in range(4) for piece in (gates_t[N_HEADS * g:N_HEADS * (g + 1)], zero4)], axis=0).astype(BF16)
    o_ref[0] = jnp.concatenate([
        w[:, 0:5 * W_GROUP],
        w[:, h0:h0 + 4 * W_GROUP],
        w[:, g0:g0 + 4 * W_GROUP],
        w[:, d0:d0 + 3 * W_GROUP],
        w[:, 5 * W_GROUP:5 * W_GROUP + 16],
        w[:, g0 + 4 * W_GROUP:g0 + 4 * W_GROUP + 2 * GLA_RANK],
        jnp.zeros((w.shape[0], 128 - 48), F32)], axis=1).astype(BF16)


def _layout_w_in(w_in):
    n_in = w_in.shape[-1]
    return pl.pallas_call(
        _w_layout_kernel,
        grid=(DEPTH, D_MODEL // ROW_TILE),
        in_specs=[pl.BlockSpec((1, ROW_TILE, n_in), lambda l, i: (l, i, 0))],
        out_specs=[pl.BlockSpec((1, ROW_TILE, N_PROJ), lambda l, i: (l, i, 0)),
                   pl.BlockSpec((1, GATE_ROWS, ROW_TILE), lambda l, i: (l, 0, i))],
        out_shape=[jax.ShapeDtypeStruct((DEPTH, D_MODEL, N_PROJ), BF16),
                   jax.ShapeDtypeStruct((DEPTH, GATE_ROWS, D_MODEL), BF16)],
        compiler_params=pltpu.CompilerParams(vmem_limit_bytes=VMEM_LIMIT),
        name="inproj_weight_layout",
    )(w_in)


def kernel(x, c, ctx, c_ctx, w_ada, b_ada, g_pre, g_post, w_in, mlstm_gate_b, mlstm_norm_g, hyena_conv_w,
           hyena_conv_b, hyena_w1, hyena_b1, hyena_w2, hyena_b2, hyena_w3, hyena_b3, hyena_freq, hyena_d,
           gla_w_alpha, gla_b_alpha, gla_norm_g, attn_sink, w_out):
    b_, lat_len, _ = x.shape
    ctx_len = ctx.shape[1]
    n_ctx_tiles = ctx_len // ROW_TILE

    tril_m, triu_m = _tri(MLSTM_CHUNK)
    tril_g, triu_g = _tri(GLA_CHUNK)
    bd = np.kron(np.eye(N_HEADS, dtype=np.float32), np.ones((HEAD_DIM, HEAD_DIM), np.float32))
    rope_cos, rope_sin = _rope_tables(ctx_len, lat_len)
    mones_m = np.kron(np.eye(N_HEADS, dtype=np.float32), np.ones((MLSTM_CHUNK, HEAD_DIM), np.float32))
    consts = dict(tril_m=tril_m, triu_m=triu_m, tril_g=tril_g, triu_g=triu_g, bd=jnp.asarray(bd, BF16),
                  mones_m=jnp.asarray(mones_m, BF16), eall_m=_mlstm_expansion(),
                  rope_cos=rope_cos, rope_sin=rope_sin, pswap=_rope_partner())
    fc_ctx = _short_dft_constants(ctx_len)
    fc_lat = _fft_constants(2 * lat_len // FFT_N2, lat_len // FFT_N2, lat_len // FFT_N2)

    cc = jnp.zeros((16, D_MODEL), F32).at[0:b_].set(c).at[b_].set(c_ctx)
    mod = _modulation(cc, w_ada, b_ada)
    mod_ctx = jnp.broadcast_to(mod[:, b_:b_ + 1], (DEPTH, b_, 3 * D_MODEL))
    mod_all = jnp.stack([mod_ctx, mod[:, 0:b_]], axis=2).reshape(DEPTH, 2 * b_, 1, 3 * D_MODEL)

    w_p, w_st = _layout_w_in(w_in)
    w_o = w_out.astype(BF16)
    w1p = jnp.zeros((DEPTH, 64, HYENA_FFN), F32).at[:, 0:HYENA_EMB].set(hyena_w1)
    filt = (w1p, hyena_b1, hyena_w2, hyena_b2, hyena_w3, hyena_b3, hyena_freq)
    kf_ctx = _short_filter_spectra(ctx_len, *filt, fc_ctx)
    kf_lat = _filter_spectra(lat_len, *filt, fc_lat)

    xs = (ctx, x)
    p, gt = _inproj(xs, mod_all, 0, g_pre[0], w_p, w_st, n_ctx_tiles)
    for l in range(DEPTH):
        a = _mlstm(p, gt, mlstm_gate_b[l], mlstm_norm_g[l], consts, ctx_len)
        hy = _hyena(p, hyena_conv_w[l], hyena_conv_b[l], hyena_d[l], kf_ctx, kf_lat, l, fc_ctx, fc_lat, ctx_len)
        g = _gla(p, gla_w_alpha[l], gla_b_alpha[l], gla_norm_g[l], consts, ctx_len)
        d = _attn(p, attn_sink[l], consts, ctx_len)
        if l < DEPTH - 1:
            xs, p, gt = _outin((a, hy, g, d), w_o, l, g_post[l], g_pre[l + 1], mod_all, xs, w_p, w_st, n_ctx_tiles)
    return _outproj((a, hy, g, d), w_o, DEPTH - 1, g_post[DEPTH - 1], mod_all, xs, n_ctx_tiles, skip_ctx=True)
```

```python
import functools
import math

import numpy as np
import jax
import jax.numpy as jnp
from jax import lax
from jax.experimental import pallas as pl
from jax.experimental.pallas import tpu as pltpu

F32 = jnp.float32
BF16 = jnp.bfloat16

D_MODEL = 1024
DEPTH = 4
GRID_W = 64
W_GROUP = 256
HEAD_DIM = 64
N_HEADS = 4
ATT_KV_HEADS = 2
WINDOW = 128
ATT_BLOCK = 128
GLA_RANK = 16
GLA_NORMALIZER = 16.0
HYENA_BANDS = 16
HYENA_EMB = 1 + 2 * HYENA_BANDS
HYENA_FFN = 64
HYENA_MIN_DECAY = math.log(1e-2) / 1.5
HYENA_MAX_DECAY = math.log(1e-2) / 0.3
ROPE_BASE = 10000.0
ROPE_AXIS_FREQS = HEAD_DIM // 4
EPS = 1e-6

N_WIDE_BLOCKS = 16
SMALL_COL0 = N_WIDE_BLOCKS * W_GROUP
N_PROJ = SMALL_COL0 + 128
SMALL_BLOCK = SMALL_COL0 // 128
(MQ, MK, MV, MO, MG, HV, HX1, HX0, HG, GQ, GK, GV, GG, AQ, AKV, AG) = range(N_WIDE_BLOCKS)

GATE_ROWS = 32
ROW_TILE = 256
MLSTM_CHUNK = 256
GLA_CHUNK = 128
ATT_STEP_BLOCKS = 3
FFT_N2 = 64
FFT_PITCH = 72
FFT_SLAB_UNROLL = 4
VMEM_LIMIT = 56 * 1024 * 1024


def _dot(a, b):
    return jnp.dot(a, b, preferred_element_type=F32)


def _dot_nt(a, b):
    return lax.dot_general(a, b, (((1,), (1,)), ((), ())), preferred_element_type=F32)


def _dot_tn(a, b):
    return lax.dot_general(a, b, (((0,), (0,)), ((), ())), preferred_element_type=F32)


def _split2(x):
    hi = x.astype(BF16)
    lo = (x - hi.astype(F32)).astype(BF16)
    return hi, lo


def _split3(x):
    h1 = x.astype(BF16)
    r1 = x - h1.astype(F32)
    h2 = r1.astype(BF16)
    h3 = (r1 - h2.astype(F32)).astype(BF16)
    return h1, h2, h3


def _dot_const(ch, cl, x):
    xh, xl = _split2(x)
    return _dot(ch, xh) + _dot(ch, xl) + _dot(cl, xh)


def _dot_const2(ch, cl, x):
    xb = x.astype(BF16)
    return _dot(ch, xb) + _dot(cl, xb)


def _dot_f32(a, b):
    ah, al = _split2(a)
    bh, bl = _split2(b)
    return _dot(ah, bh) + _dot(ah, bl) + _dot(al, bh)


def _dot_f32_tn(a, b):
    ah, al = _split2(a)
    bh, bl = _split2(b)
    return _dot_tn(ah, bh) + _dot_tn(ah, bl) + _dot_tn(al, bh)


def _dot_exact_lhs(c, x):
    x1, x2, x3 = _split3(x)
    return _dot(c, x1) + _dot(c, x2) + _dot(c, x3)


def _dot_exact_rhs(x, c):
    x1, x2, x3 = _split3(x)
    return _dot(x1, c) + _dot(x2, c) + _dot(x3, c)


def _log_sigmoid(x):
    return jnp.minimum(x, 0.0) - jnp.log(1.0 + jnp.exp(-jnp.abs(x)))


def _silu(x):
    return x * jax.nn.sigmoid(x)


def _head_masks():
    lane = lax.broadcasted_iota(jnp.int32, (1, W_GROUP), 1) // HEAD_DIM
    return [(lane == h).astype(F32) for h in range(N_HEADS)]


def _block_diag_mask():
    r = lax.broadcasted_iota(jnp.int32, (W_GROUP, W_GROUP), 0) // HEAD_DIM
    c = lax.broadcasted_iota(jnp.int32, (W_GROUP, W_GROUP), 1) // HEAD_DIM
    return (r == c).astype(F32)


def _spread(cols):
    lane = lax.broadcasted_iota(jnp.int32, (1, W_GROUP), 1)
    out = cols[N_HEADS - 1]
    for h in range(N_HEADS - 2, -1, -1):
        out = jnp.where(lane < (h + 1) * HEAD_DIM, cols[h], out)
    return out


def _head_rms_norm(h, bd_bf16, g_row):
    sq = h * h
    sh, sl = _split2(sq)
    ms = (_dot(sh, bd_bf16) + _dot(sl, bd_bf16)) * (1.0 / HEAD_DIM)
    return h * lax.rsqrt(ms + EPS) * g_row


def _mod_kernel(c_ref, w_ref, b_ref, o_ref):
    s = _silu(c_ref[...])
    o_ref[0] = _dot_f32(s, w_ref[0]) + b_ref[0]


def _modulation(cc, w_ada, b_ada):
    nblk = 3 * D_MODEL // 1024
    return pl.pallas_call(
        _mod_kernel,
        grid=(DEPTH, nblk),
        in_specs=[
            pl.BlockSpec((16, D_MODEL), lambda l, j: (0, 0)),
            pl.BlockSpec((1, D_MODEL, 1024), lambda l, j: (l, 0, j)),
            pl.BlockSpec((1, 1, 1024), lambda l, j: (l, 0, j)),
        ],
        out_specs=pl.BlockSpec((1, 16, 1024), lambda l, j: (l, 0, j)),
        out_shape=jax.ShapeDtypeStruct((DEPTH, 16, 3 * D_MODEL), F32),
        compiler_params=pltpu.CompilerParams(vmem_limit_bytes=VMEM_LIMIT),
        name="adaln_modulation",
    )(cc, w_ada, b_ada.reshape(DEPTH, 1, 3 * D_MODEL))


def _token_specs(xs, n_ctx_tiles, t0):
    shape = (1, ROW_TILE, D_MODEL)
    if not isinstance(xs, tuple):
        return [pl.BlockSpec(shape, lambda b, i: (b, i + t0, 0))], [xs]
    return [pl.BlockSpec(shape, lambda b, i: (b, jnp.minimum(i + t0, n_ctx_tiles - 1), 0)),
            pl.BlockSpec(shape, lambda b, i: (b, jnp.maximum(i + t0 - n_ctx_tiles, 0), 0))], list(xs)


def _token_tile(tok_refs, n_ctx_tiles, t0):
    if len(tok_refs) == 1:
        return tok_refs[0][0]
    return jnp.where(pl.program_id(1) + t0 < n_ctx_tiles, tok_refs[0][0], tok_refs[1][0])


def _stream_shape(xs):
    if isinstance(xs, tuple):
        return xs[1].shape[0], xs[0].shape[1] + xs[1].shape[1]
    return xs.shape[0], xs.shape[1]


PROJ_TILES = 2


def _proj_kernel(*refs, with_out, parts, n_ctx_tiles, nt):
    it = iter(refs)
    take = lambda n: [next(it) for _ in range(n)]
    if with_out:
        a_ref, h_ref, g_ref, d_ref, w_ref, gp_ref = take(6)
        mod_refs = take(PROJ_TILES)
    tok_refs = take(2 * PROJ_TILES if parts else 1)
    modn_refs = take(PROJ_TILES)
    gn_ref, wp_ref, wst_ref = take(3)
    if with_out:
        (xs_ref,) = take(1)
    p_ref, gt_ref = take(2)
    step = pl.program_id(0)

    def tile(u):
        rows = slice(u * ROW_TILE, (u + 1) * ROW_TILE)
        if parts:
            i_u = (PROJ_TILES * step + u) % nt
            x = jnp.where(i_u < n_ctx_tiles, tok_refs[2 * u][0], tok_refs[2 * u + 1][0])
        else:
            x = tok_refs[0][rows, :]
        if with_out:
            mix = jnp.concatenate([a_ref[rows, :], h_ref[rows, :], g_ref[rows, :], d_ref[rows, :]], axis=1)
            y = _dot(mix, w_ref[0])
            yield
            ms = jnp.mean(y * y, axis=-1, keepdims=True)
            x = x + mod_refs[u][0, 0, :, 2 * D_MODEL:3 * D_MODEL] * (y * lax.rsqrt(ms + EPS) * gp_ref[...])
            xs_ref[rows, :] = x
        ms = jnp.mean(x * x, axis=-1, keepdims=True)
        yy = x * lax.rsqrt(ms + EPS) * gn_ref[...]
        mod = modn_refs[u]
        hb = (yy * (1.0 + mod[0, 0, :, D_MODEL:2 * D_MODEL]) + mod[0, 0, :, 0:D_MODEL]).astype(BF16)
        yield
        p_ref[rows, :] = _dot(hb, wp_ref[0])
        gt_ref[u] = _dot_nt(wst_ref[0], hb)

    _interleave(*[tile(u) for u in range(PROJ_TILES)])


def _proj(mix, xs, layer_out, layer_in, w_o, g_post, g_pre, mod_all, w_p, w_st, n_ctx_tiles):
    b_, lt = _stream_shape(xs)
    nt = lt // ROW_TILE
    n_tiles = b_ * nt
    assert n_tiles % PROJ_TILES == 0
    step_rows = PROJ_TILES * ROW_TILE
    with_out, parts = mix is not None, isinstance(xs, tuple)
    b_of = lambda s, u: (PROJ_TILES * s + u) // nt
    i_of = lambda s, u: (PROJ_TILES * s + u) % nt
    flat = lambda width: pl.BlockSpec((step_rows, width), lambda s: (s, 0))
    mod_spec = lambda lyr, u: pl.BlockSpec(
        (1, 1, 1, 3 * D_MODEL), lambda s: (lyr, 2 * b_of(s, u) + jnp.where(i_of(s, u) >= n_ctx_tiles, 1, 0), 0, 0))
    in_specs, args = [], []
    if with_out:
        in_specs += [flat(W_GROUP)] * 4 + [pl.BlockSpec((1, D_MODEL, D_MODEL), lambda s: (layer_out, 0, 0)),
                                           pl.BlockSpec((1, D_MODEL), lambda s: (0, 0))]
        in_specs += [mod_spec(layer_out, u) for u in range(PROJ_TILES)]
        args += [m.reshape(b_ * lt, W_GROUP) for m in mix] + [w_o, g_post.reshape(1, D_MODEL)] + [mod_all] * PROJ_TILES
    if parts:
        for u in range(PROJ_TILES):
            in_specs += [pl.BlockSpec((1, ROW_TILE, D_MODEL),
                                      lambda s, u=u: (b_of(s, u), jnp.minimum(i_of(s, u), n_ctx_tiles - 1), 0)),
                         pl.BlockSpec((1, ROW_TILE, D_MODEL),
                                      lambda s, u=u: (b_of(s, u), jnp.maximum(i_of(s, u) - n_ctx_tiles, 0), 0))]
            args += list(xs)
    else:
        in_specs.append(flat(D_MODEL))
        args.append(xs.reshape(b_ * lt, D_MODEL))
    in_specs += [mod_spec(layer_in, u) for u in range(PROJ_TILES)]
    in_specs += [pl.BlockSpec((1, D_MODEL), lambda s: (0, 0)),
                 pl.BlockSpec((1, D_MODEL, N_PROJ), lambda s: (layer_in, 0, 0)),
                 pl.BlockSpec((1, GATE_ROWS, D_MODEL), lambda s: (layer_in, 0, 0))]
    args += [mod_all] * PROJ_TILES + [g_pre.reshape(1, D_MODEL), w_p, w_st]
    out_specs = [flat(N_PROJ), pl.BlockSpec((PROJ_TILES, GATE_ROWS, ROW_TILE), lambda s: (s, 0, 0))]
    out_shape = [jax.ShapeDtypeStruct((b_ * lt, N_PROJ), F32), jax.ShapeDtypeStruct((n_tiles, GATE_ROWS, ROW_TILE), F32)]
    if with_out:
        out_specs.insert(0, flat(D_MODEL))
        out_shape.insert(0, jax.ShapeDtypeStruct((b_ * lt, D_MODEL), F32))
    outs = pl.pallas_call(
        functools.partial(_proj_kernel, with_out=with_out, parts=parts, n_ctx_tiles=n_ctx_tiles, nt=nt),
        grid=(n_tiles // PROJ_TILES,),
        in_specs=in_specs,
        out_specs=out_specs,
        out_shape=out_shape,
        compiler_params=pltpu.CompilerParams(vmem_limit_bytes=VMEM_LIMIT),
        name="outproj_inproj" if with_out else "inproj",
    )(*args)
    xs_new = outs[0].reshape(b_, lt, D_MODEL) if with_out else None
    return xs_new, outs[-2].reshape(b_, lt, N_PROJ), outs[-1]


def _outproj_kernel(a_ref, h_ref, g_ref, d_ref, w_ref, gp_ref, mod_ref, *refs, n_tok, n_ctx_tiles, t0):
    o_ref = refs[n_tok]
    mix = jnp.concatenate([a_ref[0], h_ref[0], g_ref[0], d_ref[0]], axis=1)
    y = _dot(mix, w_ref[0])
    ms = jnp.mean(y * y, axis=-1, keepdims=True)
    yn = y * lax.rsqrt(ms + EPS) * gp_ref[...]
    gt = mod_ref[0, 0, :, 2 * D_MODEL:3 * D_MODEL]
    o_ref[0] = _token_tile(refs[:n_tok], n_ctx_tiles, t0) + gt * yn


def _outproj(mix, w_o, layer, g_post, mod_all, xs, n_ctx_tiles, skip_ctx):
    b_, lt = _stream_shape(xs)
    nt = lt // ROW_TILE
    t0 = n_ctx_tiles if skip_ctx else 0
    rows = lambda b, i: (b, i + t0, 0)
    tok_specs, tok_args = _token_specs(xs, n_ctx_tiles, t0)
    return pl.pallas_call(
        functools.partial(_outproj_kernel, n_tok=len(tok_args), n_ctx_tiles=n_ctx_tiles, t0=t0),
        grid=(b_, nt - t0),
        in_specs=[pl.BlockSpec((1, ROW_TILE, W_GROUP), rows)] * 4 + [
            pl.BlockSpec((1, D_MODEL, D_MODEL), lambda b, i: (layer, 0, 0)),
            pl.BlockSpec((1, D_MODEL), lambda b, i: (0, 0)),
            pl.BlockSpec((1, 1, 1, 3 * D_MODEL),
                         lambda b, i: (layer, 2 * b + jnp.where(i + t0 >= n_ctx_tiles, 1, 0), 0, 0)),
        ] + tok_specs,
        out_specs=pl.BlockSpec((1, ROW_TILE, D_MODEL), lambda b, i: (b, i, 0)),
        out_shape=jax.ShapeDtypeStruct((b_, lt - t0 * ROW_TILE, D_MODEL), F32),
        compiler_params=pltpu.CompilerParams(vmem_limit_bytes=VMEM_LIMIT),
        name="outproj",
    )(*mix, w_o, g_post.reshape(1, D_MODEL), mod_all, *tok_args)


def _interleave(*gens):
    live = list(gens)
    while live:
        live = [g for g in live if next(g, StopIteration) is not StopIteration]


def _scan_chunk(s, reverse, n_ctx, n_tot):
    if not reverse:
        return s
    return jnp.where(s < n_ctx, n_ctx - 1 - s, n_tot - 1 - (s - n_ctx))


def _mlstm_kernel(q_ref, k_ref, v_ref, o_ref, sg_ref, gt_ref, brow_ref, ng_ref,
                  bd_ref, mones_ref, eall_ref, out_ref, hf_ref, hb_ref, c_ref, n_ref, pt_ref, r8_ref, ab_ref,
                  *, chunk, n_ctx, n_tot):
    t_ = chunk
    masks = _head_masks()
    masks_bf = [m.astype(BF16) for m in masks]
    bdm = _block_diag_mask()
    row_i = lax.broadcasted_iota(jnp.int32, (t_, t_), 0)
    col_i = lax.broadcasted_iota(jnp.int32, (t_, t_), 1)
    lane_a = lax.broadcasted_iota(jnp.int32, (8 * n_tot, t_), 1)
    live = (lax.broadcasted_iota(jnp.int32, (8 * n_tot, t_), 0) % 8) < N_HEADS
    c_ref[...] = jnp.zeros_like(c_ref)
    n_ref[...] = jnp.zeros_like(n_ref)

    def scan(x, reverse, op):
        k = 1
        while k < t_:
            if reverse:
                x = jnp.where(lane_a < t_ - k, op(x, pltpu.roll(x, t_ - k, 1)), x)
            else:
                x = jnp.where(lane_a >= k, op(x, pltpu.roll(x, k, 1)), x)
            k *= 2
        return x

    def gates(reverse):
        d_ = 1 if reverse else 0
        end = 0 if reverse else t_ - 1
        order = list(range(n_tot))
        if reverse:
            order = order[:n_ctx][::-1] + order[n_ctx:][::-1]

        def rows_of(g):
            r_ = slice(8 * g, 8 * g + 8)
            return jnp.concatenate([gt_ref[c, r_, :] for c in range(n_tot)], axis=0) + jnp.concatenate(
                [brow_ref[r_, :]] * n_tot, axis=0)

        i_all = jnp.where(live, rows_of(2 * d_), 0.0)
        lf_all = jnp.where(live, _log_sigmoid(rows_of(2 * d_ + 1)), 0.0)
        b_all = scan(lf_all, reverse, jnp.add)
        r_all = i_all - b_all
        cm_all = scan(r_all, reverse, jnp.maximum)
        yield
        m_prev = jnp.zeros((8, 1), F32)
        for c in order:
            cols = slice(c * t_, (c + 1) * t_)
            blk8 = slice(8 * c, 8 * c + 8)
            i8, b8, r8 = i_all[blk8], b_all[blk8], r_all[blk8]
            inter = b8 + m_prev
            m_t = jnp.maximum(inter, b8 + cm_all[blk8])
            b_end = b8[:, end:end + 1]
            g8 = b_end - b8 + i8
            m_new = jnp.maximum(b_end + m_prev, jnp.max(g8, axis=1, keepdims=True))
            a_prev = jnp.exp(b_end + m_prev - m_new)
            packed = jnp.concatenate([b8 - m_t, jnp.exp(inter - m_t), jnp.exp(g8 - m_new), jnp.exp(-m_t),
                                      jnp.zeros((128 - 32, t_), F32)], axis=0)
            pt_ref[d_, cols, :] = packed.T
            r8_ref[d_, :, cols] = r8
            a_b = a_prev[0:1] * masks[0]
            for h in range(1, N_HEADS):
                a_b = a_b + a_prev[h:h + 1] * masks[h]
            ab_ref[d_, c] = a_b
            m_prev = m_new
            yield

    _interleave(gates(False), gates(True))

    def step(reverse, c_idx, h_ref):
        d_ = 1 if reverse else 0
        valid = (col_i >= row_i) if reverse else (col_i <= row_i)
        r0 = pl.multiple_of(c_idx * t_, t_)
        q = q_ref[0, pl.ds(r0, t_), :]
        k = k_ref[0, pl.ds(r0, t_), :] * (HEAD_DIM ** -0.5)
        qb = q.astype(BF16)
        kb = k.astype(BF16)
        vb = v_ref[0, pl.ds(r0, t_), :].astype(BF16)
        pt = pt_ref[d_, pl.ds(r0, t_), :]
        r8 = r8_ref[d_, :, pl.ds(r0, t_)]
        c_prev = c_ref[d_]
        n_prev = n_ref[d_]
        spread = _dot(pt.astype(BF16), eall_ref[...])
        s_cat = []
        for h in range(N_HEADS):
            w_intra = jnp.where(valid, jnp.exp(pt[:, h:h + 1] + r8[h:h + 1, :]), 0.0)
            s_cat.append((_dot_nt(qb * masks_bf[h], kb) * w_intra).astype(BF16))
        yield
        v_stack = jnp.concatenate([jnp.concatenate([vb * masks_bf[h] for h in range(N_HEADS)], axis=0),
                                   mones_ref[...]], axis=1)
        acc = _dot(jnp.concatenate(s_cat, axis=1), v_stack)
        qc = _dot(qb, c_prev.astype(BF16))
        qn = _dot((q * n_prev).astype(BF16), bd_ref[...])
        yield
        wi_b = spread[:, 0:W_GROUP]
        num = wi_b * qc + acc[:, 0:W_GROUP]
        den = wi_b * qn + acc[:, W_GROUP:2 * W_GROUP]
        h_ref[pl.ds(r0, t_), :] = num / jnp.maximum(jnp.abs(den), spread[:, 2 * W_GROUP:3 * W_GROUP])
        a_b = ab_ref[d_, c_idx]
        kw = k * spread[:, W_GROUP:2 * W_GROUP]
        c_ref[d_] = a_b * c_prev + bdm * _dot_tn(kw.astype(BF16), vb)
        n_ref[d_] = a_b * n_prev + jnp.sum(kw, axis=0, keepdims=True)

    def body(s, carry):
        _interleave(step(False, _scan_chunk(s, False, n_ctx, n_tot), hf_ref),
                    step(True, _scan_chunk(s, True, n_ctx, n_tot), hb_ref))
        return carry

    lax.fori_loop(0, n_tot, body, 0)

    def finish(c_idx, carry):
        r0 = pl.multiple_of(c_idx * t_, t_)
        hsum = hf_ref[pl.ds(r0, t_), :] + hb_ref[pl.ds(r0, t_), :]
        hsum = hsum * jax.nn.sigmoid(o_ref[0, pl.ds(r0, t_), :])
        hn = _head_rms_norm(hsum, bd_ref[...], ng_ref[...])
        out_ref[0, pl.ds(r0, t_), :] = (hn * _silu(sg_ref[0, pl.ds(r0, t_), :])).astype(out_ref.dtype)
        return carry

    lax.fori_loop(0, n_tot, finish, 0)


def _mlstm(p, gt, gate_b, norm_g, consts, ctx_len):
    b_, lt, _ = p.shape
    t_ = MLSTM_CHUNK
    assert t_ == ROW_TILE
    brow = jnp.pad(gate_b.reshape(4, N_HEADS), ((0, 0), (0, 8 - N_HEADS))).reshape(GATE_ROWS, 1)
    blk = lambda j: pl.BlockSpec((1, lt, W_GROUP), lambda b: (b, 0, j))
    full = lambda shape: pl.BlockSpec(shape, lambda b: (0,) * len(shape))
    kern = functools.partial(_mlstm_kernel, chunk=t_, n_ctx=ctx_len // t_, n_tot=lt // t_)
    return pl.pallas_call(
        kern,
        grid=(b_,),
        in_specs=[blk(MQ), blk(MK), blk(MV), blk(MO), blk(MG),
                  pl.BlockSpec((lt // t_, GATE_ROWS, t_), lambda b: (b, 0, 0)),
                  full((GATE_ROWS, 1)), full((1, W_GROUP)),
                  full((W_GROUP, W_GROUP)), full((N_HEADS * t_, W_GROUP)), full((128, 3 * W_GROUP))],
        out_specs=pl.BlockSpec((1, lt, W_GROUP), lambda b: (b, 0, 0)),
        out_shape=jax.ShapeDtypeStruct((b_, lt, W_GROUP), BF16),
        scratch_shapes=[pltpu.VMEM((lt, W_GROUP), F32), pltpu.VMEM((lt, W_GROUP), F32),
                        pltpu.VMEM((2, W_GROUP, W_GROUP), F32), pltpu.VMEM((2, 1, W_GROUP), F32),
                        pltpu.VMEM((2, lt, 128), F32), pltpu.VMEM((2, 8, lt), F32),
                        pltpu.VMEM((2, lt // t_, 1, W_GROUP), F32)],
        compiler_params=pltpu.CompilerParams(vmem_limit_bytes=VMEM_LIMIT),
        name="mlstm_mixer",
    )(p, p, p, p, p, gt, brow, norm_g.reshape(1, W_GROUP),
      consts["bd"], consts["mones_m"], consts["eall_m"])


def _gla_kernel(q_ref, k_ref, v_ref, sg_ref, sm_ref, wa_ref, ba_ref, ng_ref, tril_ref, triu_ref, bd_ref,
                out_ref, hf_ref, hb_ref, st_ref, qd_ref, kd_ref, kc_ref, eb_ref, *, chunk, n_ctx, n_tot):
    t_ = chunk
    masks_bf = [m.astype(BF16) for m in _head_masks()]
    bdm = _block_diag_mask()
    row_i = lax.broadcasted_iota(jnp.int32, (t_, t_), 0)
    col_i = lax.broadcasted_iota(jnp.int32, (t_, t_), 1)
    st_ref[...] = jnp.zeros_like(st_ref)
    wa_split = [_split2(wa_ref[d_]) for d_ in range(2)]

    def decay(reverse, c_idx):
        d_ = 1 if reverse else 0
        tri_c = triu_ref[...] if reverse else tril_ref[...]
        end_row = 0 if reverse else t_ - 1
        r0 = pl.multiple_of(c_idx * t_, t_)
        sm = sm_ref[0, pl.ds(r0, t_), :]
        smh, sml = _split2(sm)
        wah, wal = wa_split[d_]
        la = _log_sigmoid(_dot(smh, wah) + _dot(smh, wal) + _dot(sml, wah) + ba_ref[d_]) * (1.0 / GLA_NORMALIZER)
        yield
        bcum = _dot_exact_lhs(tri_c, la)
        yield
        q = q_ref[0, pl.ds(r0, t_), :] * (HEAD_DIM ** -0.5)
        k = k_ref[0, pl.ds(r0, t_), :]
        b_end = bcum[end_row:end_row + 1, :]
        qd_ref[d_, pl.ds(r0, t_), :] = (q * jnp.exp(bcum)).astype(BF16)
        kd_ref[d_, pl.ds(r0, t_), :] = (k * jnp.exp(-bcum)).astype(BF16)
        kc_ref[d_, pl.ds(r0, t_), :] = (k * jnp.exp(b_end - bcum)).astype(BF16)
        eb_ref[d_, c_idx] = jnp.exp(b_end)

    def decay_body(s, carry):
        _interleave(*[decay(rev, 2 * s + u) for u in range(2) for rev in (False, True)])
        return carry

    lax.fori_loop(0, n_tot // 2, decay_body, 0)

    def step(reverse, c_idx, h_ref):
        d_ = 1 if reverse else 0
        valid = (col_i >= row_i) if reverse else (col_i <= row_i)
        r0 = pl.multiple_of(c_idx * t_, t_)
        vb = v_ref[0, pl.ds(r0, t_), :].astype(BF16)
        qdb = qd_ref[d_, pl.ds(r0, t_), :]
        kd = kd_ref[d_, pl.ds(r0, t_), :]
        st_prev = st_ref[d_]
        att = [jnp.where(valid, _dot_nt(qdb * masks_bf[h], kd), 0.0).astype(BF16) for h in range(N_HEADS)]
        yield
        v_stack = jnp.concatenate([vb * masks_bf[h] for h in range(N_HEADS)], axis=0)
        h_ref[pl.ds(r0, t_), :] = (_dot_nt(qdb, st_prev.astype(BF16))
                                   + _dot(jnp.concatenate(att, axis=1), v_stack))
        st_ref[d_] = st_prev * eb_ref[d_, c_idx] + bdm * _dot_tn(vb, kc_ref[d_, pl.ds(r0, t_), :])

    def body(s, carry):
        _interleave(step(False, _scan_chunk(s, False, n_ctx, n_tot), hf_ref),
                    step(True, _scan_chunk(s, True, n_ctx, n_tot), hb_ref))
        return carry

    lax.fori_loop(0, n_tot, body, 0)

    def finish(i, carry):
        r0 = pl.multiple_of(i * ROW_TILE, ROW_TILE)
        hsum = hf_ref[pl.ds(r0, ROW_TILE), :] + hb_ref[pl.ds(r0, ROW_TILE), :]
        hn = _head_rms_norm(hsum, bd_ref[...], ng_ref[...])
        out_ref[0, pl.ds(r0, ROW_TILE), :] = (hn * _silu(sg_ref[0, pl.ds(r0, ROW_TILE), :])).astype(out_ref.dtype)
        return carry

    lax.fori_loop(0, n_tot * t_ // ROW_TILE, finish, 0)


def _gla(p, w_alpha, b_alpha, norm_g, consts, ctx_len):
    b_, lt, _ = p.shape
    t_ = GLA_CHUNK
    wa = jnp.zeros((2, 128, W_GROUP), F32)
    wa = wa.at[0, 16:32].set(w_alpha[0]).at[1, 32:48].set(w_alpha[1])
    blk = lambda j: pl.BlockSpec((1, lt, W_GROUP), lambda b: (b, 0, j))
    full = lambda shape: pl.BlockSpec(shape, lambda b: (0,) * len(shape))
    kern = functools.partial(_gla_kernel, chunk=t_, n_ctx=ctx_len // t_, n_tot=lt // t_)
    return pl.pallas_call(
        kern,
        grid=(b_,),
        in_specs=[blk(GQ), blk(GK), blk(GV), blk(GG),
                  pl.BlockSpec((1, lt, 128), lambda b: (b, 0, SMALL_BLOCK)),
                  full((2, 128, W_GROUP)), full((2, 1, W_GROUP)), full((1, W_GROUP)),
                  full((t_, t_)), full((t_, t_)), full((W_GROUP, W_GROUP))],
        out_specs=pl.BlockSpec((1, lt, W_GROUP), lambda b: (b, 0, 0)),
        out_shape=jax.ShapeDtypeStruct((b_, lt, W_GROUP), BF16),
        scratch_shapes=[pltpu.VMEM((lt, W_GROUP), F32), pltpu.VMEM((lt, W_GROUP), F32),
                        pltpu.VMEM((2, W_GROUP, W_GROUP), F32),
                        pltpu.VMEM((2, lt, W_GROUP), BF16), pltpu.VMEM((2, lt, W_GROUP), BF16),
                        pltpu.VMEM((2, lt, W_GROUP), BF16), pltpu.VMEM((2, lt // t_, 1, W_GROUP), F32)],
        compiler_params=pltpu.CompilerParams(vmem_limit_bytes=VMEM_LIMIT),
        name="gla_mixer",
    )(p, p, p, p, p, wa, b_alpha.reshape(2, 1, W_GROUP), norm_g.reshape(1, W_GROUP),
      consts["tril_g"], consts["triu_g"], consts["bd"])


def _attn_kernel(q_ref, kv_ref, sg_ref, cos_ref, sin_ref, sink_ref, pswap_ref, out_ref,
                 kb_ref, vb_ref, *, ctx_len, lat_len):
    g = pl.program_id(1)
    lt = ctx_len + lat_len
    n_ctx_blk = ctx_len // ATT_BLOCK
    nwin = 3 * ATT_BLOCK
    masks = _head_masks()
    log2e = math.log2(math.e)
    low_half = lax.broadcasted_iota(jnp.int32, (1, 2 * HEAD_DIM), 1) < HEAD_DIM

    def rope(x, r0, n):
        swapped = _dot(x.astype(BF16), pswap_ref[...])
        return x * cos_ref[pl.ds(r0, n), :] + swapped * sin_ref[pl.ds(r0, n), :]

    @pl.when(g == 0)
    def _():
        def fill(i, carry):
            r0 = pl.multiple_of(i * ROW_TILE, ROW_TILE)
            k = kv_ref[0, pl.ds(r0, ROW_TILE), 0:2 * HEAD_DIM]
            v = kv_ref[0, pl.ds(r0, ROW_TILE), 2 * HEAD_DIM:4 * HEAD_DIM]
            k_sw = pltpu.roll(k, HEAD_DIM, 1)
            v_sw = pltpu.roll(v, HEAD_DIM, 1)
            k4 = jnp.concatenate([jnp.where(low_half, k, k_sw), jnp.where(low_half, k_sw, k)], axis=1)
            kb_ref[pl.ds(r0, ROW_TILE), :] = rope(k4, r0, ROW_TILE).astype(BF16)
            vb_ref[pl.ds(r0, ROW_TILE), :] = jnp.concatenate(
                [jnp.where(low_half, v, 1.0), jnp.where(low_half, v_sw, 1.0)], axis=1).astype(BF16)
            return carry
        lax.fori_loop(0, lt // ROW_TILE, fill, 0)

    k_ctx = kb_ref[0:ctx_len, :]
    v_ctx = vb_ref[0:ctx_len, :]
    neg_inf = jnp.float32(-jnp.inf)
    rel0 = (lax.broadcasted_iota(jnp.int32, (ATT_BLOCK, nwin), 1)
            - lax.broadcasted_iota(jnp.int32, (ATT_BLOCK, nwin), 0))

    def block(u):
        j = g * ATT_STEP_BLOCKS + u
        rows = slice(u * ATT_BLOCK, (u + 1) * ATT_BLOCK)
        is_lat = j >= n_ctx_blk
        i_lat = jnp.maximum(j - n_ctx_blk, 0)
        w0 = pl.multiple_of(ctx_len + jnp.clip((i_lat - 1) * ATT_BLOCK, 0, lat_len - nwin), ATT_BLOCK)
        q0 = pl.multiple_of(j * ATT_BLOCK, ATT_BLOCK)
        q = rope(q_ref[0, rows, :], q0, ATT_BLOCK) * (HEAD_DIM ** -0.5 * log2e)
        k_win = kb_ref[pl.ds(w0, nwin), :]
        v_win = vb_ref[pl.ds(w0, nwin), :]
        band = is_lat & (jnp.abs(rel0 + (w0 - q0)) <= WINDOW)
        q_stack = jnp.concatenate([(q * masks[h]).astype(BF16) for h in range(N_HEADS)], axis=0)
        band4 = jnp.concatenate([band] * N_HEADS, axis=0)
        sink = jnp.concatenate([jnp.broadcast_to(sink_ref[h] * log2e, (ATT_BLOCK, 128)) for h in range(N_HEADS)],
                               axis=0)
        s_loc = jnp.where(band4, _dot_nt(q_stack, k_win), neg_inf)
        s_ctx = _dot_nt(q_stack, k_ctx)
        yield
        m = jnp.maximum(jnp.max(s_loc, axis=-1, keepdims=True), jnp.max(s_ctx, axis=-1, keepdims=True))
        m128 = jnp.maximum(jnp.broadcast_to(m, (N_HEADS * ATT_BLOCK, 128)), sink)
        p_loc = jnp.exp2(s_loc - jnp.concatenate([m128] * (nwin // 128), axis=1)).astype(BF16)
        p_ctx = jnp.exp2(s_ctx - jnp.concatenate([m128] * (ctx_len // 128), axis=1)).astype(BF16)
        yield
        o_all = _dot(p_loc, v_win) + _dot(p_ctx, v_ctx)
        e_sink = jnp.exp2(sink - m128)
        o = jnp.zeros((ATT_BLOCK, W_GROUP), F32)
        for h in range(N_HEADS):
            hr = slice(h * ATT_BLOCK, (h + 1) * ATT_BLOCK)
            if h % 2 == 0:
                num, den = o_all[hr], pltpu.roll(o_all[hr], W_GROUP - HEAD_DIM, 1)
            else:
                num, den = pltpu.roll(o_all[hr], HEAD_DIM, 1), o_all[hr]
            den = den + jnp.concatenate([e_sink[hr], e_sink[hr]], axis=1)
            o = jnp.where(masks[h] > 0.0, num / den, o)
        out_ref[0, rows, :] = (o * _silu(sg_ref[0, rows, :])).astype(out_ref.dtype)

    _interleave(*[block(u) for u in range(ATT_STEP_BLOCKS)])


def _attn(p, sink, consts, ctx_len):
    b_, lt, _ = p.shape
    step_rows = ATT_STEP_BLOCKS * ATT_BLOCK
    nb = lt // step_rows
    kern = functools.partial(_attn_kernel, ctx_len=ctx_len, lat_len=lt - ctx_len)
    return pl.pallas_call(
        kern,
        grid=(b_, nb),
        in_specs=[pl.BlockSpec((1, step_rows, W_GROUP), lambda b, j: (b, j, AQ)),
                  pl.BlockSpec((1, lt, W_GROUP), lambda b, j: (b, 0, AKV)),
                  pl.BlockSpec((1, step_rows, W_GROUP), lambda b, j: (b, j, AG)),
                  pl.BlockSpec((lt, W_GROUP), lambda b, j: (0, 0)),
                  pl.BlockSpec((lt, W_GROUP), lambda b, j: (0, 0)),
                  pl.BlockSpec((N_HEADS, 1, 1), lambda b, j: (0, 0, 0)),
                  pl.BlockSpec((W_GROUP, W_GROUP), lambda b, j: (0, 0))],
        out_specs=pl.BlockSpec((1, step_rows, W_GROUP), lambda b, j: (b, j, 0)),
        out_shape=jax.ShapeDtypeStruct((b_, lt, W_GROUP), BF16),
        scratch_shapes=[pltpu.VMEM((lt, W_GROUP), BF16), pltpu.VMEM((lt, W_GROUP), BF16)],
        compiler_params=pltpu.CompilerParams(vmem_limit_bytes=VMEM_LIMIT,
                                             dimension_semantics=("arbitrary", "arbitrary")),
        name="window_attention",
    )(p, p, p, consts["rope_cos"], consts["rope_sin"], sink.reshape(N_HEADS, 1, 1), consts["pswap"])


class _FftPlan:
    def __init__(self, n1):
        self.n1 = n1
        self.n = n1 * FFT_N2
        self.k1p = -(-(n1 // 2 + 1) // 8) * 8


def _fft_constants(n1, n_in, n_out):
    plan = _FftPlan(n1)
    n, k1p, half = plan.n, plan.k1p, n1 // 2
    k1 = np.arange(k1p)[:, None].astype(np.float64)
    live = (np.arange(k1p) <= half)[:, None]
    i1 = np.arange(n1)[None, :].astype(np.float64)
    ang = 2.0 * np.pi * k1 * i1 / n1
    f1 = np.concatenate([np.where(live, np.cos(ang), 0.0), np.where(live, -np.sin(ang), 0.0)], axis=0)
    i2 = np.arange(FFT_N2)[None, :].astype(np.float64)
    phi = 2.0 * np.pi * k1 * i2 / n
    tw = np.stack([np.cos(phi), -np.sin(phi)], axis=0)
    tw = np.broadcast_to(tw.reshape(2, k1p * FFT_N2, 1), (2, k1p * FFT_N2, 128))
    th = 2.0 * np.pi * np.outer(np.arange(FFT_N2), np.arange(FFT_N2)) / FFT_N2
    c2, s2 = np.cos(th), np.sin(th)
    m_fwd = np.block([[c2, s2], [-s2, c2]])
    m_inv = np.block([[c2, -s2], [s2, c2]])
    wk = np.where(np.arange(k1p) <= half, 2.0, 0.0)
    wk[0] = 1.0
    wk[half] = 1.0
    psi = 2.0 * np.pi * np.outer(np.arange(n1), np.arange(k1p)) / n1
    g = np.concatenate([np.cos(psi) * wk[None, :], -np.sin(psi) * wk[None, :]], axis=1) / n

    def hl(a):
        a32 = jnp.asarray(a, F32)
        hi = a32.astype(BF16)
        lo = (a32 - hi.astype(F32)).astype(BF16)
        return hi, lo

    return dict(f1=hl(f1), f1_in=hl(f1[:, 0:n_in]), g=hl(g[0:n_out]), m_fwd=hl(m_fwd), m_inv=hl(m_inv),
                tw=jnp.asarray(tw, F32))


def _slab_row(k):
    return k * FFT_PITCH if isinstance(k, int) else pl.multiple_of(k * FFT_PITCH, 8)


def _load_slab(ref, k):
    r0 = _slab_row(k)
    return jnp.concatenate([ref[0, pl.ds(r0, FFT_N2), :], ref[1, pl.ds(r0, FFT_N2), :]], axis=1)


def _store_slab(ref, k, val):
    r0 = _slab_row(k)
    ref[0, pl.ds(r0, FFT_N2), :] = val[:, 0:128]
    ref[1, pl.ds(r0, FFT_N2), :] = val[:, 128:256]


FFT_GROUP = 8


def _cross_slab(src_refs, n_src, mat_h, mat_l, dst_refs, n_dst, dot=_dot_const):
    def body(gidx, carry):
        cols = []
        for i in range(FFT_GROUP):
            i2 = gidx * FFT_GROUP + i
            parts = []
            for half in range(2):
                parts.append(jnp.concatenate(
                    [r[half, pl.ds(i2, n_src, stride=FFT_PITCH), :] for r in src_refs], axis=0))
            cols.append(jnp.concatenate(parts, axis=1))
        x = jnp.concatenate(cols, axis=1)
        y = dot(mat_h, mat_l, x)
        for i in range(FFT_GROUP):
            i2 = gidx * FFT_GROUP + i
            for half in range(2):
                c0 = (2 * i + half) * 128
                for d, ref in enumerate(dst_refs):
                    ref[half, pl.ds(i2, n_dst, stride=FFT_PITCH), :] = y[d * n_dst:(d + 1) * n_dst, c0:c0 + 128]
        return carry

    lax.fori_loop(0, FFT_N2 // FFT_GROUP, body, 0)


def _twiddle(ar, ai, tw_ref, k, conj):
    r0 = pl.multiple_of(k * FFT_N2, FFT_N2)
    tr = tw_ref[0, pl.ds(r0, FFT_N2), :]
    ti = tw_ref[1, pl.ds(r0, FFT_N2), :]
    tr = jnp.concatenate([tr, tr], axis=1)
    ti = jnp.concatenate([ti, ti], axis=1)
    if conj:
        return ar * tr + ai * ti, ai * tr - ar * ti
    return ar * tr - ai * ti, ar * ti + ai * tr


def _filter_kernel(feat_ref, dec_ref, w1_ref, b1_ref, w2_ref, b2_ref, w3_ref, b3_ref, fq_ref,
                   f1h_ref, f1l_ref, mh_ref, ml_ref, tw_ref, out_ref, zbuf, are, aim, *, n1, k1p):
    half = n1 // 2

    u_ = FFT_SLAB_UNROLL
    rows = u_ * FFT_N2

    def fill_block(g):
        r0 = pl.multiple_of(g * rows, rows)
        z = feat_ref[:, pl.ds(r0, rows)]
        h = jnp.sin(fq_ref[0, 0] * (_dot_f32_tn(w1_ref[0], z) + b1_ref[0]))
        yield
        h = jnp.sin(fq_ref[0, 1] * (_dot_f32_tn(w2_ref[0], h) + b2_ref[0]))
        yield
        h = _dot_f32_tn(h, w3_ref[0]) + b3_ref[0]
        hsel = jnp.where(g * u_ < half, h[:, 0:W_GROUP], h[:, W_GROUP:2 * W_GROUP])
        kern = hsel * dec_ref[pl.ds(r0, rows), :]
        for u in range(u_):
            _store_slab(zbuf, g * u_ + u, kern[u * FFT_N2:(u + 1) * FFT_N2])

    def fill(s, carry):
        _interleave(fill_block(2 * s), fill_block(2 * s + 1))
        return carry

    lax.fori_loop(0, n1 // u_ // 2, fill, 0)
    _cross_slab([zbuf], n1, f1h_ref[...], f1l_ref[...], [are, aim], k1p)

    def slab(g, carry):
        cols = []
        for u in range(u_):
            k = g * u_ + u
            ar, ai = _twiddle(_load_slab(are, k), _load_slab(aim, k), tw_ref, k, conj=False)
            cols.append(jnp.concatenate([ar, ai], axis=0))
        x = _dot_const(mh_ref[...], ml_ref[...], jnp.concatenate(cols, axis=1))
        for u in range(u_):
            r0 = pl.multiple_of((g * u_ + u) * FFT_N2, FFT_N2)
            out_ref[0, 0, pl.ds(r0, FFT_N2), :] = x[0:FFT_N2, u * W_GROUP:(u + 1) * W_GROUP]
            out_ref[0, 1, pl.ds(r0, FFT_N2), :] = x[FFT_N2:2 * FFT_N2, u * W_GROUP:(u + 1) * W_GROUP]
        return carry

    lax.fori_loop(0, k1p // u_, slab, 0)


def _filter_tables(l_):
    pos = np.concatenate([np.arange(l_), l_ - np.arange(l_)]).astype(np.float32)
    pos[l_] = 0.0
    t = pos / np.float32(max(l_ - 1, 1))
    w = np.float32(2.0 * math.pi) * pos / np.float32(l_)
    f = np.linspace(1e-4, HYENA_BANDS - 1, HYENA_BANDS, dtype=np.float32)
    feat = np.zeros((2 * l_, 64), np.float32)
    feat[:, 0] = t
    feat[:, 1:1 + HYENA_BANDS] = np.cos(w[:, None] * f)
    feat[:, 1 + HYENA_BANDS:HYENA_EMB] = -np.sin(w[:, None] * f)
    deltas = np.abs(np.linspace(HYENA_MIN_DECAY, HYENA_MAX_DECAY, W_GROUP, dtype=np.float32))
    dec = np.exp(-t[:, None] * deltas[None, :]).astype(np.float32)
    dec[l_] = 0.0
    return jnp.asarray(feat), jnp.asarray(dec)


def _filter_spectra(l_, w1p, b1, w2, b2, w3, b3, freq, fc):
    n1 = 2 * l_ // FFT_N2
    plan = _FftPlan(n1)
    k1p = plan.k1p
    feat, dec = _filter_tables(l_)
    full = lambda shape: pl.BlockSpec(shape, lambda l: (0,) * len(shape))
    per = lambda shape: pl.BlockSpec((1,) + shape, lambda l: (l,) + (0,) * len(shape))
    kern = functools.partial(_filter_kernel, n1=n1, k1p=k1p)
    return pl.pallas_call(
        kern,
        grid=(DEPTH,),
        in_specs=[full((64, 2 * l_)), full((2 * l_, W_GROUP)),
                  per((64, HYENA_FFN)), per((HYENA_FFN, 1)), per((HYENA_FFN, HYENA_FFN)), per((HYENA_FFN, 1)),
                  per((HYENA_FFN, 2 * W_GROUP)), per((1, 2 * W_GROUP)), per((2, HYENA_FFN, 1)),
                  full((2 * k1p, n1)), full((2 * k1p, n1)), full((128, 128)), full((128, 128)),
                  full((2, k1p * FFT_N2, 128))],
        out_specs=pl.BlockSpec((1, 2, k1p * FFT_N2, W_GROUP), lambda l: (l, 0, 0, 0)),
        out_shape=jax.ShapeDtypeStruct((DEPTH, 2, k1p * FFT_N2, W_GROUP), F32),
        scratch_shapes=[pltpu.VMEM((2, n1 * FFT_PITCH, 128), F32),
                        pltpu.VMEM((2, k1p * FFT_PITCH, 128), F32),
                        pltpu.VMEM((2, k1p * FFT_PITCH, 128), F32)],
        compiler_params=pltpu.CompilerParams(vmem_limit_bytes=VMEM_LIMIT),
        name="hyena_filter_spectrum_%d" % l_,
    )(feat.T, dec, w1p, b1.reshape(DEPTH, HYENA_FFN, 1), w2, b2.reshape(DEPTH, HYENA_FFN, 1),
      w3, b3.reshape(DEPTH, 1, 2 * W_GROUP), freq.reshape(DEPTH, 2, HYENA_FFN, 1),
      fc["f1"][0], fc["f1"][1], fc["m_fwd"][0], fc["m_fwd"][1], fc["tw"])


def _short_dft_constants(l_):
    n = 2 * l_
    kp = -(-(l_ + 1) // 8) * 8
    k = np.arange(kp)[:, None].astype(np.float64)
    live = (np.arange(kp) <= l_)[:, None]
    pos = np.arange(n)[None, :].astype(np.float64)
    ang = 2.0 * np.pi * k * pos / n
    flt = np.concatenate([np.where(live, np.cos(ang), 0.0), np.where(live, -np.sin(ang), 0.0)], axis=0)
    wk = np.where(np.arange(kp) <= l_, 2.0, 0.0)
    wk[0] = 1.0
    wk[l_] = 1.0
    psi = 2.0 * np.pi * np.outer(np.arange(l_), np.arange(kp)) / n
    inv = np.concatenate([np.cos(psi) * wk[None, :], -np.sin(psi) * wk[None, :]], axis=1) / n

    def hl(a):
        a32 = jnp.asarray(a, F32)
        hi = a32.astype(BF16)
        return jnp.stack([hi, (a32 - hi.astype(F32)).astype(BF16)], axis=0)

    return dict(fwd=hl(flt[:, 0:l_]), flt=hl(flt), inv=hl(inv), kp=kp)


def _short_filter_kernel(feat_ref, dec_ref, w1_ref, b1_ref, w2_ref, b2_ref, w3_ref, b3_ref, fq_ref, flt_ref,
                         out_ref, *, l_):
    kp = out_ref.shape[2]
    halves = []
    for part in range(2):
        z = feat_ref[part * l_:(part + 1) * l_, :]
        h = jnp.sin(fq_ref[0, 0:1, :] * (_dot_f32(z, w1_ref[0]) + b1_ref[0]))
        h = jnp.sin(fq_ref[0, 1:2, :] * (_dot_f32(h, w2_ref[0]) + b2_ref[0]))
        h = _dot_f32(h, w3_ref[0]) + b3_ref[0]
        halves.append(h[:, part * W_GROUP:(part + 1) * W_GROUP] * dec_ref[part * l_:(part + 1) * l_, :])
    spec = _dot_const(flt_ref[0], flt_ref[1], jnp.concatenate(halves, axis=0))
    out_ref[0, 0] = spec[0:kp]
    out_ref[0, 1] = spec[kp:2 * kp]


def _short_filter_spectra(l_, w1p, b1, w2, b2, w3, b3, freq, sc):
    kp = sc["kp"]
    feat, dec = _filter_tables(l_)
    full = lambda shape: pl.BlockSpec(shape, lambda l: (0,) * len(shape))
    per = lambda shape: pl.BlockSpec((1,) + shape, lambda l: (l,) + (0,) * len(shape))
    return pl.pallas_call(
        functools.partial(_short_filter_kernel, l_=l_),
        grid=(DEPTH,),
        in_specs=[full((2 * l_, 64)), full((2 * l_, W_GROUP)),
                  per((64, HYENA_FFN)), per((1, HYENA_FFN)), per((HYENA_FFN, HYENA_FFN)), per((1, HYENA_FFN)),
                  per((HYENA_FFN, 2 * W_GROUP)), per((1, 2 * W_GROUP)), per((2, HYENA_FFN)),
                  full((2, 2 * kp, 2 * l_))],
        out_specs=pl.BlockSpec((1, 2, kp, W_GROUP), lambda l: (l, 0, 0, 0)),
        out_shape=jax.ShapeDtypeStruct((DEPTH, 2, kp, W_GROUP), F32),
        compiler_params=pltpu.CompilerParams(vmem_limit_bytes=VMEM_LIMIT),
        name="hyena_filter_spectrum_%d" % l_,
    )(feat, dec, w1p, b1.reshape(DEPTH, 1, HYENA_FFN), w2, b2.reshape(DEPTH, 1, HYENA_FFN),
      w3, b3.reshape(DEPTH, 1, 2 * W_GROUP), freq, sc["flt"])


def _hyena_segment(v_ref, x1_ref, x0_ref, sg_ref, cw_ref, cb_ref, d_ref, kf_ref, consts, scratch, out_ref,
                   *, row0, seg_len, n1):
    f1h, f1l, gh, gl, mfh, mfl, mih, mil, tw_ref = consts
    zbuf, are, aim, ybuf = scratch
    n_sig = seg_len // FFT_N2
    n_in = max(n_sig, 8)
    n_out = max(n_sig, 8)
    k1p = _FftPlan(n1).k1p
    row_i = lax.broadcasted_iota(jnp.int32, (FFT_N2, W_GROUP), 0)

    def conv3(ref, blk, jslab):
        r = pl.multiple_of(row0 + jslab * FFT_N2, FFT_N2)
        cur = ref[0, pl.ds(r, FFT_N2), :]
        pe = ref[0, pl.ds(jnp.maximum(r - 1, row0), 1), :] * jnp.where(jslab > 0, 1.0, 0.0)
        ne = ref[0, pl.ds(jnp.minimum(r + FFT_N2, row0 + seg_len - 1), 1), :] * jnp.where(jslab < n_sig - 1, 1.0, 0.0)
        prev = jnp.where(row_i == 0, pe, pltpu.roll(cur, 1, 0))
        nxt = jnp.where(row_i == FFT_N2 - 1, ne, pltpu.roll(cur, FFT_N2 - 1, 0))
        c0 = blk * W_GROUP
        return (cw_ref[0:1, c0:c0 + W_GROUP] * prev + cw_ref[1:2, c0:c0 + W_GROUP] * cur
                + cw_ref[2:3, c0:c0 + W_GROUP] * nxt + cb_ref[:, c0:c0 + W_GROUP])

    def pre(jslab, carry):
        _store_slab(zbuf, jslab, conv3(v_ref, 0, jslab) * conv3(x1_ref, 1, jslab))
        return carry

    lax.fori_loop(0, n_sig, pre, 0)
    for jz in range(n_sig, n_in):
        _store_slab(zbuf, jz, jnp.zeros((FFT_N2, W_GROUP), F32))

    _cross_slab([zbuf], n_in, f1h, f1l, [are, aim], k1p, dot=_dot_const2)

    u_ = FFT_SLAB_UNROLL

    def slab(g, carry):
        cols = []
        for u in range(u_):
            k = g * u_ + u
            ar, ai = _twiddle(_load_slab(are, k), _load_slab(aim, k), tw_ref, k, conj=False)
            cols.append(jnp.concatenate([ar, ai], axis=0))
        x = _dot_const2(mfh, mfl, jnp.concatenate(cols, axis=1))
        cols = []
        for u in range(u_):
            xr = x[0:FFT_N2, u * W_GROUP:(u + 1) * W_GROUP]
            xi = x[FFT_N2:2 * FFT_N2, u * W_GROUP:(u + 1) * W_GROUP]
            r0 = pl.multiple_of((g * u_ + u) * FFT_N2, FFT_N2)
            kr = kf_ref[0, 0, pl.ds(r0, FFT_N2), :]
            ki = kf_ref[0, 1, pl.ds(r0, FFT_N2), :]
            cols.append(jnp.concatenate([xr * kr - xi * ki, xr * ki + xi * kr], axis=0))
        bm = _dot_const2(mih, mil, jnp.concatenate(cols, axis=1))
        for u in range(u_):
            k = g * u_ + u
            br, bi = _twiddle(bm[0:FFT_N2, u * W_GROUP:(u + 1) * W_GROUP],
                              bm[FFT_N2:2 * FFT_N2, u * W_GROUP:(u + 1) * W_GROUP], tw_ref, k, conj=True)
            _store_slab(are, k, br)
            _store_slab(aim, k, bi)
        return carry

    lax.fori_loop(0, k1p // u_, slab, 0)
    _cross_slab([are, aim], k1p, gh, gl, [ybuf], n_out, dot=_dot_const2)

    def post(jslab, carry):
        r = pl.multiple_of(row0 + jslab * FFT_N2, FFT_N2)
        y = _load_slab(ybuf, jslab) + _load_slab(zbuf, jslab) * d_ref[...]
        out_ref[0, pl.ds(r, FFT_N2), :] = (conv3(x0_ref, 2, jslab) * y
                                           * _silu(sg_ref[0, pl.ds(r, FFT_N2), :])).astype(out_ref.dtype)
        return carry

    lax.fori_loop(0, n_sig, post, 0)


def _hyena_short_segment(v_ref, x1_ref, x0_ref, sg_ref, cw_ref, cb_ref, d_ref, kf_ref, fwd_ref, inv_ref,
                         zbuf, ybuf, out_ref, *, seg_len):
    n_sig = seg_len // FFT_N2
    kp = kf_ref.shape[2]
    row_i = lax.broadcasted_iota(jnp.int32, (FFT_N2, W_GROUP), 0)

    def conv3(ref, blk, jslab):
        r = pl.multiple_of(jslab * FFT_N2, FFT_N2)
        cur = ref[0, pl.ds(r, FFT_N2), :]
        pe = ref[0, pl.ds(jnp.maximum(r - 1, 0), 1), :] * jnp.where(jslab > 0, 1.0, 0.0)
        ne = ref[0, pl.ds(jnp.minimum(r + FFT_N2, seg_len - 1), 1), :] * jnp.where(jslab < n_sig - 1, 1.0, 0.0)
        prev = jnp.where(row_i == 0, pe, pltpu.roll(cur, 1, 0))
        nxt = jnp.where(row_i == FFT_N2 - 1, ne, pltpu.roll(cur, FFT_N2 - 1, 0))
        c0 = blk * W_GROUP
        return (cw_ref[0:1, c0:c0 + W_GROUP] * prev + cw_ref[1:2, c0:c0 + W_GROUP] * cur
                + cw_ref[2:3, c0:c0 + W_GROUP] * nxt + cb_ref[:, c0:c0 + W_GROUP])

    def pre(jslab, carry):
        r = pl.multiple_of(jslab * FFT_N2, FFT_N2)
        zbuf[pl.ds(r, FFT_N2), :] = conv3(v_ref, 0, jslab) * conv3(x1_ref, 1, jslab)
        return carry

    lax.fori_loop(0, n_sig, pre, 0)
    x = _dot_const2(fwd_ref[0], fwd_ref[1], zbuf[...])
    xr, xi = x[0:kp], x[kp:2 * kp]
    kr, ki = kf_ref[0, 0], kf_ref[0, 1]
    y = jnp.concatenate([xr * kr - xi * ki, xr * ki + xi * kr], axis=0)
    ybuf[...] = _dot_const2(inv_ref[0], inv_ref[1], y)

    def post(jslab, carry):
        r = pl.multiple_of(jslab * FFT_N2, FFT_N2)
        y_ = ybuf[pl.ds(r, FFT_N2), :] + zbuf[pl.ds(r, FFT_N2), :] * d_ref[...]
        out_ref[0, pl.ds(r, FFT_N2), :] = (conv3(x0_ref, 2, jslab) * y_
                                           * _silu(sg_ref[0, pl.ds(r, FFT_N2), :])).astype(out_ref.dtype)
        return carry

    lax.fori_loop(0, n_sig, post, 0)


def _hyena_kernel(v_ref, x1_ref, x0_ref, sg_ref, cw_ref, cb_ref, d_ref, kfc_ref, kfl_ref,
                  c_fwd, c_inv, l_f1h, l_f1l, l_gh, l_gl, l_tw,
                  mfh_ref, mfl_ref, mih_ref, mil_ref, out_ref,
                  zc, yc, zl, arl, ail, yl, *, ctx_len, lat_len):
    mats = (mfh_ref[...], mfl_ref[...], mih_ref[...], mil_ref[...])
    _hyena_short_segment(v_ref, x1_ref, x0_ref, sg_ref, cw_ref, cb_ref, d_ref, kfc_ref, c_fwd, c_inv,
                         zc, yc, out_ref, seg_len=ctx_len)
    _hyena_segment(v_ref, x1_ref, x0_ref, sg_ref, cw_ref, cb_ref, d_ref,
                   kfl_ref, (l_f1h[...], l_f1l[...], l_gh[...], l_gl[...]) + mats + (l_tw,), (zl, arl, ail, yl), out_ref,
                   row0=ctx_len, seg_len=lat_len, n1=2 * lat_len // FFT_N2)


def _hyena(p, conv_w, conv_b, d, kf_ctx, kf_lat, layer, fc_ctx, fc_lat, ctx_len):
    b_, lt, _ = p.shape
    lat_len = lt - ctx_len
    n1l = 2 * lat_len // FFT_N2
    kl = _FftPlan(n1l).k1p
    blk = lambda j: pl.BlockSpec((1, lt, W_GROUP), lambda b: (b, 0, j))
    full = lambda a: pl.BlockSpec(a.shape, lambda b: (0,) * a.ndim)
    per_layer = lambda a: pl.BlockSpec((1,) + a.shape[1:], lambda b: (layer,) + (0,) * (a.ndim - 1))
    cst = [fc_ctx["fwd"], fc_ctx["inv"],
           fc_lat["f1_in"][0], fc_lat["f1_in"][1], fc_lat["g"][0], fc_lat["g"][1], fc_lat["tw"],
           fc_lat["m_fwd"][0], fc_lat["m_fwd"][1], fc_lat["m_inv"][0], fc_lat["m_inv"][1]]
    small = [conv_w, conv_b.reshape(1, 3 * W_GROUP), d.reshape(1, W_GROUP)]
    sbuf = lambda n: pltpu.VMEM((2, n * FFT_PITCH, 128), F32)
    kern = functools.partial(_hyena_kernel, ctx_len=ctx_len, lat_len=lat_len)
    return pl.pallas_call(
        kern,
        grid=(b_,),
        in_specs=([blk(HV), blk(HX1), blk(HX0), blk(HG)] + [full(a) for a in small]
                  + [per_layer(kf_ctx), per_layer(kf_lat)] + [full(a) for a in cst]),
        out_specs=pl.BlockSpec((1, lt, W_GROUP), lambda b: (b, 0, 0)),
        out_shape=jax.ShapeDtypeStruct((b_, lt, W_GROUP), BF16),
        scratch_shapes=[pltpu.VMEM((ctx_len, W_GROUP), F32), pltpu.VMEM((ctx_len, W_GROUP), F32),
                        sbuf(n1l // 2), sbuf(kl), sbuf(kl), sbuf(n1l // 2)],
        compiler_params=pltpu.CompilerParams(vmem_limit_bytes=VMEM_LIMIT),
        name="hyena_mixer",
    )(p, p, p, p, *small, kf_ctx, kf_lat, *cst)


def _rope_tables(ctx_len, lat_len):
    pos = np.arange(lat_len)
    row = (pos // GRID_W).astype(np.float32)
    col = (pos % GRID_W).astype(np.float32)
    inv = (np.float32(ROPE_BASE) ** (-np.arange(ROPE_AXIS_FREQS, dtype=np.float32) / np.float32(ROPE_AXIS_FREQS)))
    ang = np.concatenate([row[:, None] * inv, col[:, None] * inv], axis=-1).astype(np.float32)
    cos, sin = np.cos(ang), np.sin(ang)
    cos_h = np.concatenate([cos, cos], axis=-1)
    sin_h = np.concatenate([-sin, sin], axis=-1)
    cos_l = np.tile(cos_h, (1, N_HEADS))
    sin_l = np.tile(sin_h, (1, N_HEADS))
    cos_t = np.concatenate([np.ones((ctx_len, W_GROUP), np.float32), cos_l], axis=0)
    sin_t = np.concatenate([np.zeros((ctx_len, W_GROUP), np.float32), sin_l], axis=0)
    return jnp.asarray(cos_t, F32), jnp.asarray(sin_t, F32)


def _rope_partner():
    d = np.arange(W_GROUP)
    partner = np.where(d % HEAD_DIM < HEAD_DIM // 2, d + HEAD_DIM // 2, d - HEAD_DIM // 2)
    p = np.zeros((W_GROUP, W_GROUP), np.float32)
    p[partner, d] = 1.0
    return jnp.asarray(p, BF16)


def _mlstm_expansion():
    e = np.zeros((128, 3 * W_GROUP), np.float32)
    for j in range(3):
        for h in range(N_HEADS):
            e[8 * (j + 1) + h, j * W_GROUP + h * HEAD_DIM:j * W_GROUP + (h + 1) * HEAD_DIM] = 1.0
    return jnp.asarray(e, BF16)


def _tri(n):
    t = np.tril(np.ones((n, n), np.float32))
    return jnp.asarray(t, BF16), jnp.asarray(t.T, BF16)


def _w_layout_kernel(w_ref, o_ref, ot_ref):
    h0 = 5 * W_GROUP + 16
    g0 = h0 + 4 * W_GROUP
    d0 = g0 + 4 * W_GROUP + 2 * GLA_RANK
    w = w_ref[0]
    zero4 = jnp.zeros((8 - N_HEADS, w.shape[1]), F32)
    ot_ref[0] = jnp.concatenate(
        [piece for g in range(4)
         for piece in (w[5 * W_GROUP + N_HEADS * g:5 * W_GROUP + N_HEADS * (g + 1)], zero4)], axis=0).astype(BF16)
    o_ref[0] = jnp.concatenate([
        w[0:5 * W_GROUP],
        w[h0:h0 + 4 * W_GROUP],
        w[g0:g0 + 4 * W_GROUP],
        w[d0:d0 + 3 * W_GROUP],
        w[5 * W_GROUP:5 * W_GROUP + 16],
        w[g0 + 4 * W_GROUP:g0 + 4 * W_GROUP + 2 * GLA_RANK],
        jnp.zeros((128 - 48, w.shape[1]), F32)], axis=0).T.astype(BF16)


def _layout_w_in(w_in):
    n_in = w_in.shape[-1]
    return pl.pallas_call(
        _w_layout_kernel,
        grid=(DEPTH, D_MODEL // ROW_TILE),
        in_specs=[pl.BlockSpec((1, n_in, ROW_TILE), lambda l, i: (l, 0, i))],
        out_specs=[pl.BlockSpec((1, ROW_TILE, N_PROJ), lambda l, i: (l, i, 0)),
                   pl.BlockSpec((1, GATE_ROWS, ROW_TILE), lambda l, i: (l, 0, i))],
        out_shape=[jax.ShapeDtypeStruct((DEPTH, D_MODEL, N_PROJ), BF16),
                   jax.ShapeDtypeStruct((DEPTH, GATE_ROWS, D_MODEL), BF16)],
        compiler_params=pltpu.CompilerParams(vmem_limit_bytes=VMEM_LIMIT),
        name="inproj_weight_layout",
    )(jnp.swapaxes(w_in, 1, 2))


def kernel(x, c, ctx, c_ctx, w_ada, b_ada, g_pre, g_post, w_in, mlstm_gate_b, mlstm_norm_g, hyena_conv_w,
           hyena_conv_b, hyena_w1, hyena_b1, hyena_w2, hyena_b2, hyena_w3, hyena_b3, hyena_freq, hyena_d,
           gla_w_alpha, gla_b_alpha, gla_norm_g, attn_sink, w_out):
    b_, lat_len, _ = x.shape
    ctx_len = ctx.shape[1]
    n_ctx_tiles = ctx_len // ROW_TILE

    tril_m, triu_m = _tri(MLSTM_CHUNK)
    tril_g, triu_g = _tri(GLA_CHUNK)
    bd = np.kron(np.eye(N_HEADS, dtype=np.float32), np.ones((HEAD_DIM, HEAD_DIM), np.float32))
    rope_cos, rope_sin = _rope_tables(ctx_len, lat_len)
    mones_m = np.kron(np.eye(N_HEADS, dtype=np.float32), np.ones((MLSTM_CHUNK, HEAD_DIM), np.float32))
    consts = dict(tril_m=tril_m, triu_m=triu_m, tril_g=tril_g, triu_g=triu_g, bd=jnp.asarray(bd, BF16),
                  mones_m=jnp.asarray(mones_m, BF16), eall_m=_mlstm_expansion(),
                  rope_cos=rope_cos, rope_sin=rope_sin, pswap=_rope_partner())
    fc_ctx = _short_dft_constants(ctx_len)
    fc_lat = _fft_constants(2 * lat_len // FFT_N2, lat_len // FFT_N2, lat_len // FFT_N2)

    cc = jnp.zeros((16, D_MODEL), F32).at[0:b_].set(c).at[b_].set(c_ctx)
    mod = _modulation(cc, w_ada, b_ada)
    mod_ctx = jnp.broadcast_to(mod[:, b_:b_ + 1], (DEPTH, b_, 3 * D_MODEL))
    mod_all = jnp.stack([mod_ctx, mod[:, 0:b_]], axis=2).reshape(DEPTH, 2 * b_, 1, 3 * D_MODEL)

    w_p, w_st = _layout_w_in(w_in)
    w_o = w_out.astype(BF16)
    w1p = jnp.zeros((DEPTH, 64, HYENA_FFN), F32).at[:, 0:HYENA_EMB].set(hyena_w1)
    filt = (w1p, hyena_b1, hyena_w2, hyena_b2, hyena_w3, hyena_b3, hyena_freq)
    kf_ctx = _short_filter_spectra(ctx_len, *filt, fc_ctx)
    kf_lat = _filter_spectra(lat_len, *filt, fc_lat)

    xs = (ctx, x)
    _, p, gt = _proj(None, xs, None, 0, w_o, None, g_pre[0], mod_all, w_p, w_st, n_ctx_tiles)
    for l in range(DEPTH):
        a = _mlstm(p, gt, mlstm_gate_b[l], mlstm_norm_g[l], consts, ctx_len)
        hy = _hyena(p, hyena_conv_w[l], hyena_conv_b[l], hyena_d[l], kf_ctx, kf_lat, l, fc_ctx, fc_lat, ctx_len)
        g = _gla(p, gla_w_alpha[l], gla_b_alpha[l], gla_norm_g[l], consts, ctx_len)
        d = _attn(p, attn_sink[l], consts, ctx_len)
        if l < DEPTH - 1:
            xs, p, gt = _proj((a, hy, g, d), xs, l, l + 1, w_o, g_post[l], g_pre[l + 1], mod_all, w_p, w_st,
                              n_ctx_tiles)
    return _outproj((a, hy, g, d), w_o, DEPTH - 1, g_post[DEPTH - 1], mod_all, xs, n_ctx_tiles, skip_ctx=True)
```

```python
import functools
import math

import numpy as np
import jax
import jax.numpy as jnp
from jax import lax
from jax.experimental import pallas as pl
from jax.experimental.pallas import tpu as pltpu

F32 = jnp.float32
BF16 = jnp.bfloat16

D_MODEL = 1024
DEPTH = 4
GRID_W = 64
W_GROUP = 256
HEAD_DIM = 64
N_HEADS = 4
ATT_KV_HEADS = 2
WINDOW = 128
ATT_BLOCK = 128
GLA_RANK = 16
GLA_NORMALIZER = 16.0
HYENA_BANDS = 16
HYENA_EMB = 1 + 2 * HYENA_BANDS
HYENA_FFN = 64
HYENA_MIN_DECAY = math.log(1e-2) / 1.5
HYENA_MAX_DECAY = math.log(1e-2) / 0.3
ROPE_BASE = 10000.0
ROPE_AXIS_FREQS = HEAD_DIM // 4
EPS = 1e-6

N_WIDE_BLOCKS = 16
SMALL_COL0 = N_WIDE_BLOCKS * W_GROUP
N_PROJ = SMALL_COL0 + 128
SMALL_BLOCK = SMALL_COL0 // 128
(MQ, MK, MV, MO, MG, HV, HX1, HX0, HG, GQ, GK, GV, GG, AQ, AKV, AG) = range(N_WIDE_BLOCKS)

GATE_ROWS = 32
ROW_TILE = 256
MLSTM_CHUNK = 256
GLA_CHUNK = 128
ATT_STEP_BLOCKS = 6
FFT_N2 = 64
FFT_PITCH = 72
FFT_SLAB_UNROLL = 8
VMEM_LIMIT = 56 * 1024 * 1024


def _dot(a, b):
    return jnp.dot(a, b, preferred_element_type=F32)


def _dot_nt(a, b):
    return lax.dot_general(a, b, (((1,), (1,)), ((), ())), preferred_element_type=F32)


def _dot_tn(a, b):
    return lax.dot_general(a, b, (((0,), (0,)), ((), ())), preferred_element_type=F32)


def _split2(x):
    hi = x.astype(BF16)
    lo = (x - hi.astype(F32)).astype(BF16)
    return hi, lo


def _dot_const(ch, cl, x):
    xh, xl = _split2(x)
    return _dot(ch, xh) + _dot(ch, xl) + _dot(cl, xh)


def _dot_const2(ch, cl, x):
    xb = x.astype(BF16)
    return _dot(ch, xb) + _dot(cl, xb)


def _dot_f32(a, b):
    ah, al = _split2(a)
    bh, bl = _split2(b)
    return _dot(ah, bh) + _dot(ah, bl) + _dot(al, bh)


def _dot_f32_tn(a, b):
    ah, al = _split2(a)
    bh, bl = _split2(b)
    return _dot_tn(ah, bh) + _dot_tn(ah, bl) + _dot_tn(al, bh)


def _dot_exact_lhs(c, x):
    xh, xl = _split2(x)
    return _dot(c, xh) + _dot(c, xl)


def _log_sigmoid(x):
    return jnp.minimum(x, 0.0) - jnp.log(1.0 + jnp.exp(-jnp.abs(x)))


def _silu(x):
    return x * jax.nn.sigmoid(x)


def _head_masks():
    lane = lax.broadcasted_iota(jnp.int32, (1, W_GROUP), 1) // HEAD_DIM
    return [(lane == h).astype(F32) for h in range(N_HEADS)]


def _block_diag_mask():
    r = lax.broadcasted_iota(jnp.int32, (W_GROUP, W_GROUP), 0) // HEAD_DIM
    c = lax.broadcasted_iota(jnp.int32, (W_GROUP, W_GROUP), 1) // HEAD_DIM
    return (r == c).astype(F32)


def _spread(cols):
    lane = lax.broadcasted_iota(jnp.int32, (1, W_GROUP), 1)
    out = cols[N_HEADS - 1]
    for h in range(N_HEADS - 2, -1, -1):
        out = jnp.where(lane < (h + 1) * HEAD_DIM, cols[h], out)
    return out


def _head_rms_norm(h, bd_bf16, g_row):
    sq = h * h
    sh, sl = _split2(sq)
    ms = (_dot(sh, bd_bf16) + _dot(sl, bd_bf16)) * (1.0 / HEAD_DIM)
    return h * lax.rsqrt(ms + EPS) * g_row


def _mod_kernel(c_ref, w_ref, b_ref, o_ref):
    s = _silu(c_ref[...])
    o_ref[0] = _dot_f32(s, w_ref[0]) + b_ref[0]


def _modulation(cc, w_ada, b_ada):
    nblk = 3 * D_MODEL // 1024
    return pl.pallas_call(
        _mod_kernel,
        grid=(DEPTH, nblk),
        in_specs=[
            pl.BlockSpec((16, D_MODEL), lambda l, j: (0, 0)),
            pl.BlockSpec((1, D_MODEL, 1024), lambda l, j: (l, 0, j)),
            pl.BlockSpec((1, 1, 1024), lambda l, j: (l, 0, j)),
        ],
        out_specs=pl.BlockSpec((1, 16, 1024), lambda l, j: (l, 0, j)),
        out_shape=jax.ShapeDtypeStruct((DEPTH, 16, 3 * D_MODEL), F32),
        compiler_params=pltpu.CompilerParams(vmem_limit_bytes=VMEM_LIMIT),
        name="adaln_modulation",
    )(cc, w_ada, b_ada.reshape(DEPTH, 1, 3 * D_MODEL))


def _token_specs(xs, n_ctx_tiles, t0):
    shape = (1, ROW_TILE, D_MODEL)
    if not isinstance(xs, tuple):
        return [pl.BlockSpec(shape, lambda b, i: (b, i + t0, 0))], [xs]
    return [pl.BlockSpec(shape, lambda b, i: (b, jnp.minimum(i + t0, n_ctx_tiles - 1), 0)),
            pl.BlockSpec(shape, lambda b, i: (b, jnp.maximum(i + t0 - n_ctx_tiles, 0), 0))], list(xs)


def _token_tile(tok_refs, n_ctx_tiles, t0):
    if len(tok_refs) == 1:
        return tok_refs[0][0]
    return jnp.where(pl.program_id(1) + t0 < n_ctx_tiles, tok_refs[0][0], tok_refs[1][0])


def _stream_shape(xs):
    if isinstance(xs, tuple):
        return xs[1].shape[0], xs[0].shape[1] + xs[1].shape[1]
    return xs.shape[0], xs.shape[1]


PROJ_TILES = 2


def _proj_kernel(*refs, with_out, parts, n_ctx_tiles, nt):
    it = iter(refs)
    take = lambda n: [next(it) for _ in range(n)]
    if with_out:
        a_ref, h_ref, g_ref, d_ref, w_ref, gp_ref = take(6)
        mod_refs = take(PROJ_TILES)
    tok_refs = take(2 * PROJ_TILES if parts else 1)
    modn_refs = take(PROJ_TILES)
    gn_ref, wp_ref, wst_ref = take(3)
    if with_out:
        (xs_ref,) = take(1)
    p_ref, gt_ref = take(2)
    step = pl.program_id(0)

    def tile(u):
        rows = slice(u * ROW_TILE, (u + 1) * ROW_TILE)
        if parts:
            i_u = (PROJ_TILES * step + u) % nt
            x = jnp.where(i_u < n_ctx_tiles, tok_refs[2 * u][0], tok_refs[2 * u + 1][0])
        else:
            x = tok_refs[0][rows, :]
        if with_out:
            mix = jnp.concatenate([a_ref[rows, :], h_ref[rows, :], g_ref[rows, :], d_ref[rows, :]], axis=1)
            y = _dot(mix, w_ref[0])
            yield
            ms = jnp.mean(y * y, axis=-1, keepdims=True)
            x = x + mod_refs[u][0, 0, :, 2 * D_MODEL:3 * D_MODEL] * (y * lax.rsqrt(ms + EPS) * gp_ref[...])
            xs_ref[rows, :] = x
        ms = jnp.mean(x * x, axis=-1, keepdims=True)
        yy = x * lax.rsqrt(ms + EPS) * gn_ref[...]
        mod = modn_refs[u]
        hb = (yy * (1.0 + mod[0, 0, :, D_MODEL:2 * D_MODEL]) + mod[0, 0, :, 0:D_MODEL]).astype(BF16)
        yield
        p_ref[rows, :] = _dot(hb, wp_ref[0])
        gt_ref[u] = _dot_nt(wst_ref[0], hb)

    _interleave(*[tile(u) for u in range(PROJ_TILES)])


def _proj(mix, xs, layer_out, layer_in, w_o, g_post, g_pre, mod_all, w_p, w_st, n_ctx_tiles):
    b_, lt = _stream_shape(xs)
    nt = lt // ROW_TILE
    n_tiles = b_ * nt
    assert n_tiles % PROJ_TILES == 0
    step_rows = PROJ_TILES * ROW_TILE
    with_out, parts = mix is not None, isinstance(xs, tuple)
    b_of = lambda s, u: (PROJ_TILES * s + u) // nt
    i_of = lambda s, u: (PROJ_TILES * s + u) % nt
    flat = lambda width: pl.BlockSpec((step_rows, width), lambda s: (s, 0))
    mod_spec = lambda lyr, u: pl.BlockSpec(
        (1, 1, 1, 3 * D_MODEL), lambda s: (lyr, 2 * b_of(s, u) + jnp.where(i_of(s, u) >= n_ctx_tiles, 1, 0), 0, 0))
    in_specs, args = [], []
    if with_out:
        in_specs += [flat(W_GROUP)] * 4 + [pl.BlockSpec((1, D_MODEL, D_MODEL), lambda s: (layer_out, 0, 0)),
                                           pl.BlockSpec((1, D_MODEL), lambda s: (0, 0))]
        in_specs += [mod_spec(layer_out, u) for u in range(PROJ_TILES)]
        args += [m.reshape(b_ * lt, W_GROUP) for m in mix] + [w_o, g_post.reshape(1, D_MODEL)] + [mod_all] * PROJ_TILES
    if parts:
        for u in range(PROJ_TILES):
            in_specs += [pl.BlockSpec((1, ROW_TILE, D_MODEL),
                                      lambda s, u=u: (b_of(s, u), jnp.minimum(i_of(s, u), n_ctx_tiles - 1), 0)),
                         pl.BlockSpec((1, ROW_TILE, D_MODEL),
                                      lambda s, u=u: (b_of(s, u), jnp.maximum(i_of(s, u) - n_ctx_tiles, 0), 0))]
            args += list(xs)
    else:
        in_specs.append(flat(D_MODEL))
        args.append(xs.reshape(b_ * lt, D_MODEL))
    in_specs += [mod_spec(layer_in, u) for u in range(PROJ_TILES)]
    in_specs += [pl.BlockSpec((1, D_MODEL), lambda s: (0, 0)),
                 pl.BlockSpec((1, D_MODEL, N_PROJ), lambda s: (layer_in, 0, 0)),
                 pl.BlockSpec((1, GATE_ROWS, D_MODEL), lambda s: (layer_in, 0, 0))]
    args += [mod_all] * PROJ_TILES + [g_pre.reshape(1, D_MODEL), w_p, w_st]
    out_specs = [flat(N_PROJ), pl.BlockSpec((PROJ_TILES, GATE_ROWS, ROW_TILE), lambda s: (s, 0, 0))]
    out_shape = [jax.ShapeDtypeStruct((b_ * lt, N_PROJ), F32), jax.ShapeDtypeStruct((n_tiles, GATE_ROWS, ROW_TILE), F32)]
    if with_out:
        out_specs.insert(0, flat(D_MODEL))
        out_shape.insert(0, jax.ShapeDtypeStruct((b_ * lt, D_MODEL), F32))
    outs = pl.pallas_call(
        functools.partial(_proj_kernel, with_out=with_out, parts=parts, n_ctx_tiles=n_ctx_tiles, nt=nt),
        grid=(n_tiles // PROJ_TILES,),
        in_specs=in_specs,
        out_specs=out_specs,
        out_shape=out_shape,
        compiler_params=pltpu.CompilerParams(vmem_limit_bytes=VMEM_LIMIT),
        name="outproj_inproj" if with_out else "inproj",
    )(*args)
    xs_new = outs[0].reshape(b_, lt, D_MODEL) if with_out else None
    return xs_new, outs[-2].reshape(b_, lt, N_PROJ), outs[-1]


def _outproj_kernel(a_ref, h_ref, g_ref, d_ref, w_ref, gp_ref, mod_ref, *refs, n_tok, n_ctx_tiles, t0):
    o_ref = refs[n_tok]
    mix = jnp.concatenate([a_ref[0], h_ref[0], g_ref[0], d_ref[0]], axis=1)
    y = _dot(mix, w_ref[0])
    ms = jnp.mean(y * y, axis=-1, keepdims=True)
    yn = y * lax.rsqrt(ms + EPS) * gp_ref[...]
    gt = mod_ref[0, 0, :, 2 * D_MODEL:3 * D_MODEL]
    o_ref[0] = _token_tile(refs[:n_tok], n_ctx_tiles, t0) + gt * yn


def _outproj(mix, w_o, layer, g_post, mod_all, xs, n_ctx_tiles, skip_ctx):
    b_, lt = _stream_shape(xs)
    nt = lt // ROW_TILE
    t0 = n_ctx_tiles if skip_ctx else 0
    rows = lambda b, i: (b, i + t0, 0)
    tok_specs, tok_args = _token_specs(xs, n_ctx_tiles, t0)
    return pl.pallas_call(
        functools.partial(_outproj_kernel, n_tok=len(tok_args), n_ctx_tiles=n_ctx_tiles, t0=t0),
        grid=(b_, nt - t0),
        in_specs=[pl.BlockSpec((1, ROW_TILE, W_GROUP), rows)] * 4 + [
            pl.BlockSpec((1, D_MODEL, D_MODEL), lambda b, i: (layer, 0, 0)),
            pl.BlockSpec((1, D_MODEL), lambda b, i: (0, 0)),
            pl.BlockSpec((1, 1, 1, 3 * D_MODEL),
                         lambda b, i: (layer, 2 * b + jnp.where(i + t0 >= n_ctx_tiles, 1, 0), 0, 0)),
        ] + tok_specs,
        out_specs=pl.BlockSpec((1, ROW_TILE, D_MODEL), lambda b, i: (b, i, 0)),
        out_shape=jax.ShapeDtypeStruct((b_, lt - t0 * ROW_TILE, D_MODEL), F32),
        compiler_params=pltpu.CompilerParams(vmem_limit_bytes=VMEM_LIMIT),
        name="outproj",
    )(*mix, w_o, g_post.reshape(1, D_MODEL), mod_all, *tok_args)


def _interleave(*gens):
    live = list(gens)
    while live:
        live = [g for g in live if next(g, StopIteration) is not StopIteration]


def _scan_chunk(s, reverse, n_ctx, n_tot):
    if not reverse:
        return s
    return jnp.where(s < n_ctx, n_ctx - 1 - s, n_tot - 1 - (s - n_ctx))


def _mlstm_kernel(q_ref, k_ref, v_ref, o_ref, sg_ref, gt_ref, brow_ref, ng_ref,
                  bd_ref, mones_ref, eall_ref, out_ref, hf_ref, hb_ref, c_ref, n_ref, pt_ref, r8_ref, ab_ref,
                  *, chunk, n_ctx, n_tot):
    t_ = chunk
    masks = _head_masks()
    masks_bf = [m.astype(BF16) for m in masks]
    bdm = _block_diag_mask()
    row_i = lax.broadcasted_iota(jnp.int32, (t_, t_), 0)
    col_i = lax.broadcasted_iota(jnp.int32, (t_, t_), 1)
    lane_a = lax.broadcasted_iota(jnp.int32, (8 * n_tot, t_), 1)
    live = (lax.broadcasted_iota(jnp.int32, (8 * n_tot, t_), 0) % 8) < N_HEADS
    c_ref[...] = jnp.zeros_like(c_ref)
    n_ref[...] = jnp.zeros_like(n_ref)

    def scan(x, reverse, op):
        k = 1
        while k < t_:
            if reverse:
                x = jnp.where(lane_a < t_ - k, op(x, pltpu.roll(x, t_ - k, 1)), x)
            else:
                x = jnp.where(lane_a >= k, op(x, pltpu.roll(x, k, 1)), x)
            k *= 2
        return x

    def gates(reverse):
        d_ = 1 if reverse else 0
        end = 0 if reverse else t_ - 1
        order = list(range(n_tot))
        if reverse:
            order = order[:n_ctx][::-1] + order[n_ctx:][::-1]

        def rows_of(g):
            r_ = slice(8 * g, 8 * g + 8)
            return jnp.concatenate([gt_ref[c, r_, :] for c in range(n_tot)], axis=0) + jnp.concatenate(
                [brow_ref[r_, :]] * n_tot, axis=0)

        i_all = jnp.where(live, rows_of(2 * d_), 0.0)
        lf_all = jnp.where(live, _log_sigmoid(rows_of(2 * d_ + 1)), 0.0)
        b_all = scan(lf_all, reverse, jnp.add)
        r_all = i_all - b_all
        cm_all = scan(r_all, reverse, jnp.maximum)
        yield
        m_prev = jnp.zeros((8, 1), F32)
        for c in order:
            cols = slice(c * t_, (c + 1) * t_)
            blk8 = slice(8 * c, 8 * c + 8)
            i8, b8, r8 = i_all[blk8], b_all[blk8], r_all[blk8]
            inter = b8 + m_prev
            m_t = jnp.maximum(inter, b8 + cm_all[blk8])
            b_end = b8[:, end:end + 1]
            g8 = b_end - b8 + i8
            m_new = jnp.maximum(b_end + m_prev, jnp.max(g8, axis=1, keepdims=True))
            a_prev = jnp.exp(b_end + m_prev - m_new)
            packed = jnp.concatenate([b8 - m_t, jnp.exp(inter - m_t), jnp.exp(g8 - m_new), jnp.exp(-m_t),
                                      jnp.zeros((128 - 32, t_), F32)], axis=0)
            pt_ref[d_, cols, :] = packed.T
            r8_ref[d_, :, cols] = r8
            a_b = a_prev[0:1] * masks[0]
            for h in range(1, N_HEADS):
                a_b = a_b + a_prev[h:h + 1] * masks[h]
            ab_ref[d_, c] = a_b
            m_prev = m_new
            yield

    _interleave(gates(False), gates(True))

    def step(reverse, c_idx, h_ref):
        d_ = 1 if reverse else 0
        valid = (col_i >= row_i) if reverse else (col_i <= row_i)
        r0 = pl.multiple_of(c_idx * t_, t_)
        q = q_ref[0, pl.ds(r0, t_), :]
        k = k_ref[0, pl.ds(r0, t_), :] * (HEAD_DIM ** -0.5)
        qb = q.astype(BF16)
        kb = k.astype(BF16)
        vb = v_ref[0, pl.ds(r0, t_), :].astype(BF16)
        pt = pt_ref[d_, pl.ds(r0, t_), :]
        r8 = r8_ref[d_, :, pl.ds(r0, t_)]
        c_prev = c_ref[d_]
        n_prev = n_ref[d_]
        spread = _dot(pt.astype(BF16), eall_ref[...])
        s_cat = []
        for h in range(N_HEADS):
            w_intra = jnp.where(valid, jnp.exp(pt[:, h:h + 1] + r8[h:h + 1, :]), 0.0)
            s_cat.append((_dot_nt(qb * masks_bf[h], kb) * w_intra).astype(BF16))
        yield
        v_stack = jnp.concatenate([jnp.concatenate([vb * masks_bf[h] for h in range(N_HEADS)], axis=0),
                                   mones_ref[...]], axis=1)
        acc = _dot(jnp.concatenate(s_cat, axis=1), v_stack)
        qc = _dot(qb, c_prev.astype(BF16))
        qn = _dot((q * n_prev).astype(BF16), bd_ref[...])
        yield
        wi_b = spread[:, 0:W_GROUP]
        num = wi_b * qc + acc[:, 0:W_GROUP]
        den = wi_b * qn + acc[:, W_GROUP:2 * W_GROUP]
        h_ref[pl.ds(r0, t_), :] = num / jnp.maximum(jnp.abs(den), spread[:, 2 * W_GROUP:3 * W_GROUP])
        a_b = ab_ref[d_, c_idx]
        kw = k * spread[:, W_GROUP:2 * W_GROUP]
        c_ref[d_] = a_b * c_prev + bdm * _dot_tn(kw.astype(BF16), vb)
        n_ref[d_] = a_b * n_prev + jnp.sum(kw, axis=0, keepdims=True)

    def body(s, carry):
        _interleave(step(False, _scan_chunk(s, False, n_ctx, n_tot), hf_ref),
                    step(True, _scan_chunk(s, True, n_ctx, n_tot), hb_ref))
        return carry

    lax.fori_loop(0, n_tot, body, 0)

    def finish(c_idx, carry):
        r0 = pl.multiple_of(c_idx * t_, t_)
        hsum = hf_ref[pl.ds(r0, t_), :] + hb_ref[pl.ds(r0, t_), :]
        hsum = hsum * jax.nn.sigmoid(o_ref[0, pl.ds(r0, t_), :])
        hn = _head_rms_norm(hsum, bd_ref[...], ng_ref[...])
        out_ref[0, pl.ds(r0, t_), :] = (hn * _silu(sg_ref[0, pl.ds(r0, t_), :])).astype(out_ref.dtype)
        return carry

    lax.fori_loop(0, n_tot, finish, 0)


def _mlstm(p, gt, gate_b, norm_g, consts, ctx_len):
    b_, lt, _ = p.shape
    t_ = MLSTM_CHUNK
    assert t_ == ROW_TILE
    brow = jnp.pad(gate_b.reshape(4, N_HEADS), ((0, 0), (0, 8 - N_HEADS))).reshape(GATE_ROWS, 1)
    blk = lambda j: pl.BlockSpec((1, lt, W_GROUP), lambda b: (b, 0, j))
    full = lambda shape: pl.BlockSpec(shape, lambda b: (0,) * len(shape))
    kern = functools.partial(_mlstm_kernel, chunk=t_, n_ctx=ctx_len // t_, n_tot=lt // t_)
    return pl.pallas_call(
        kern,
        grid=(b_,),
        in_specs=[blk(MQ), blk(MK), blk(MV), blk(MO), blk(MG),
                  pl.BlockSpec((lt // t_, GATE_ROWS, t_), lambda b: (b, 0, 0)),
                  full((GATE_ROWS, 1)), full((1, W_GROUP)),
                  full((W_GROUP, W_GROUP)), full((N_HEADS * t_, W_GROUP)), full((128, 3 * W_GROUP))],
        out_specs=pl.BlockSpec((1, lt, W_GROUP), lambda b: (b, 0, 0)),
        out_shape=jax.ShapeDtypeStruct((b_, lt, W_GROUP), BF16),
        scratch_shapes=[pltpu.VMEM((lt, W_GROUP), F32), pltpu.VMEM((lt, W_GROUP), F32),
                        pltpu.VMEM((2, W_GROUP, W_GROUP), F32), pltpu.VMEM((2, 1, W_GROUP), F32),
                        pltpu.VMEM((2, lt, 128), F32), pltpu.VMEM((2, 8, lt), F32),
                        pltpu.VMEM((2, lt // t_, 1, W_GROUP), F32)],
        compiler_params=pltpu.CompilerParams(vmem_limit_bytes=VMEM_LIMIT),
        name="mlstm_mixer",
    )(p, p, p, p, p, gt, brow, norm_g.reshape(1, W_GROUP),
      consts["bd"], consts["mones_m"], consts["eall_m"])


def _gla_kernel(q_ref, k_ref, v_ref, sg_ref, sm_ref, wa_ref, ba_ref, ng_ref, tril_ref, triu_ref, bd_ref,
                out_ref, hf_ref, hb_ref, st_ref, qd_ref, kd_ref, kc_ref, eb_ref, *, chunk, n_ctx, n_tot):
    t_ = chunk
    masks_bf = [m.astype(BF16) for m in _head_masks()]
    bdm = _block_diag_mask()
    row_i = lax.broadcasted_iota(jnp.int32, (t_, t_), 0)
    col_i = lax.broadcasted_iota(jnp.int32, (t_, t_), 1)
    st_ref[...] = jnp.zeros_like(st_ref)
    wa_split = [_split2(wa_ref[d_]) for d_ in range(2)]

    def decay(reverse, c_idx):
        d_ = 1 if reverse else 0
        tri_c = triu_ref[...] if reverse else tril_ref[...]
        end_row = 0 if reverse else t_ - 1
        r0 = pl.multiple_of(c_idx * t_, t_)
        sm = sm_ref[0, pl.ds(r0, t_), :]
        smh, sml = _split2(sm)
        wah, wal = wa_split[d_]
        la = _log_sigmoid(_dot(smh, wah) + _dot(smh, wal) + _dot(sml, wah) + ba_ref[d_]) * (1.0 / GLA_NORMALIZER)
        yield
        bcum = _dot_exact_lhs(tri_c, la)
        yield
        q = q_ref[0, pl.ds(r0, t_), :] * (HEAD_DIM ** -0.5)
        k = k_ref[0, pl.ds(r0, t_), :]
        b_end = bcum[end_row:end_row + 1, :]
        qd_ref[d_, pl.ds(r0, t_), :] = (q * jnp.exp(bcum)).astype(BF16)
        kd_ref[d_, pl.ds(r0, t_), :] = (k * jnp.exp(-bcum)).astype(BF16)
        kc_ref[d_, pl.ds(r0, t_), :] = (k * jnp.exp(b_end - bcum)).astype(BF16)
        eb_ref[d_, c_idx] = jnp.exp(b_end)

    def decay_body(s, carry):
        _interleave(*[decay(rev, 2 * s + u) for u in range(2) for rev in (False, True)])
        return carry

    lax.fori_loop(0, n_tot // 2, decay_body, 0)

    def step(reverse, c_idx, h_ref):
        d_ = 1 if reverse else 0
        valid = (col_i >= row_i) if reverse else (col_i <= row_i)
        r0 = pl.multiple_of(c_idx * t_, t_)
        vb = v_ref[0, pl.ds(r0, t_), :].astype(BF16)
        qdb = qd_ref[d_, pl.ds(r0, t_), :]
        kd = kd_ref[d_, pl.ds(r0, t_), :]
        st_prev = st_ref[d_]
        att = [jnp.where(valid, _dot_nt(qdb * masks_bf[h], kd), 0.0).astype(BF16) for h in range(N_HEADS)]
        yield
        v_stack = jnp.concatenate([vb * masks_bf[h] for h in range(N_HEADS)], axis=0)
        h_ref[pl.ds(r0, t_), :] = (_dot_nt(qdb, st_prev.astype(BF16))
                                   + _dot(jnp.concatenate(att, axis=1), v_stack))
        st_ref[d_] = st_prev * eb_ref[d_, c_idx] + bdm * _dot_tn(vb, kc_ref[d_, pl.ds(r0, t_), :])

    def body(s, carry):
        _interleave(step(False, _scan_chunk(s, False, n_ctx, n_tot), hf_ref),
                    step(True, _scan_chunk(s, True, n_ctx, n_tot), hb_ref))
        return carry

    lax.fori_loop(0, n_tot, body, 0)

    def finish(i, carry):
        r0 = pl.multiple_of(i * ROW_TILE, ROW_TILE)
        hsum = hf_ref[pl.ds(r0, ROW_TILE), :] + hb_ref[pl.ds(r0, ROW_TILE), :]
        hn = _head_rms_norm(hsum, bd_ref[...], ng_ref[...])
        out_ref[0, pl.ds(r0, ROW_TILE), :] = (hn * _silu(sg_ref[0, pl.ds(r0, ROW_TILE), :])).astype(out_ref.dtype)
        return carry

    lax.fori_loop(0, n_tot * t_ // ROW_TILE, finish, 0)


def _gla(p, w_alpha, b_alpha, norm_g, consts, ctx_len):
    b_, lt, _ = p.shape
    t_ = GLA_CHUNK
    wa = jnp.zeros((2, 128, W_GROUP), F32)
    wa = wa.at[0, 16:32].set(w_alpha[0]).at[1, 32:48].set(w_alpha[1])
    blk = lambda j: pl.BlockSpec((1, lt, W_GROUP), lambda b: (b, 0, j))
    full = lambda shape: pl.BlockSpec(shape, lambda b: (0,) * len(shape))
    kern = functools.partial(_gla_kernel, chunk=t_, n_ctx=ctx_len // t_, n_tot=lt // t_)
    return pl.pallas_call(
        kern,
        grid=(b_,),
        in_specs=[blk(GQ), blk(GK), blk(GV), blk(GG),
                  pl.BlockSpec((1, lt, 128), lambda b: (b, 0, SMALL_BLOCK)),
                  full((2, 128, W_GROUP)), full((2, 1, W_GROUP)), full((1, W_GROUP)),
                  full((t_, t_)), full((t_, t_)), full((W_GROUP, W_GROUP))],
        out_specs=pl.BlockSpec((1, lt, W_GROUP), lambda b: (b, 0, 0)),
        out_shape=jax.ShapeDtypeStruct((b_, lt, W_GROUP), BF16),
        scratch_shapes=[pltpu.VMEM((lt, W_GROUP), F32), pltpu.VMEM((lt, W_GROUP), F32),
                        pltpu.VMEM((2, W_GROUP, W_GROUP), F32),
                        pltpu.VMEM((2, lt, W_GROUP), BF16), pltpu.VMEM((2, lt, W_GROUP), BF16),
                        pltpu.VMEM((2, lt, W_GROUP), BF16), pltpu.VMEM((2, lt // t_, 1, W_GROUP), F32)],
        compiler_params=pltpu.CompilerParams(vmem_limit_bytes=VMEM_LIMIT),
        name="gla_mixer",
    )(p, p, p, p, p, wa, b_alpha.reshape(2, 1, W_GROUP), norm_g.reshape(1, W_GROUP),
      consts["tril_g"], consts["triu_g"], consts["bd"])


def _attn_kernel(q_ref, kv_ref, sg_ref, cos_ref, sin_ref, sink_ref, pswap_ref, out_ref,
                 kb_ref, vb_ref, *, ctx_len, lat_len):
    g = pl.program_id(1)
    lt = ctx_len + lat_len
    n_ctx_blk = ctx_len // ATT_BLOCK
    nwin = 3 * ATT_BLOCK
    masks = _head_masks()
    log2e = math.log2(math.e)
    low_half = lax.broadcasted_iota(jnp.int32, (1, 2 * HEAD_DIM), 1) < HEAD_DIM

    def rope(x, r0, n):
        swapped = _dot(x.astype(BF16), pswap_ref[...])
        return x * cos_ref[pl.ds(r0, n), :] + swapped * sin_ref[pl.ds(r0, n), :]

    @pl.when(g == 0)
    def _():
        def fill(i, carry):
            r0 = pl.multiple_of(i * ROW_TILE, ROW_TILE)
            k = kv_ref[0, pl.ds(r0, ROW_TILE), 0:2 * HEAD_DIM]
            v = kv_ref[0, pl.ds(r0, ROW_TILE), 2 * HEAD_DIM:4 * HEAD_DIM]
            k_sw = pltpu.roll(k, HEAD_DIM, 1)
            v_sw = pltpu.roll(v, HEAD_DIM, 1)
            k4 = jnp.concatenate([jnp.where(low_half, k, k_sw), jnp.where(low_half, k_sw, k)], axis=1)
            kb_ref[pl.ds(r0, ROW_TILE), :] = rope(k4, r0, ROW_TILE).astype(BF16)
            vb_ref[pl.ds(r0, ROW_TILE), :] = jnp.concatenate(
                [jnp.where(low_half, v, 1.0), jnp.where(low_half, v_sw, 1.0)], axis=1).astype(BF16)
            return carry
        lax.fori_loop(0, lt // ROW_TILE, fill, 0)

    k_ctx = kb_ref[0:ctx_len, :]
    v_ctx = vb_ref[0:ctx_len, :]
    neg_inf = jnp.float32(-jnp.inf)
    rel0 = (lax.broadcasted_iota(jnp.int32, (ATT_BLOCK, nwin), 1)
            - lax.broadcasted_iota(jnp.int32, (ATT_BLOCK, nwin), 0))

    def block(u):
        j = g * ATT_STEP_BLOCKS + u
        rows = slice(u * ATT_BLOCK, (u + 1) * ATT_BLOCK)
        is_lat = j >= n_ctx_blk
        i_lat = jnp.maximum(j - n_ctx_blk, 0)
        w0 = pl.multiple_of(ctx_len + jnp.clip((i_lat - 1) * ATT_BLOCK, 0, lat_len - nwin), ATT_BLOCK)
        q0 = pl.multiple_of(j * ATT_BLOCK, ATT_BLOCK)
        q = rope(q_ref[0, rows, :], q0, ATT_BLOCK) * (HEAD_DIM ** -0.5 * log2e)
        k_win = kb_ref[pl.ds(w0, nwin), :]
        v_win = vb_ref[pl.ds(w0, nwin), :]
        band = is_lat & (jnp.abs(rel0 + (w0 - q0)) <= WINDOW)
        q_stack = jnp.concatenate([(q * masks[h]).astype(BF16) for h in range(N_HEADS)], axis=0)
        band4 = jnp.concatenate([band] * N_HEADS, axis=0)
        sink = jnp.concatenate([jnp.broadcast_to(sink_ref[h] * log2e, (ATT_BLOCK, 128)) for h in range(N_HEADS)],
                               axis=0)
        s_loc = jnp.where(band4, _dot_nt(q_stack, k_win), neg_inf)
        s_ctx = _dot_nt(q_stack, k_ctx)
        yield
        m = jnp.maximum(jnp.max(s_loc, axis=-1, keepdims=True), jnp.max(s_ctx, axis=-1, keepdims=True))
        m128 = jnp.maximum(jnp.broadcast_to(m, (N_HEADS * ATT_BLOCK, 128)), sink)
        p_loc = jnp.exp2(s_loc - jnp.concatenate([m128] * (nwin // 128), axis=1)).astype(BF16)
        p_ctx = jnp.exp2(s_ctx - jnp.concatenate([m128] * (ctx_len // 128), axis=1)).astype(BF16)
        yield
        o_all = _dot(p_loc, v_win) + _dot(p_ctx, v_ctx)
        e_sink = jnp.exp2(sink - m128)
        o = jnp.zeros((ATT_BLOCK, W_GROUP), F32)
        for h in range(N_HEADS):
            hr = slice(h * ATT_BLOCK, (h + 1) * ATT_BLOCK)
            if h % 2 == 0:
                num, den = o_all[hr], pltpu.roll(o_all[hr], W_GROUP - HEAD_DIM, 1)
            else:
                num, den = pltpu.roll(o_all[hr], HEAD_DIM, 1), o_all[hr]
            den = den + jnp.concatenate([e_sink[hr], e_sink[hr]], axis=1)
            o = jnp.where(masks[h] > 0.0, num / den, o)
        out_ref[0, rows, :] = (o * _silu(sg_ref[0, rows, :])).astype(out_ref.dtype)

    _interleave(*[block(u) for u in range(ATT_STEP_BLOCKS)])


def _attn(p, sink, consts, ctx_len):
    b_, lt, _ = p.shape
    step_rows = ATT_STEP_BLOCKS * ATT_BLOCK
    nb = lt // step_rows
    kern = functools.partial(_attn_kernel, ctx_len=ctx_len, lat_len=lt - ctx_len)
    return pl.pallas_call(
        kern,
        grid=(b_, nb),
        in_specs=[pl.BlockSpec((1, step_rows, W_GROUP), lambda b, j: (b, j, AQ)),
                  pl.BlockSpec((1, lt, W_GROUP), lambda b, j: (b, 0, AKV)),
                  pl.BlockSpec((1, step_rows, W_GROUP), lambda b, j: (b, j, AG)),
                  pl.BlockSpec((lt, W_GROUP), lambda b, j: (0, 0)),
                  pl.BlockSpec((lt, W_GROUP), lambda b, j: (0, 0)),
                  pl.BlockSpec((N_HEADS, 1, 1), lambda b, j: (0, 0, 0)),
                  pl.BlockSpec((W_GROUP, W_GROUP), lambda b, j: (0, 0))],
        out_specs=pl.BlockSpec((1, step_rows, W_GROUP), lambda b, j: (b, j, 0)),
        out_shape=jax.ShapeDtypeStruct((b_, lt, W_GROUP), BF16),
        scratch_shapes=[pltpu.VMEM((lt, W_GROUP), BF16), pltpu.VMEM((lt, W_GROUP), BF16)],
        compiler_params=pltpu.CompilerParams(vmem_limit_bytes=VMEM_LIMIT,
                                             dimension_semantics=("arbitrary", "arbitrary")),
        name="window_attention",
    )(p, p, p, consts["rope_cos"], consts["rope_sin"], sink.reshape(N_HEADS, 1, 1), consts["pswap"])


class _FftPlan:
    def __init__(self, n1):
        self.n1 = n1
        self.n = n1 * FFT_N2
        self.k1p = -(-(n1 // 2 + 1) // 8) * 8


def _fft_constants(n1, n_in, n_out):
    plan = _FftPlan(n1)
    n, k1p, half = plan.n, plan.k1p, n1 // 2
    k1 = np.arange(k1p)[:, None].astype(np.float64)
    live = (np.arange(k1p) <= half)[:, None]
    i1 = np.arange(n1)[None, :].astype(np.float64)
    ang = 2.0 * np.pi * k1 * i1 / n1
    f1 = np.concatenate([np.where(live, np.cos(ang), 0.0), np.where(live, -np.sin(ang), 0.0)], axis=0)
    i2 = np.arange(FFT_N2)[None, :].astype(np.float64)
    phi = 2.0 * np.pi * k1 * i2 / n
    tw = np.stack([np.cos(phi), -np.sin(phi)], axis=0)
    tw = np.broadcast_to(tw.reshape(2, k1p * FFT_N2, 1), (2, k1p * FFT_N2, 128))
    th = 2.0 * np.pi * np.outer(np.arange(FFT_N2), np.arange(FFT_N2)) / FFT_N2
    c2, s2 = np.cos(th), np.sin(th)
    m_fwd = np.block([[c2, s2], [-s2, c2]])
    m_inv = np.block([[c2, -s2], [s2, c2]])
    wk = np.where(np.arange(k1p) <= half, 2.0, 0.0)
    wk[0] = 1.0
    wk[half] = 1.0
    psi = 2.0 * np.pi * np.outer(np.arange(n1), np.arange(k1p)) / n1
    g = np.concatenate([np.cos(psi) * wk[None, :], -np.sin(psi) * wk[None, :]], axis=1) / n

    def hl(a):
        a32 = jnp.asarray(a, F32)
        hi = a32.astype(BF16)
        lo = (a32 - hi.astype(F32)).astype(BF16)
        return hi, lo

    return dict(f1=hl(f1), f1_in=hl(f1[:, 0:n_in]), g=hl(g[0:n_out]), m_fwd=hl(m_fwd), m_inv=hl(m_inv),
                tw=jnp.asarray(tw, F32))


def _slab_row(k):
    return k * FFT_PITCH if isinstance(k, int) else pl.multiple_of(k * FFT_PITCH, 8)


def _load_slab(ref, k):
    r0 = _slab_row(k)
    return jnp.concatenate([ref[0, pl.ds(r0, FFT_N2), :], ref[1, pl.ds(r0, FFT_N2), :]], axis=1)


def _store_slab(ref, k, val):
    r0 = _slab_row(k)
    ref[0, pl.ds(r0, FFT_N2), :] = val[:, 0:128]
    ref[1, pl.ds(r0, FFT_N2), :] = val[:, 128:256]


FFT_GROUP = 16


def _cross_slab(src_refs, n_src, mat_h, mat_l, dst_refs, n_dst, dot=_dot_const):
    def body(gidx, carry):
        cols = []
        for i in range(FFT_GROUP):
            i2 = gidx * FFT_GROUP + i
            parts = []
            for half in range(2):
                parts.append(jnp.concatenate(
                    [r[half, pl.ds(i2, n_src, stride=FFT_PITCH), :] for r in src_refs], axis=0))
            cols.append(jnp.concatenate(parts, axis=1))
        x = jnp.concatenate(cols, axis=1)
        y = dot(mat_h, mat_l, x)
        for i in range(FFT_GROUP):
            i2 = gidx * FFT_GROUP + i
            for half in range(2):
                c0 = (2 * i + half) * 128
                for d, ref in enumerate(dst_refs):
                    ref[half, pl.ds(i2, n_dst, stride=FFT_PITCH), :] = y[d * n_dst:(d + 1) * n_dst, c0:c0 + 128]
        return carry

    lax.fori_loop(0, FFT_N2 // FFT_GROUP, body, 0)


def _twiddle(ar, ai, tw_ref, k, conj):
    r0 = pl.multiple_of(k * FFT_N2, FFT_N2)
    tr = tw_ref[0, pl.ds(r0, FFT_N2), :]
    ti = tw_ref[1, pl.ds(r0, FFT_N2), :]
    tr = jnp.concatenate([tr, tr], axis=1)
    ti = jnp.concatenate([ti, ti], axis=1)
    if conj:
        return ar * tr + ai * ti, ai * tr - ar * ti
    return ar * tr - ai * ti, ar * ti + ai * tr


def _filter_kernel(feat_ref, dec_ref, w1_ref, b1_ref, w2_ref, b2_ref, w3_ref, b3_ref, fq_ref,
                   f1h_ref, f1l_ref, mh_ref, ml_ref, tw_ref, out_ref, zbuf, are, aim, *, n1, k1p):
    half = n1 // 2

    u_ = FFT_SLAB_UNROLL
    rows = u_ * FFT_N2

    def fill_block(g):
        r0 = pl.multiple_of(g * rows, rows)
        z = feat_ref[:, pl.ds(r0, rows)]
        h = jnp.sin(fq_ref[0, 0] * (_dot_f32_tn(w1_ref[0], z) + b1_ref[0]))
        yield
        h = jnp.sin(fq_ref[0, 1] * (_dot_f32_tn(w2_ref[0], h) + b2_ref[0]))
        yield
        h = _dot_f32_tn(h, w3_ref[0]) + b3_ref[0]
        hsel = jnp.where(g * u_ < half, h[:, 0:W_GROUP], h[:, W_GROUP:2 * W_GROUP])
        kern = hsel * dec_ref[pl.ds(r0, rows), :]
        for u in range(u_):
            _store_slab(zbuf, g * u_ + u, kern[u * FFT_N2:(u + 1) * FFT_N2])

    def fill(s, carry):
        _interleave(fill_block(2 * s), fill_block(2 * s + 1))
        return carry

    lax.fori_loop(0, n1 // u_ // 2, fill, 0)
    _cross_slab([zbuf], n1, f1h_ref[...], f1l_ref[...], [are, aim], k1p)

    def slab(g, carry):
        cols = []
        for u in range(u_):
            k = g * u_ + u
            ar, ai = _twiddle(_load_slab(are, k), _load_slab(aim, k), tw_ref, k, conj=False)
            cols.append(jnp.concatenate([ar, ai], axis=0))
        x = _dot_const(mh_ref[...], ml_ref[...], jnp.concatenate(cols, axis=1))
        for u in range(u_):
            r0 = pl.multiple_of((g * u_ + u) * FFT_N2, FFT_N2)
            out_ref[0, 0, pl.ds(r0, FFT_N2), :] = x[0:FFT_N2, u * W_GROUP:(u + 1) * W_GROUP]
            out_ref[0, 1, pl.ds(r0, FFT_N2), :] = x[FFT_N2:2 * FFT_N2, u * W_GROUP:(u + 1) * W_GROUP]
        return carry

    lax.fori_loop(0, k1p // u_, slab, 0)


def _filter_tables(l_):
    pos = np.concatenate([np.arange(l_), l_ - np.arange(l_)]).astype(np.float32)
    pos[l_] = 0.0
    t = pos / np.float32(max(l_ - 1, 1))
    w = np.float32(2.0 * math.pi) * pos / np.float32(l_)
    f = np.linspace(1e-4, HYENA_BANDS - 1, HYENA_BANDS, dtype=np.float32)
    feat = np.zeros((2 * l_, 64), np.float32)
    feat[:, 0] = t
    feat[:, 1:1 + HYENA_BANDS] = np.cos(w[:, None] * f)
    feat[:, 1 + HYENA_BANDS:HYENA_EMB] = -np.sin(w[:, None] * f)
    deltas = np.abs(np.linspace(HYENA_MIN_DECAY, HYENA_MAX_DECAY, W_GROUP, dtype=np.float32))
    dec = np.exp(-t[:, None] * deltas[None, :]).astype(np.float32)
    dec[l_] = 0.0
    return jnp.asarray(feat), jnp.asarray(dec)


def _filter_spectra(l_, w1p, b1, w2, b2, w3, b3, freq, fc):
    n1 = 2 * l_ // FFT_N2
    plan = _FftPlan(n1)
    k1p = plan.k1p
    feat, dec = _filter_tables(l_)
    full = lambda shape: pl.BlockSpec(shape, lambda l: (0,) * len(shape))
    per = lambda shape: pl.BlockSpec((1,) + shape, lambda l: (l,) + (0,) * len(shape))
    kern = functools.partial(_filter_kernel, n1=n1, k1p=k1p)
    return pl.pallas_call(
        kern,
        grid=(DEPTH,),
        in_specs=[full((64, 2 * l_)), full((2 * l_, W_GROUP)),
                  per((64, HYENA_FFN)), per((HYENA_FFN, 1)), per((HYENA_FFN, HYENA_FFN)), per((HYENA_FFN, 1)),
                  per((HYENA_FFN, 2 * W_GROUP)), per((1, 2 * W_GROUP)), per((2, HYENA_FFN, 1)),
                  full((2 * k1p, n1)), full((2 * k1p, n1)), full((128, 128)), full((128, 128)),
                  full((2, k1p * FFT_N2, 128))],
        out_specs=pl.BlockSpec((1, 2, k1p * FFT_N2, W_GROUP), lambda l: (l, 0, 0, 0)),
        out_shape=jax.ShapeDtypeStruct((DEPTH, 2, k1p * FFT_N2, W_GROUP), F32),
        scratch_shapes=[pltpu.VMEM((2, n1 * FFT_PITCH, 128), F32),
                        pltpu.VMEM((2, k1p * FFT_PITCH, 128), F32),
                        pltpu.VMEM((2, k1p * FFT_PITCH, 128), F32)],
        compiler_params=pltpu.CompilerParams(vmem_limit_bytes=VMEM_LIMIT),
        name="hyena_filter_spectrum_%d" % l_,
    )(feat.T, dec, w1p, b1.reshape(DEPTH, HYENA_FFN, 1), w2, b2.reshape(DEPTH, HYENA_FFN, 1),
      w3, b3.reshape(DEPTH, 1, 2 * W_GROUP), freq.reshape(DEPTH, 2, HYENA_FFN, 1),
      fc["f1"][0], fc["f1"][1], fc["m_fwd"][0], fc["m_fwd"][1], fc["tw"])


def _short_dft_constants(l_):
    n = 2 * l_
    kp = -(-(l_ + 1) // 8) * 8
    k = np.arange(kp)[:, None].astype(np.float64)
    live = (np.arange(kp) <= l_)[:, None]
    pos = np.arange(n)[None, :].astype(np.float64)
    ang = 2.0 * np.pi * k * pos / n
    flt = np.concatenate([np.where(live, np.cos(ang), 0.0), np.where(live, -np.sin(ang), 0.0)], axis=0)
    wk = np.where(np.arange(kp) <= l_, 2.0, 0.0)
    wk[0] = 1.0
    wk[l_] = 1.0
    psi = 2.0 * np.pi * np.outer(np.arange(l_), np.arange(kp)) / n
    inv = np.concatenate([np.cos(psi) * wk[None, :], -np.sin(psi) * wk[None, :]], axis=1) / n

    def hl(a):
        a32 = jnp.asarray(a, F32)
        hi = a32.astype(BF16)
        return jnp.stack([hi, (a32 - hi.astype(F32)).astype(BF16)], axis=0)

    return dict(fwd=hl(flt[:, 0:l_]), flt=hl(flt), inv=hl(inv), kp=kp)


def _short_filter_kernel(feat_ref, dec_ref, w1_ref, b1_ref, w2_ref, b2_ref, w3_ref, b3_ref, fq_ref, flt_ref,
                         out_ref, *, l_):
    kp = out_ref.shape[2]
    halves = []
    for part in range(2):
        z = feat_ref[part * l_:(part + 1) * l_, :]
        h = jnp.sin(fq_ref[0, 0:1, :] * (_dot_f32(z, w1_ref[0]) + b1_ref[0]))
        h = jnp.sin(fq_ref[0, 1:2, :] * (_dot_f32(h, w2_ref[0]) + b2_ref[0]))
        h = _dot_f32(h, w3_ref[0]) + b3_ref[0]
        halves.append(h[:, part * W_GROUP:(part + 1) * W_GROUP] * dec_ref[part * l_:(part + 1) * l_, :])
    spec = _dot_const(flt_ref[0], flt_ref[1], jnp.concatenate(halves, axis=0))
    out_ref[0, 0] = spec[0:kp]
    out_ref[0, 1] = spec[kp:2 * kp]


def _short_filter_spectra(l_, w1p, b1, w2, b2, w3, b3, freq, sc):
    kp = sc["kp"]
    feat, dec = _filter_tables(l_)
    full = lambda shape: pl.BlockSpec(shape, lambda l: (0,) * len(shape))
    per = lambda shape: pl.BlockSpec((1,) + shape, lambda l: (l,) + (0,) * len(shape))
    return pl.pallas_call(
        functools.partial(_short_filter_kernel, l_=l_),
        grid=(DEPTH,),
        in_specs=[full((2 * l_, 64)), full((2 * l_, W_GROUP)),
                  per((64, HYENA_FFN)), per((1, HYENA_FFN)), per((HYENA_FFN, HYENA_FFN)), per((1, HYENA_FFN)),
                  per((HYENA_FFN, 2 * W_GROUP)), per((1, 2 * W_GROUP)), per((2, HYENA_FFN)),
                  full((2, 2 * kp, 2 * l_))],
        out_specs=pl.BlockSpec((1, 2, kp, W_GROUP), lambda l: (l, 0, 0, 0)),
        out_shape=jax.ShapeDtypeStruct((DEPTH, 2, kp, W_GROUP), F32),
        compiler_params=pltpu.CompilerParams(vmem_limit_bytes=VMEM_LIMIT),
        name="hyena_filter_spectrum_%d" % l_,
    )(feat, dec, w1p, b1.reshape(DEPTH, 1, HYENA_FFN), w2, b2.reshape(DEPTH, 1, HYENA_FFN),
      w3, b3.reshape(DEPTH, 1, 2 * W_GROUP), freq, sc["flt"])


def _hyena_segment(v_ref, x1_ref, x0_ref, sg_ref, cw_ref, cb_ref, d_ref, kf_ref, consts, scratch, out_ref,
                   *, row0, seg_len, n1):
    f1h, f1l, gh, gl, mfh, mfl, mih, mil, tw_ref = consts
    zbuf, are, aim, ybuf = scratch
    n_sig = seg_len // FFT_N2
    n_in = max(n_sig, 8)
    n_out = max(n_sig, 8)
    k1p = _FftPlan(n1).k1p
    row_i = lax.broadcasted_iota(jnp.int32, (FFT_N2, W_GROUP), 0)

    def conv3(ref, blk, jslab):
        r = pl.multiple_of(row0 + jslab * FFT_N2, FFT_N2)
        cur = ref[0, pl.ds(r, FFT_N2), :]
        pe = ref[0, pl.ds(jnp.maximum(r - 1, row0), 1), :] * jnp.where(jslab > 0, 1.0, 0.0)
        ne = ref[0, pl.ds(jnp.minimum(r + FFT_N2, row0 + seg_len - 1), 1), :] * jnp.where(jslab < n_sig - 1, 1.0, 0.0)
        prev = jnp.where(row_i == 0, pe, pltpu.roll(cur, 1, 0))
        nxt = jnp.where(row_i == FFT_N2 - 1, ne, pltpu.roll(cur, FFT_N2 - 1, 0))
        c0 = blk * W_GROUP
        return (cw_ref[0:1, c0:c0 + W_GROUP] * prev + cw_ref[1:2, c0:c0 + W_GROUP] * cur
                + cw_ref[2:3, c0:c0 + W_GROUP] * nxt + cb_ref[:, c0:c0 + W_GROUP])

    def pre(jslab, carry):
        _store_slab(zbuf, jslab, conv3(v_ref, 0, jslab) * conv3(x1_ref, 1, jslab))
        return carry

    lax.fori_loop(0, n_sig, pre, 0)
    for jz in range(n_sig, n_in):
        _store_slab(zbuf, jz, jnp.zeros((FFT_N2, W_GROUP), F32))

    _cross_slab([zbuf], n_in, f1h, f1l, [are, aim], k1p, dot=_dot_const2)

    u_ = FFT_SLAB_UNROLL

    def slab(g, carry):
        cols = []
        for u in range(u_):
            k = g * u_ + u
            ar, ai = _twiddle(_load_slab(are, k), _load_slab(aim, k), tw_ref, k, conj=False)
            cols.append(jnp.concatenate([ar, ai], axis=0))
        x = _dot_const2(mfh, mfl, jnp.concatenate(cols, axis=1))
        cols = []
        for u in range(u_):
            xr = x[0:FFT_N2, u * W_GROUP:(u + 1) * W_GROUP]
            xi = x[FFT_N2:2 * FFT_N2, u * W_GROUP:(u + 1) * W_GROUP]
            r0 = pl.multiple_of((g * u_ + u) * FFT_N2, FFT_N2)
            kr = kf_ref[0, 0, pl.ds(r0, FFT_N2), :]
            ki = kf_ref[0, 1, pl.ds(r0, FFT_N2), :]
            cols.append(jnp.concatenate([xr * kr - xi * ki, xr * ki + xi * kr], axis=0))
        bm = _dot_const2(mih, mil, jnp.concatenate(cols, axis=1))
        for u in range(u_):
            k = g * u_ + u
            br, bi = _twiddle(bm[0:FFT_N2, u * W_GROUP:(u + 1) * W_GROUP],
                              bm[FFT_N2:2 * FFT_N2, u * W_GROUP:(u + 1) * W_GROUP], tw_ref, k, conj=True)
            _store_slab(are, k, br)
            _store_slab(aim, k, bi)
        return carry

    lax.fori_loop(0, k1p // u_, slab, 0)
    _cross_slab([are, aim], k1p, gh, gl, [ybuf], n_out, dot=_dot_const2)

    def post(jslab, carry):
        r = pl.multiple_of(row0 + jslab * FFT_N2, FFT_N2)
        y = _load_slab(ybuf, jslab) + _load_slab(zbuf, jslab) * d_ref[...]
        out_ref[0, pl.ds(r, FFT_N2), :] = (conv3(x0_ref, 2, jslab) * y
                                           * _silu(sg_ref[0, pl.ds(r, FFT_N2), :])).astype(out_ref.dtype)
        return carry

    lax.fori_loop(0, n_sig, post, 0)


def _hyena_short_segment(v_ref, x1_ref, x0_ref, sg_ref, cw_ref, cb_ref, d_ref, kf_ref, fwd_ref, inv_ref,
                         zbuf, ybuf, out_ref, *, seg_len):
    n_sig = seg_len // FFT_N2
    kp = kf_ref.shape[2]
    row_i = lax.broadcasted_iota(jnp.int32, (FFT_N2, W_GROUP), 0)

    def conv3(ref, blk, jslab):
        r = pl.multiple_of(jslab * FFT_N2, FFT_N2)
        cur = ref[0, pl.ds(r, FFT_N2), :]
        pe = ref[0, pl.ds(jnp.maximum(r - 1, 0), 1), :] * jnp.where(jslab > 0, 1.0, 0.0)
        ne = ref[0, pl.ds(jnp.minimum(r + FFT_N2, seg_len - 1), 1), :] * jnp.where(jslab < n_sig - 1, 1.0, 0.0)
        prev = jnp.where(row_i == 0, pe, pltpu.roll(cur, 1, 0))
        nxt = jnp.where(row_i == FFT_N2 - 1, ne, pltpu.roll(cur, FFT_N2 - 1, 0))
        c0 = blk * W_GROUP
        return (cw_ref[0:1, c0:c0 + W_GROUP] * prev + cw_ref[1:2, c0:c0 + W_GROUP] * cur
                + cw_ref[2:3, c0:c0 + W_GROUP] * nxt + cb_ref[:, c0:c0 + W_GROUP])

    def pre(jslab, carry):
        r = pl.multiple_of(jslab * FFT_N2, FFT_N2)
        zbuf[pl.ds(r, FFT_N2), :] = conv3(v_ref, 0, jslab) * conv3(x1_ref, 1, jslab)
        return carry

    lax.fori_loop(0, n_sig, pre, 0)
    x = _dot_const2(fwd_ref[0], fwd_ref[1], zbuf[...])
    xr, xi = x[0:kp], x[kp:2 * kp]
    kr, ki = kf_ref[0, 0], kf_ref[0, 1]
    y = jnp.concatenate([xr * kr - xi * ki, xr * ki + xi * kr], axis=0)
    ybuf[...] = _dot_const2(inv_ref[0], inv_ref[1], y)

    def post(jslab, carry):
        r = pl.multiple_of(jslab * FFT_N2, FFT_N2)
        y_ = ybuf[pl.ds(r, FFT_N2), :] + zbuf[pl.ds(r, FFT_N2), :] * d_ref[...]
        out_ref[0, pl.ds(r, FFT_N2), :] = (conv3(x0_ref, 2, jslab) * y_
                                           * _silu(sg_ref[0, pl.ds(r, FFT_N2), :])).astype(out_ref.dtype)
        return carry

    lax.fori_loop(0, n_sig, post, 0)


def _hyena_kernel(v_ref, x1_ref, x0_ref, sg_ref, cw_ref, cb_ref, d_ref, kfc_ref, kfl_ref,
                  c_fwd, c_inv, l_f1h, l_f1l, l_gh, l_gl, l_tw,
                  mfh_ref, mfl_ref, mih_ref, mil_ref, out_ref,
                  zc, yc, zl, arl, ail, yl, *, ctx_len, lat_len):
    mats = (mfh_ref[...], mfl_ref[...], mih_ref[...], mil_ref[...])
    _hyena_short_segment(v_ref, x1_ref, x0_ref, sg_ref, cw_ref, cb_ref, d_ref, kfc_ref, c_fwd, c_inv,
                         zc, yc, out_ref, seg_len=ctx_len)
    _hyena_segment(v_ref, x1_ref, x0_ref, sg_ref, cw_ref, cb_ref, d_ref,
                   kfl_ref, (l_f1h[...], l_f1l[...], l_gh[...], l_gl[...]) + mats + (l_tw,), (zl, arl, ail, yl), out_ref,
                   row0=ctx_len, seg_len=lat_len, n1=2 * lat_len // FFT_N2)


def _hyena(p, conv_w, conv_b, d, kf_ctx, kf_lat, layer, fc_ctx, fc_lat, ctx_len):
    b_, lt, _ = p.shape
    lat_len = lt - ctx_len
    n1l = 2 * lat_len // FFT_N2
    kl = _FftPlan(n1l).k1p
    blk = lambda j: pl.BlockSpec((1, lt, W_GROUP), lambda b: (b, 0, j))
    full = lambda a: pl.BlockSpec(a.shape, lambda b: (0,) * a.ndim)
    per_layer = lambda a: pl.BlockSpec((1,) + a.shape[1:], lambda b: (layer,) + (0,) * (a.ndim - 1))
    cst = [fc_ctx["fwd"], fc_ctx["inv"],
           fc_lat["f1_in"][0], fc_lat["f1_in"][1], fc_lat["g"][0], fc_lat["g"][1], fc_lat["tw"],
           fc_lat["m_fwd"][0], fc_lat["m_fwd"][1], fc_lat["m_inv"][0], fc_lat["m_inv"][1]]
    small = [conv_w, conv_b.reshape(1, 3 * W_GROUP), d.reshape(1, W_GROUP)]
    sbuf = lambda n: pltpu.VMEM((2, n * FFT_PITCH, 128), F32)
    kern = functools.partial(_hyena_kernel, ctx_len=ctx_len, lat_len=lat_len)
    return pl.pallas_call(
        kern,
        grid=(b_,),
        in_specs=([blk(HV), blk(HX1), blk(HX0), blk(HG)] + [full(a) for a in small]
                  + [per_layer(kf_ctx), per_layer(kf_lat)] + [full(a) for a in cst]),
        out_specs=pl.BlockSpec((1, lt, W_GROUP), lambda b: (b, 0, 0)),
        out_shape=jax.ShapeDtypeStruct((b_, lt, W_GROUP), BF16),
        scratch_shapes=[pltpu.VMEM((ctx_len, W_GROUP), F32), pltpu.VMEM((ctx_len, W_GROUP), F32),
                        sbuf(n1l // 2), sbuf(kl), sbuf(kl), sbuf(n1l // 2)],
        compiler_params=pltpu.CompilerParams(vmem_limit_bytes=VMEM_LIMIT),
        name="hyena_mixer",
    )(p, p, p, p, *small, kf_ctx, kf_lat, *cst)


def _rope_tables(ctx_len, lat_len):
    pos = np.arange(lat_len)
    row = (pos // GRID_W).astype(np.float32)
    col = (pos % GRID_W).astype(np.float32)
    inv = (np.float32(ROPE_BASE) ** (-np.arange(ROPE_AXIS_FREQS, dtype=np.float32) / np.float32(ROPE_AXIS_FREQS)))
    ang = np.concatenate([row[:, None] * inv, col[:, None] * inv], axis=-1).astype(np.float32)
    cos, sin = np.cos(ang), np.sin(ang)
    cos_h = np.concatenate([cos, cos], axis=-1)
    sin_h = np.concatenate([-sin, sin], axis=-1)
    cos_l = np.tile(cos_h, (1, N_HEADS))
    sin_l = np.tile(sin_h, (1, N_HEADS))
    cos_t = np.concatenate([np.ones((ctx_len, W_GROUP), np.float32), cos_l], axis=0)
    sin_t = np.concatenate([np.zeros((ctx_len, W_GROUP), np.float32), sin_l], axis=0)
    return jnp.asarray(cos_t, F32), jnp.asarray(sin_t, F32)


def _rope_partner():
    d = np.arange(W_GROUP)
    partner = np.where(d % HEAD_DIM < HEAD_DIM // 2, d + HEAD_DIM // 2, d - HEAD_DIM // 2)
    p = np.zeros((W_GROUP, W_GROUP), np.float32)
    p[partner, d] = 1.0
    return jnp.asarray(p, BF16)


def _mlstm_expansion():
    e = np.zeros((128, 3 * W_GROUP), np.float32)
    for j in range(3):
        for h in range(N_HEADS):
            e[8 * (j + 1) + h, j * W_GROUP + h * HEAD_DIM:j * W_GROUP + (h + 1) * HEAD_DIM] = 1.0
    return jnp.asarray(e, BF16)


def _tri(n):
    t = np.tril(np.ones((n, n), np.float32))
    return jnp.asarray(t, BF16), jnp.asarray(t.T, BF16)


def _w_layout_kernel(w_ref, o_ref, ot_ref):
    h0 = 5 * W_GROUP + 16
    g0 = h0 + 4 * W_GROUP
    d0 = g0 + 4 * W_GROUP + 2 * GLA_RANK
    w = w_ref[0]
    zero4 = jnp.zeros((8 - N_HEADS, w.shape[1]), F32)
    ot_ref[0] = jnp.concatenate(
        [piece for g in range(4)
         for piece in (w[5 * W_GROUP + N_HEADS * g:5 * W_GROUP + N_HEADS * (g + 1)], zero4)], axis=0).astype(BF16)
    o_ref[0] = jnp.concatenate([
        w[0:5 * W_GROUP],
        w[h0:h0 + 4 * W_GROUP],
        w[g0:g0 + 4 * W_GROUP],
        w[d0:d0 + 3 * W_GROUP],
        w[5 * W_GROUP:5 * W_GROUP + 16],
        w[g0 + 4 * W_GROUP:g0 + 4 * W_GROUP + 2 * GLA_RANK],
        jnp.zeros((128 - 48, w.shape[1]), F32)], axis=0).T.astype(BF16)


def _layout_w_in(w_in):
    n_in = w_in.shape[-1]
    return pl.pallas_call(
        _w_layout_kernel,
        grid=(DEPTH, D_MODEL // ROW_TILE),
        in_specs=[pl.BlockSpec((1, n_in, ROW_TILE), lambda l, i: (l, 0, i))],
        out_specs=[pl.BlockSpec((1, ROW_TILE, N_PROJ), lambda l, i: (l, i, 0)),
                   pl.BlockSpec((1, GATE_ROWS, ROW_TILE), lambda l, i: (l, 0, i))],
        out_shape=[jax.ShapeDtypeStruct((DEPTH, D_MODEL, N_PROJ), BF16),
                   jax.ShapeDtypeStruct((DEPTH, GATE_ROWS, D_MODEL), BF16)],
        compiler_params=pltpu.CompilerParams(vmem_limit_bytes=VMEM_LIMIT),
        name="inproj_weight_layout",
    )(jnp.swapaxes(w_in, 1, 2))


def kernel(x, c, ctx, c_ctx, w_ada, b_ada, g_pre, g_post, w_in, mlstm_gate_b, mlstm_norm_g, hyena_conv_w,
           hyena_conv_b, hyena_w1, hyena_b1, hyena_w2, hyena_b2, hyena_w3, hyena_b3, hyena_freq, hyena_d,
           gla_w_alpha, gla_b_alpha, gla_norm_g, attn_sink, w_out):
    b_, lat_len, _ = x.shape
    ctx_len = ctx.shape[1]
    n_ctx_tiles = ctx_len // ROW_TILE

    tril_m, triu_m = _tri(MLSTM_CHUNK)
    tril_g, triu_g = _tri(GLA_CHUNK)
    bd = np.kron(np.eye(N_HEADS, dtype=np.float32), np.ones((HEAD_DIM, HEAD_DIM), np.float32))
    rope_cos, rope_sin = _rope_tables(ctx_len, lat_len)
    mones_m = np.kron(np.eye(N_HEADS, dtype=np.float32), np.ones((MLSTM_CHUNK, HEAD_DIM), np.float32))
    consts = dict(tril_m=tril_m, triu_m=triu_m, tril_g=tril_g, triu_g=triu_g, bd=jnp.asarray(bd, BF16),
                  mones_m=jnp.asarray(mones_m, BF16), eall_m=_mlstm_expansion(),
                  rope_cos=rope_cos, rope_sin=rope_sin, pswap=_rope_partner())
    fc_ctx = _short_dft_constants(ctx_len)
    fc_lat = _fft_constants(2 * lat_len // FFT_N2, lat_len // FFT_N2, lat_len // FFT_N2)

    cc = jnp.zeros((16, D_MODEL), F32).at[0:b_].set(c).at[b_].set(c_ctx)
    mod = _modulation(cc, w_ada, b_ada)
    mod_ctx = jnp.broadcast_to(mod[:, b_:b_ + 1], (DEPTH, b_, 3 * D_MODEL))
    mod_all = jnp.stack([mod_ctx, mod[:, 0:b_]], axis=2).reshape(DEPTH, 2 * b_, 1, 3 * D_MODEL)

    w_p, w_st = _layout_w_in(w_in)
    w_o = w_out.astype(BF16)
    w1p = jnp.zeros((DEPTH, 64, HYENA_FFN), F32).at[:, 0:HYENA_EMB].set(hyena_w1)
    filt = (w1p, hyena_b1, hyena_w2, hyena_b2, hyena_w3, hyena_b3, hyena_freq)
    kf_ctx = _short_filter_spectra(ctx_len, *filt, fc_ctx)
    kf_lat = _filter_spectra(lat_len, *filt, fc_lat)

    xs = (ctx, x)
    _, p, gt = _proj(None, xs, None, 0, w_o, None, g_pre[0], mod_all, w_p, w_st, n_ctx_tiles)
    for l in range(DEPTH):
        a = _mlstm(p, gt, mlstm_gate_b[l], mlstm_norm_g[l], consts, ctx_len)
        hy = _hyena(p, hyena_conv_w[l], hyena_conv_b[l], hyena_d[l], kf_ctx, kf_lat, l, fc_ctx, fc_lat, ctx_len)
        g = _gla(p, gla_w_alpha[l], gla_b_alpha[l], gla_norm_g[l], consts, ctx_len)
        d = _attn(p, attn_sink[l], consts, ctx_len)
        if l < DEPTH - 1:
            xs, p, gt = _proj((a, hy, g, d), xs, l, l + 1, w_o, g_post[l], g_pre[l + 1], mod_all, w_p, w_st,
                              n_ctx_tiles)
    return _outproj((a, hy, g, d), w_o, DEPTH - 1, g_post[DEPTH - 1], mod_all, xs, n_ctx_tiles, skip_ctx=True)
```

```python
import functools
import math

import numpy as np
import jax
import jax.numpy as jnp
from jax import lax
from jax.experimental import pallas as pl
from jax.experimental.pallas import tpu as pltpu

F32 = jnp.float32
BF16 = jnp.bfloat16

D_MODEL = 1024
DEPTH = 4
GRID_W = 64
W_GROUP = 256
HEAD_DIM = 64
N_HEADS = 4
ATT_KV_HEADS = 2
WINDOW = 128
ATT_BLOCK = 128
GLA_RANK = 16
GLA_NORMALIZER = 16.0
HYENA_BANDS = 16
HYENA_EMB = 1 + 2 * HYENA_BANDS
HYENA_FFN = 64
HYENA_MIN_DECAY = math.log(1e-2) / 1.5
HYENA_MAX_DECAY = math.log(1e-2) / 0.3
ROPE_BASE = 10000.0
ROPE_AXIS_FREQS = HEAD_DIM // 4
EPS = 1e-6

N_WIDE_BLOCKS = 16
SMALL_COL0 = N_WIDE_BLOCKS * W_GROUP
N_PROJ = SMALL_COL0 + 128
SMALL_BLOCK = SMALL_COL0 // 128
(MQ, MK, MV, MO, MG, HV, HX1, HX0, HG, GQ, GK, GV, GG, AQ, AKV, AG) = range(N_WIDE_BLOCKS)

GATE_ROWS = 32
ROW_TILE = 256
MLSTM_CHUNK = 256
MLSTM_SCAN_UNROLL = 3
GLA_CHUNK = 128
GLA_SCAN_UNROLL = 3
ATT_STEP_BLOCKS = 6
FFT_N2 = 64
FFT_PITCH = 72
FFT_SLAB_UNROLL = 8
VMEM_LIMIT = 56 * 1024 * 1024


def _dot(a, b):
    return jnp.dot(a, b, preferred_element_type=F32)


def _dot_nt(a, b):
    return lax.dot_general(a, b, (((1,), (1,)), ((), ())), preferred_element_type=F32)


def _dot_tn(a, b):
    return lax.dot_general(a, b, (((0,), (0,)), ((), ())), preferred_element_type=F32)


def _split2(x):
    hi = x.astype(BF16)
    lo = (x - hi.astype(F32)).astype(BF16)
    return hi, lo


def _dot_const(ch, cl, x):
    xh, xl = _split2(x)
    return _dot(ch, xh) + _dot(ch, xl) + _dot(cl, xh)


def _dot_const2(ch, cl, x):
    xb = x.astype(BF16)
    return _dot(ch, xb) + _dot(cl, xb)


def _dot_f32(a, b):
    ah, al = _split2(a)
    bh, bl = _split2(b)
    return _dot(ah, bh) + _dot(ah, bl) + _dot(al, bh)


def _dot_f32_tn(a, b):
    ah, al = _split2(a)
    bh, bl = _split2(b)
    return _dot_tn(ah, bh) + _dot_tn(ah, bl) + _dot_tn(al, bh)


def _dot_exact_lhs(c, x):
    xh, xl = _split2(x)
    return _dot(c, xh) + _dot(c, xl)


def _log_sigmoid(x):
    return jnp.minimum(x, 0.0) - jnp.log(1.0 + jnp.exp(-jnp.abs(x)))


def _silu(x):
    return x * jax.nn.sigmoid(x)


def _head_masks():
    lane = lax.broadcasted_iota(jnp.int32, (1, W_GROUP), 1) // HEAD_DIM
    return [(lane == h).astype(F32) for h in range(N_HEADS)]


def _block_diag_mask():
    r = lax.broadcasted_iota(jnp.int32, (W_GROUP, W_GROUP), 0) // HEAD_DIM
    c = lax.broadcasted_iota(jnp.int32, (W_GROUP, W_GROUP), 1) // HEAD_DIM
    return (r == c).astype(F32)


def _spread(cols):
    lane = lax.broadcasted_iota(jnp.int32, (1, W_GROUP), 1)
    out = cols[N_HEADS - 1]
    for h in range(N_HEADS - 2, -1, -1):
        out = jnp.where(lane < (h + 1) * HEAD_DIM, cols[h], out)
    return out


def _head_rms_norm(h, bd_bf16, g_row):
    sq = h * h
    sh, sl = _split2(sq)
    ms = (_dot(sh, bd_bf16) + _dot(sl, bd_bf16)) * (1.0 / HEAD_DIM)
    return h * lax.rsqrt(ms + EPS) * g_row


def _mod_kernel(c_ref, w_ref, b_ref, o_ref):
    s = _silu(c_ref[...])
    o_ref[0] = _dot_f32(s, w_ref[0]) + b_ref[0]


def _modulation(cc, w_ada, b_ada):
    nblk = 3 * D_MODEL // 1024
    return pl.pallas_call(
        _mod_kernel,
        grid=(DEPTH, nblk),
        in_specs=[
            pl.BlockSpec((16, D_MODEL), lambda l, j: (0, 0)),
            pl.BlockSpec((1, D_MODEL, 1024), lambda l, j: (l, 0, j)),
            pl.BlockSpec((1, 1, 1024), lambda l, j: (l, 0, j)),
        ],
        out_specs=pl.BlockSpec((1, 16, 1024), lambda l, j: (l, 0, j)),
        out_shape=jax.ShapeDtypeStruct((DEPTH, 16, 3 * D_MODEL), F32),
        compiler_params=pltpu.CompilerParams(vmem_limit_bytes=VMEM_LIMIT),
        name="adaln_modulation",
    )(cc, w_ada, b_ada.reshape(DEPTH, 1, 3 * D_MODEL))


def _token_specs(xs, n_ctx_tiles, t0):
    shape = (1, ROW_TILE, D_MODEL)
    if not isinstance(xs, tuple):
        return [pl.BlockSpec(shape, lambda b, i: (b, i + t0, 0))], [xs]
    return [pl.BlockSpec(shape, lambda b, i: (b, jnp.minimum(i + t0, n_ctx_tiles - 1), 0)),
            pl.BlockSpec(shape, lambda b, i: (b, jnp.maximum(i + t0 - n_ctx_tiles, 0), 0))], list(xs)


def _token_tile(tok_refs, n_ctx_tiles, t0):
    if len(tok_refs) == 1:
        return tok_refs[0][0]
    return jnp.where(pl.program_id(1) + t0 < n_ctx_tiles, tok_refs[0][0], tok_refs[1][0])


def _stream_shape(xs):
    if isinstance(xs, tuple):
        return xs[1].shape[0], xs[0].shape[1] + xs[1].shape[1]
    return xs.shape[0], xs.shape[1]


PROJ_TILES = 2


def _proj_kernel(*refs, with_out, parts, n_ctx_tiles, nt):
    it = iter(refs)
    take = lambda n: [next(it) for _ in range(n)]
    if with_out:
        a_ref, h_ref, g_ref, d_ref, w_ref, gp_ref = take(6)
        mod_refs = take(PROJ_TILES)
    tok_refs = take(2 * PROJ_TILES if parts else 1)
    modn_refs = take(PROJ_TILES)
    gn_ref, wp_ref, wst_ref = take(3)
    if with_out:
        (xs_ref,) = take(1)
    p_ref, gt_ref = take(2)
    step = pl.program_id(0)

    def tile(u):
        rows = slice(u * ROW_TILE, (u + 1) * ROW_TILE)
        if parts:
            i_u = (PROJ_TILES * step + u) % nt
            x = jnp.where(i_u < n_ctx_tiles, tok_refs[2 * u][0], tok_refs[2 * u + 1][0])
        else:
            x = tok_refs[0][rows, :]
        if with_out:
            mix = jnp.concatenate([a_ref[rows, :], h_ref[rows, :], g_ref[rows, :], d_ref[rows, :]], axis=1)
            y = _dot(mix, w_ref[0])
            yield
            ms = jnp.mean(y * y, axis=-1, keepdims=True)
            x = x + mod_refs[u][0, 0, :, 2 * D_MODEL:3 * D_MODEL] * (y * lax.rsqrt(ms + EPS) * gp_ref[...])
            xs_ref[rows, :] = x
        ms = jnp.mean(x * x, axis=-1, keepdims=True)
        yy = x * lax.rsqrt(ms + EPS) * gn_ref[...]
        mod = modn_refs[u]
        hb = (yy * (1.0 + mod[0, 0, :, D_MODEL:2 * D_MODEL]) + mod[0, 0, :, 0:D_MODEL]).astype(BF16)
        yield
        p_ref[rows, :] = _dot(hb, wp_ref[0])
        gt_ref[u] = _dot_nt(wst_ref[0], hb)

    _interleave(*[tile(u) for u in range(PROJ_TILES)])


def _proj(mix, xs, layer_out, layer_in, w_o, g_post, g_pre, mod_all, w_p, w_st, n_ctx_tiles):
    b_, lt = _stream_shape(xs)
    nt = lt // ROW_TILE
    n_tiles = b_ * nt
    assert n_tiles % PROJ_TILES == 0
    step_rows = PROJ_TILES * ROW_TILE
    with_out, parts = mix is not None, isinstance(xs, tuple)
    b_of = lambda s, u: (PROJ_TILES * s + u) // nt
    i_of = lambda s, u: (PROJ_TILES * s + u) % nt
    flat = lambda width: pl.BlockSpec((step_rows, width), lambda s: (s, 0))
    mod_spec = lambda lyr, u: pl.BlockSpec(
        (1, 1, 1, 3 * D_MODEL), lambda s: (lyr, 2 * b_of(s, u) + jnp.where(i_of(s, u) >= n_ctx_tiles, 1, 0), 0, 0))
    in_specs, args = [], []
    if with_out:
        in_specs += [flat(W_GROUP)] * 4 + [pl.BlockSpec((1, D_MODEL, D_MODEL), lambda s: (layer_out, 0, 0)),
                                           pl.BlockSpec((1, D_MODEL), lambda s: (0, 0))]
        in_specs += [mod_spec(layer_out, u) for u in range(PROJ_TILES)]
        args += [m.reshape(b_ * lt, W_GROUP) for m in mix] + [w_o, g_post.reshape(1, D_MODEL)] + [mod_all] * PROJ_TILES
    if parts:
        for u in range(PROJ_TILES):
            in_specs += [pl.BlockSpec((1, ROW_TILE, D_MODEL),
                                      lambda s, u=u: (b_of(s, u), jnp.minimum(i_of(s, u), n_ctx_tiles - 1), 0)),
                         pl.BlockSpec((1, ROW_TILE, D_MODEL),
                                      lambda s, u=u: (b_of(s, u), jnp.maximum(i_of(s, u) - n_ctx_tiles, 0), 0))]
            args += list(xs)
    else:
        in_specs.append(flat(D_MODEL))
        args.append(xs.reshape(b_ * lt, D_MODEL))
    in_specs += [mod_spec(layer_in, u) for u in range(PROJ_TILES)]
    in_specs += [pl.BlockSpec((1, D_MODEL), lambda s: (0, 0)),
                 pl.BlockSpec((1, D_MODEL, N_PROJ), lambda s: (layer_in, 0, 0)),
                 pl.BlockSpec((1, GATE_ROWS, D_MODEL), lambda s: (layer_in, 0, 0))]
    args += [mod_all] * PROJ_TILES + [g_pre.reshape(1, D_MODEL), w_p, w_st]
    out_specs = [flat(N_PROJ), pl.BlockSpec((PROJ_TILES, GATE_ROWS, ROW_TILE), lambda s: (s, 0, 0))]
    out_shape = [jax.ShapeDtypeStruct((b_ * lt, N_PROJ), F32), jax.ShapeDtypeStruct((n_tiles, GATE_ROWS, ROW_TILE), F32)]
    if with_out:
        out_specs.insert(0, flat(D_MODEL))
        out_shape.insert(0, jax.ShapeDtypeStruct((b_ * lt, D_MODEL), F32))
    outs = pl.pallas_call(
        functools.partial(_proj_kernel, with_out=with_out, parts=parts, n_ctx_tiles=n_ctx_tiles, nt=nt),
        grid=(n_tiles // PROJ_TILES,),
        in_specs=in_specs,
        out_specs=out_specs,
        out_shape=out_shape,
        compiler_params=pltpu.CompilerParams(vmem_limit_bytes=VMEM_LIMIT),
        name="outproj_inproj" if with_out else "inproj",
    )(*args)
    xs_new = outs[0].reshape(b_, lt, D_MODEL) if with_out else None
    return xs_new, outs[-2].reshape(b_, lt, N_PROJ), outs[-1]


def _outproj_kernel(a_ref, h_ref, g_ref, d_ref, w_ref, gp_ref, mod_ref, *refs, n_tok, n_ctx_tiles, t0):
    o_ref = refs[n_tok]
    mix = jnp.concatenate([a_ref[0], h_ref[0], g_ref[0], d_ref[0]], axis=1)
    y = _dot(mix, w_ref[0])
    ms = jnp.mean(y * y, axis=-1, keepdims=True)
    yn = y * lax.rsqrt(ms + EPS) * gp_ref[...]
    gt = mod_ref[0, 0, :, 2 * D_MODEL:3 * D_MODEL]
    o_ref[0] = _token_tile(refs[:n_tok], n_ctx_tiles, t0) + gt * yn


def _outproj(mix, w_o, layer, g_post, mod_all, xs, n_ctx_tiles, skip_ctx):
    b_, lt = _stream_shape(xs)
    nt = lt // ROW_TILE
    t0 = n_ctx_tiles if skip_ctx else 0
    rows = lambda b, i: (b, i + t0, 0)
    tok_specs, tok_args = _token_specs(xs, n_ctx_tiles, t0)
    return pl.pallas_call(
        functools.partial(_outproj_kernel, n_tok=len(tok_args), n_ctx_tiles=n_ctx_tiles, t0=t0),
        grid=(b_, nt - t0),
        in_specs=[pl.BlockSpec((1, ROW_TILE, W_GROUP), rows)] * 4 + [
            pl.BlockSpec((1, D_MODEL, D_MODEL), lambda b, i: (layer, 0, 0)),
            pl.BlockSpec((1, D_MODEL), lambda b, i: (0, 0)),
            pl.BlockSpec((1, 1, 1, 3 * D_MODEL),
                         lambda b, i: (layer, 2 * b + jnp.where(i + t0 >= n_ctx_tiles, 1, 0), 0, 0)),
        ] + tok_specs,
        out_specs=pl.BlockSpec((1, ROW_TILE, D_MODEL), lambda b, i: (b, i, 0)),
        out_shape=jax.ShapeDtypeStruct((b_, lt - t0 * ROW_TILE, D_MODEL), F32),
        compiler_params=pltpu.CompilerParams(vmem_limit_bytes=VMEM_LIMIT),
        name="outproj",
    )(*mix, w_o, g_post.reshape(1, D_MODEL), mod_all, *tok_args)


def _interleave(*gens):
    live = list(gens)
    while live:
        live = [g for g in live if next(g, StopIteration) is not StopIteration]


def _scan_chunk(s, reverse, n_ctx, n_tot):
    if not reverse:
        return s
    return jnp.where(s < n_ctx, n_ctx - 1 - s, n_tot - 1 - (s - n_ctx))


def _mlstm_kernel(q_ref, k_ref, v_ref, o_ref, sg_ref, gt_ref, brow_ref, ng_ref,
                  bd_ref, mones_ref, eall_ref, out_ref, hf_ref, hb_ref, c_ref, n_ref, pt_ref, r8_ref, ab_ref,
                  *, chunk, n_ctx, n_tot):
    t_ = chunk
    masks = _head_masks()
    masks_bf = [m.astype(BF16) for m in masks]
    bdm = _block_diag_mask()
    row_i = lax.broadcasted_iota(jnp.int32, (t_, t_), 0)
    col_i = lax.broadcasted_iota(jnp.int32, (t_, t_), 1)
    lane_a = lax.broadcasted_iota(jnp.int32, (8 * n_tot, t_), 1)
    live = (lax.broadcasted_iota(jnp.int32, (8 * n_tot, t_), 0) % 8) < N_HEADS
    c_ref[...] = jnp.zeros_like(c_ref)
    n_ref[...] = jnp.zeros_like(n_ref)

    def scan(x, reverse, op):
        k = 1
        while k < t_:
            if reverse:
                x = jnp.where(lane_a < t_ - k, op(x, pltpu.roll(x, t_ - k, 1)), x)
            else:
                x = jnp.where(lane_a >= k, op(x, pltpu.roll(x, k, 1)), x)
            k *= 2
        return x

    def gates(reverse):
        d_ = 1 if reverse else 0
        end = 0 if reverse else t_ - 1
        order = list(range(n_tot))
        if reverse:
            order = order[:n_ctx][::-1] + order[n_ctx:][::-1]

        def rows_of(g):
            r_ = slice(8 * g, 8 * g + 8)
            return jnp.concatenate([gt_ref[c, r_, :] for c in range(n_tot)], axis=0) + jnp.concatenate(
                [brow_ref[r_, :]] * n_tot, axis=0)

        i_all = jnp.where(live, rows_of(2 * d_), 0.0)
        lf_all = jnp.where(live, _log_sigmoid(rows_of(2 * d_ + 1)), 0.0)
        b_all = scan(lf_all, reverse, jnp.add)
        r_all = i_all - b_all
        cm_all = scan(r_all, reverse, jnp.maximum)
        yield
        m_prev = jnp.zeros((8, 1), F32)
        for c in order:
            cols = slice(c * t_, (c + 1) * t_)
            blk8 = slice(8 * c, 8 * c + 8)
            i8, b8, r8 = i_all[blk8], b_all[blk8], r_all[blk8]
            inter = b8 + m_prev
            m_t = jnp.maximum(inter, b8 + cm_all[blk8])
            b_end = b8[:, end:end + 1]
            g8 = b_end - b8 + i8
            m_new = jnp.maximum(b_end + m_prev, jnp.max(g8, axis=1, keepdims=True))
            a_prev = jnp.exp(b_end + m_prev - m_new)
            packed = jnp.concatenate([b8 - m_t, jnp.exp(inter - m_t), jnp.exp(g8 - m_new), jnp.exp(-m_t),
                                      jnp.zeros((128 - 32, t_), F32)], axis=0)
            pt_ref[d_, cols, :] = packed.T
            r8_ref[d_, :, cols] = r8
            a_b = a_prev[0:1] * masks[0]
            for h in range(1, N_HEADS):
                a_b = a_b + a_prev[h:h + 1] * masks[h]
            ab_ref[d_, c] = a_b
            m_prev = m_new
            yield

    _interleave(gates(False), gates(True))

    def step(reverse, c_idx, h_ref):
        d_ = 1 if reverse else 0
        valid = (col_i >= row_i) if reverse else (col_i <= row_i)
        r0 = pl.multiple_of(c_idx * t_, t_)
        q = q_ref[0, pl.ds(r0, t_), :]
        k = k_ref[0, pl.ds(r0, t_), :] * (HEAD_DIM ** -0.5)
        qb = q.astype(BF16)
        kb = k.astype(BF16)
        vb = v_ref[0, pl.ds(r0, t_), :].astype(BF16)
        pt = pt_ref[d_, pl.ds(r0, t_), :]
        r8 = r8_ref[d_, :, pl.ds(r0, t_)]
        spread = _dot(pt.astype(BF16), eall_ref[...])
        s_cat = []
        for h in range(N_HEADS):
            w_intra = jnp.where(valid, jnp.exp(pt[:, h:h + 1] + r8[h:h + 1, :]), 0.0)
            s_cat.append((_dot_nt(qb * masks_bf[h], kb) * w_intra).astype(BF16))
        v_stack = jnp.concatenate([jnp.concatenate([vb * masks_bf[h] for h in range(N_HEADS)], axis=0),
                                   mones_ref[...]], axis=1)
        acc = _dot(jnp.concatenate(s_cat, axis=1), v_stack)
        yield
        c_prev = c_ref[d_]
        n_prev = n_ref[d_]
        qc = _dot(qb, c_prev.astype(BF16))
        qn = _dot((q * n_prev).astype(BF16), bd_ref[...])
        wi_b = spread[:, 0:W_GROUP]
        num = wi_b * qc + acc[:, 0:W_GROUP]
        den = wi_b * qn + acc[:, W_GROUP:2 * W_GROUP]
        h_ref[pl.ds(r0, t_), :] = num / jnp.maximum(jnp.abs(den), spread[:, 2 * W_GROUP:3 * W_GROUP])
        a_b = ab_ref[d_, c_idx]
        kw = k * spread[:, W_GROUP:2 * W_GROUP]
        c_ref[d_] = a_b * c_prev + bdm * _dot_tn(kw.astype(BF16), vb)
        n_ref[d_] = a_b * n_prev + jnp.sum(kw, axis=0, keepdims=True)

    def body(s, carry):
        _interleave(*[step(rev, _scan_chunk(MLSTM_SCAN_UNROLL * s + u, rev, n_ctx, n_tot), h_ref)
                      for u in range(MLSTM_SCAN_UNROLL) for rev, h_ref in ((False, hf_ref), (True, hb_ref))])
        return carry

    lax.fori_loop(0, n_tot // MLSTM_SCAN_UNROLL, body, 0)

    def finish(c_idx, carry):
        r0 = pl.multiple_of(c_idx * t_, t_)
        hsum = hf_ref[pl.ds(r0, t_), :] + hb_ref[pl.ds(r0, t_), :]
        hsum = hsum * jax.nn.sigmoid(o_ref[0, pl.ds(r0, t_), :])
        hn = _head_rms_norm(hsum, bd_ref[...], ng_ref[...])
        out_ref[0, pl.ds(r0, t_), :] = (hn * _silu(sg_ref[0, pl.ds(r0, t_), :])).astype(out_ref.dtype)
        return carry

    lax.fori_loop(0, n_tot, finish, 0)


def _mlstm(p, gt, gate_b, norm_g, consts, ctx_len):
    b_, lt, _ = p.shape
    t_ = MLSTM_CHUNK
    assert t_ == ROW_TILE
    brow = jnp.pad(gate_b.reshape(4, N_HEADS), ((0, 0), (0, 8 - N_HEADS))).reshape(GATE_ROWS, 1)
    blk = lambda j: pl.BlockSpec((1, lt, W_GROUP), lambda b: (b, 0, j))
    full = lambda shape: pl.BlockSpec(shape, lambda b: (0,) * len(shape))
    kern = functools.partial(_mlstm_kernel, chunk=t_, n_ctx=ctx_len // t_, n_tot=lt // t_)
    return pl.pallas_call(
        kern,
        grid=(b_,),
        in_specs=[blk(MQ), blk(MK), blk(MV), blk(MO), blk(MG),
                  pl.BlockSpec((lt // t_, GATE_ROWS, t_), lambda b: (b, 0, 0)),
                  full((GATE_ROWS, 1)), full((1, W_GROUP)),
                  full((W_GROUP, W_GROUP)), full((N_HEADS * t_, W_GROUP)), full((128, 3 * W_GROUP))],
        out_specs=pl.BlockSpec((1, lt, W_GROUP), lambda b: (b, 0, 0)),
        out_shape=jax.ShapeDtypeStruct((b_, lt, W_GROUP), BF16),
        scratch_shapes=[pltpu.VMEM((lt, W_GROUP), F32), pltpu.VMEM((lt, W_GROUP), F32),
                        pltpu.VMEM((2, W_GROUP, W_GROUP), F32), pltpu.VMEM((2, 1, W_GROUP), F32),
                        pltpu.VMEM((2, lt, 128), F32), pltpu.VMEM((2, 8, lt), F32),
                        pltpu.VMEM((2, lt // t_, 1, W_GROUP), F32)],
        compiler_params=pltpu.CompilerParams(vmem_limit_bytes=VMEM_LIMIT),
        name="mlstm_mixer",
    )(p, p, p, p, p, gt, brow, norm_g.reshape(1, W_GROUP),
      consts["bd"], consts["mones_m"], consts["eall_m"])


def _gla_kernel(q_ref, k_ref, v_ref, sg_ref, sm_ref, wa_ref, ba_ref, ng_ref, tril_ref, triu_ref, bd_ref,
                out_ref, hf_ref, hb_ref, st_ref, qd_ref, kd_ref, kc_ref, eb_ref, *, chunk, n_ctx, n_tot):
    t_ = chunk
    masks_bf = [m.astype(BF16) for m in _head_masks()]
    bdm = _block_diag_mask()
    row_i = lax.broadcasted_iota(jnp.int32, (t_, t_), 0)
    col_i = lax.broadcasted_iota(jnp.int32, (t_, t_), 1)
    st_ref[...] = jnp.zeros_like(st_ref)
    wa_split = [_split2(wa_ref[d_]) for d_ in range(2)]

    def decay(reverse, c_idx):
        d_ = 1 if reverse else 0
        tri_c = triu_ref[...] if reverse else tril_ref[...]
        end_row = 0 if reverse else t_ - 1
        r0 = pl.multiple_of(c_idx * t_, t_)
        sm = sm_ref[0, pl.ds(r0, t_), :]
        smh, sml = _split2(sm)
        wah, wal = wa_split[d_]
        la = _log_sigmoid(_dot(smh, wah) + _dot(smh, wal) + _dot(sml, wah) + ba_ref[d_]) * (1.0 / GLA_NORMALIZER)
        yield
        bcum = _dot_exact_lhs(tri_c, la)
        yield
        q = q_ref[0, pl.ds(r0, t_), :] * (HEAD_DIM ** -0.5)
        k = k_ref[0, pl.ds(r0, t_), :]
        b_end = bcum[end_row:end_row + 1, :]
        qd_ref[d_, pl.ds(r0, t_), :] = (q * jnp.exp(bcum)).astype(BF16)
        kd_ref[d_, pl.ds(r0, t_), :] = (k * jnp.exp(-bcum)).astype(BF16)
        kc_ref[d_, pl.ds(r0, t_), :] = (k * jnp.exp(b_end - bcum)).astype(BF16)
        eb_ref[d_, c_idx] = jnp.exp(b_end)

    def decay_body(s, carry):
        _interleave(*[decay(rev, 2 * s + u) for u in range(2) for rev in (False, True)])
        return carry

    lax.fori_loop(0, n_tot // 2, decay_body, 0)

    def step(reverse, c_idx, h_ref):
        d_ = 1 if reverse else 0
        valid = (col_i >= row_i) if reverse else (col_i <= row_i)
        r0 = pl.multiple_of(c_idx * t_, t_)
        vb = v_ref[0, pl.ds(r0, t_), :].astype(BF16)
        qdb = qd_ref[d_, pl.ds(r0, t_), :]
        kd = kd_ref[d_, pl.ds(r0, t_), :]
        att = [jnp.where(valid, _dot_nt(qdb * masks_bf[h], kd), 0.0).astype(BF16) for h in range(N_HEADS)]
        yield
        st_prev = st_ref[d_]
        v_stack = jnp.concatenate([vb * masks_bf[h] for h in range(N_HEADS)], axis=0)
        h_ref[pl.ds(r0, t_), :] = (_dot_nt(qdb, st_prev.astype(BF16))
                                   + _dot(jnp.concatenate(att, axis=1), v_stack))
        st_ref[d_] = st_prev * eb_ref[d_, c_idx] + bdm * _dot_tn(vb, kc_ref[d_, pl.ds(r0, t_), :])

    def body(s, carry):
        _interleave(*[step(rev, _scan_chunk(GLA_SCAN_UNROLL * s + u, rev, n_ctx, n_tot), h_ref)
                      for u in range(GLA_SCAN_UNROLL) for rev, h_ref in ((False, hf_ref), (True, hb_ref))])
        return carry

    lax.fori_loop(0, n_tot // GLA_SCAN_UNROLL, body, 0)

    def finish(i, carry):
        r0 = pl.multiple_of(i * ROW_TILE, ROW_TILE)
        hsum = hf_ref[pl.ds(r0, ROW_TILE), :] + hb_ref[pl.ds(r0, ROW_TILE), :]
        hn = _head_rms_norm(hsum, bd_ref[...], ng_ref[...])
        out_ref[0, pl.ds(r0, ROW_TILE), :] = (hn * _silu(sg_ref[0, pl.ds(r0, ROW_TILE), :])).astype(out_ref.dtype)
        return carry

    lax.fori_loop(0, n_tot * t_ // ROW_TILE, finish, 0)


def _gla(p, w_alpha, b_alpha, norm_g, consts, ctx_len):
    b_, lt, _ = p.shape
    t_ = GLA_CHUNK
    wa = jnp.zeros((2, 128, W_GROUP), F32)
    wa = wa.at[0, 16:32].set(w_alpha[0]).at[1, 32:48].set(w_alpha[1])
    blk = lambda j: pl.BlockSpec((1, lt, W_GROUP), lambda b: (b, 0, j))
    full = lambda shape: pl.BlockSpec(shape, lambda b: (0,) * len(shape))
    kern = functools.partial(_gla_kernel, chunk=t_, n_ctx=ctx_len // t_, n_tot=lt // t_)
    return pl.pallas_call(
        kern,
        grid=(b_,),
        in_specs=[blk(GQ), blk(GK), blk(GV), blk(GG),
                  pl.BlockSpec((1, lt, 128), lambda b: (b, 0, SMALL_BLOCK)),
                  full((2, 128, W_GROUP)), full((2, 1, W_GROUP)), full((1, W_GROUP)),
                  full((t_, t_)), full((t_, t_)), full((W_GROUP, W_GROUP))],
        out_specs=pl.BlockSpec((1, lt, W_GROUP), lambda b: (b, 0, 0)),
        out_shape=jax.ShapeDtypeStruct((b_, lt, W_GROUP), BF16),
        scratch_shapes=[pltpu.VMEM((lt, W_GROUP), F32), pltpu.VMEM((lt, W_GROUP), F32),
                        pltpu.VMEM((2, W_GROUP, W_GROUP), F32),
                        pltpu.VMEM((2, lt, W_GROUP), BF16), pltpu.VMEM((2, lt, W_GROUP), BF16),
                        pltpu.VMEM((2, lt, W_GROUP), BF16), pltpu.VMEM((2, lt // t_, 1, W_GROUP), F32)],
        compiler_params=pltpu.CompilerParams(vmem_limit_bytes=VMEM_LIMIT),
        name="gla_mixer",
    )(p, p, p, p, p, wa, b_alpha.reshape(2, 1, W_GROUP), norm_g.reshape(1, W_GROUP),
      consts["tril_g"], consts["triu_g"], consts["bd"])


def _attn_kernel(q_ref, kv_ref, sg_ref, cos_ref, sin_ref, sink_ref, pswap_ref, out_ref,
                 kb_ref, vb_ref, *, ctx_len, lat_len):
    g = pl.program_id(1)
    lt = ctx_len + lat_len
    n_ctx_blk = ctx_len // ATT_BLOCK
    nwin = 3 * ATT_BLOCK
    masks = _head_masks()
    log2e = math.log2(math.e)
    low_half = lax.broadcasted_iota(jnp.int32, (1, 2 * HEAD_DIM), 1) < HEAD_DIM

    def rope(x, r0, n):
        swapped = _dot(x.astype(BF16), pswap_ref[...])
        return x * cos_ref[pl.ds(r0, n), :] + swapped * sin_ref[pl.ds(r0, n), :]

    @pl.when(g == 0)
    def _():
        def fill(i, carry):
            r0 = pl.multiple_of(i * ROW_TILE, ROW_TILE)
            k = kv_ref[0, pl.ds(r0, ROW_TILE), 0:2 * HEAD_DIM]
            v = kv_ref[0, pl.ds(r0, ROW_TILE), 2 * HEAD_DIM:4 * HEAD_DIM]
            k_sw = pltpu.roll(k, HEAD_DIM, 1)
            v_sw = pltpu.roll(v, HEAD_DIM, 1)
            k4 = jnp.concatenate([jnp.where(low_half, k, k_sw), jnp.where(low_half, k_sw, k)], axis=1)
            kb_ref[pl.ds(r0, ROW_TILE), :] = rope(k4, r0, ROW_TILE).astype(BF16)
            vb_ref[pl.ds(r0, ROW_TILE), :] = jnp.concatenate(
                [jnp.where(low_half, v, 1.0), jnp.where(low_half, v_sw, 1.0)], axis=1).astype(BF16)
            return carry
        lax.fori_loop(0, lt // ROW_TILE, fill, 0)

    k_ctx = kb_ref[0:ctx_len, :]
    v_ctx = vb_ref[0:ctx_len, :]
    neg_inf = jnp.float32(-jnp.inf)
    rel0 = (lax.broadcasted_iota(jnp.int32, (ATT_BLOCK, nwin), 1)
            - lax.broadcasted_iota(jnp.int32, (ATT_BLOCK, nwin), 0))

    def block(u):
        j = g * ATT_STEP_BLOCKS + u
        rows = slice(u * ATT_BLOCK, (u + 1) * ATT_BLOCK)
        is_lat = j >= n_ctx_blk
        i_lat = jnp.maximum(j - n_ctx_blk, 0)
        w0 = pl.multiple_of(ctx_len + jnp.clip((i_lat - 1) * ATT_BLOCK, 0, lat_len - nwin), ATT_BLOCK)
        q0 = pl.multiple_of(j * ATT_BLOCK, ATT_BLOCK)
        q = rope(q_ref[0, rows, :], q0, ATT_BLOCK) * (HEAD_DIM ** -0.5 * log2e)
        k_win = kb_ref[pl.ds(w0, nwin), :]
        v_win = vb_ref[pl.ds(w0, nwin), :]
        band = is_lat & (jnp.abs(rel0 + (w0 - q0)) <= WINDOW)
        q_stack = jnp.concatenate([(q * masks[h]).astype(BF16) for h in range(N_HEADS)], axis=0)
        band4 = jnp.concatenate([band] * N_HEADS, axis=0)
        sink = jnp.concatenate([jnp.broadcast_to(sink_ref[h] * log2e, (ATT_BLOCK, 128)) for h in range(N_HEADS)],
                               axis=0)
        s_loc = jnp.where(band4, _dot_nt(q_stack, k_win), neg_inf)
        s_ctx = _dot_nt(q_stack, k_ctx)
        yield
        m = jnp.maximum(jnp.max(s_loc, axis=-1, keepdims=True), jnp.max(s_ctx, axis=-1, keepdims=True))
        m128 = jnp.maximum(jnp.broadcast_to(m, (N_HEADS * ATT_BLOCK, 128)), sink)
        p_loc = jnp.exp2(s_loc - jnp.concatenate([m128] * (nwin // 128), axis=1)).astype(BF16)
        p_ctx = jnp.exp2(s_ctx - jnp.concatenate([m128] * (ctx_len // 128), axis=1)).astype(BF16)
        yield
        o_all = _dot(p_loc, v_win) + _dot(p_ctx, v_ctx)
        e_sink = jnp.exp2(sink - m128)
        o = jnp.zeros((ATT_BLOCK, W_GROUP), F32)
        for h in range(N_HEADS):
            hr = slice(h * ATT_BLOCK, (h + 1) * ATT_BLOCK)
            if h % 2 == 0:
                num, den = o_all[hr], pltpu.roll(o_all[hr], W_GROUP - HEAD_DIM, 1)
            else:
                num, den = pltpu.roll(o_all[hr], HEAD_DIM, 1), o_all[hr]
            den = den + jnp.concatenate([e_sink[hr], e_sink[hr]], axis=1)
            o = jnp.where(masks[h] > 0.0, num / den, o)
        out_ref[0, rows, :] = (o * _silu(sg_ref[0, rows, :])).astype(out_ref.dtype)

    _interleave(*[block(u) for u in range(ATT_STEP_BLOCKS)])


def _attn(p, sink, consts, ctx_len):
    b_, lt, _ = p.shape
    step_rows = ATT_STEP_BLOCKS * ATT_BLOCK
    nb = lt // step_rows
    kern = functools.partial(_attn_kernel, ctx_len=ctx_len, lat_len=lt - ctx_len)
    return pl.pallas_call(
        kern,
        grid=(b_, nb),
        in_specs=[pl.BlockSpec((1, step_rows, W_GROUP), lambda b, j: (b, j, AQ)),
                  pl.BlockSpec((1, lt, W_GROUP), lambda b, j: (b, 0, AKV)),
                  pl.BlockSpec((1, step_rows, W_GROUP), lambda b, j: (b, j, AG)),
                  pl.BlockSpec((lt, W_GROUP), lambda b, j: (0, 0)),
                  pl.BlockSpec((lt, W_GROUP), lambda b, j: (0, 0)),
                  pl.BlockSpec((N_HEADS, 1, 1), lambda b, j: (0, 0, 0)),
                  pl.BlockSpec((W_GROUP, W_GROUP), lambda b, j: (0, 0))],
        out_specs=pl.BlockSpec((1, step_rows, W_GROUP), lambda b, j: (b, j, 0)),
        out_shape=jax.ShapeDtypeStruct((b_, lt, W_GROUP), BF16),
        scratch_shapes=[pltpu.VMEM((lt, W_GROUP), BF16), pltpu.VMEM((lt, W_GROUP), BF16)],
        compiler_params=pltpu.CompilerParams(vmem_limit_bytes=VMEM_LIMIT,
                                             dimension_semantics=("arbitrary", "arbitrary")),
        name="window_attention",
    )(p, p, p, consts["rope_cos"], consts["rope_sin"], sink.reshape(N_HEADS, 1, 1), consts["pswap"])


class _FftPlan:
    def __init__(self, n1):
        self.n1 = n1
        self.n = n1 * FFT_N2
        self.k1p = -(-(n1 // 2 + 1) // 8) * 8


def _fft_constants(n1, n_in, n_out):
    plan = _FftPlan(n1)
    n, k1p, half = plan.n, plan.k1p, n1 // 2
    k1 = np.arange(k1p)[:, None].astype(np.float64)
    live = (np.arange(k1p) <= half)[:, None]
    i1 = np.arange(n1)[None, :].astype(np.float64)
    ang = 2.0 * np.pi * k1 * i1 / n1
    f1 = np.concatenate([np.where(live, np.cos(ang), 0.0), np.where(live, -np.sin(ang), 0.0)], axis=0)
    i2 = np.arange(FFT_N2)[None, :].astype(np.float64)
    phi = 2.0 * np.pi * k1 * i2 / n
    tw = np.stack([np.cos(phi), -np.sin(phi)], axis=0)
    tw = np.broadcast_to(tw.reshape(2, k1p * FFT_N2, 1), (2, k1p * FFT_N2, 128))
    th = 2.0 * np.pi * np.outer(np.arange(FFT_N2), np.arange(FFT_N2)) / FFT_N2
    c2, s2 = np.cos(th), np.sin(th)
    m_fwd = np.block([[c2, s2], [-s2, c2]])
    m_inv = np.block([[c2, -s2], [s2, c2]])
    wk = np.where(np.arange(k1p) <= half, 2.0, 0.0)
    wk[0] = 1.0
    wk[half] = 1.0
    psi = 2.0 * np.pi * np.outer(np.arange(n1), np.arange(k1p)) / n1
    g = np.concatenate([np.cos(psi) * wk[None, :], -np.sin(psi) * wk[None, :]], axis=1) / n

    def hl(a):
        a32 = jnp.asarray(a, F32)
        hi = a32.astype(BF16)
        lo = (a32 - hi.astype(F32)).astype(BF16)
        return hi, lo

    return dict(f1=hl(f1), f1_in=hl(f1[:, 0:n_in]), g=hl(g[0:n_out]), m_fwd=hl(m_fwd), m_inv=hl(m_inv),
                tw=jnp.asarray(tw, F32))


def _slab_row(k):
    return k * FFT_PITCH if isinstance(k, int) else pl.multiple_of(k * FFT_PITCH, 8)


def _load_slab(ref, k):
    r0 = _slab_row(k)
    return jnp.concatenate([ref[0, pl.ds(r0, FFT_N2), :], ref[1, pl.ds(r0, FFT_N2), :]], axis=1)


def _store_slab(ref, k, val):
    r0 = _slab_row(k)
    ref[0, pl.ds(r0, FFT_N2), :] = val[:, 0:128]
    ref[1, pl.ds(r0, FFT_N2), :] = val[:, 128:256]


FFT_GROUP = 16


def _cross_slab(src_refs, n_src, mat_h, mat_l, dst_refs, n_dst, dot=_dot_const):
    def body(gidx, carry):
        cols = []
        for i in range(FFT_GROUP):
            i2 = gidx * FFT_GROUP + i
            parts = []
            for half in range(2):
                parts.append(jnp.concatenate(
                    [r[half, pl.ds(i2, n_src, stride=FFT_PITCH), :] for r in src_refs], axis=0))
            cols.append(jnp.concatenate(parts, axis=1))
        x = jnp.concatenate(cols, axis=1)
        y = dot(mat_h, mat_l, x)
        for i in range(FFT_GROUP):
            i2 = gidx * FFT_GROUP + i
            for half in range(2):
                c0 = (2 * i + half) * 128
                for d, ref in enumerate(dst_refs):
                    ref[half, pl.ds(i2, n_dst, stride=FFT_PITCH), :] = y[d * n_dst:(d + 1) * n_dst, c0:c0 + 128]
        return carry

    lax.fori_loop(0, FFT_N2 // FFT_GROUP, body, 0)


def _twiddle(ar, ai, tw_ref, k, conj):
    r0 = pl.multiple_of(k * FFT_N2, FFT_N2)
    tr = tw_ref[0, pl.ds(r0, FFT_N2), :]
    ti = tw_ref[1, pl.ds(r0, FFT_N2), :]
    tr = jnp.concatenate([tr, tr], axis=1)
    ti = jnp.concatenate([ti, ti], axis=1)
    if conj:
        return ar * tr + ai * ti, ai * tr - ar * ti
    return ar * tr - ai * ti, ar * ti + ai * tr


def _filter_kernel(feat_ref, dec_ref, w1_ref, b1_ref, w2_ref, b2_ref, w3_ref, b3_ref, fq_ref,
                   f1h_ref, f1l_ref, mh_ref, ml_ref, tw_ref, out_ref, zbuf, are, aim, *, n1, k1p):
    half = n1 // 2

    u_ = FFT_SLAB_UNROLL
    rows = u_ * FFT_N2

    def fill_block(g):
        r0 = pl.multiple_of(g * rows, rows)
        z = feat_ref[:, pl.ds(r0, rows)]
        h = jnp.sin(fq_ref[0, 0] * (_dot_f32_tn(w1_ref[0], z) + b1_ref[0]))
        yield
        h = jnp.sin(fq_ref[0, 1] * (_dot_f32_tn(w2_ref[0], h) + b2_ref[0]))
        yield
        h = _dot_f32_tn(h, w3_ref[0]) + b3_ref[0]
        hsel = jnp.where(g * u_ < half, h[:, 0:W_GROUP], h[:, W_GROUP:2 * W_GROUP])
        kern = hsel * dec_ref[pl.ds(r0, rows), :]
        for u in range(u_):
            _store_slab(zbuf, g * u_ + u, kern[u * FFT_N2:(u + 1) * FFT_N2])

    def fill(s, carry):
        _interleave(fill_block(2 * s), fill_block(2 * s + 1))
        return carry

    lax.fori_loop(0, n1 // u_ // 2, fill, 0)
    _cross_slab([zbuf], n1, f1h_ref[...], f1l_ref[...], [are, aim], k1p)

    def slab(g, carry):
        cols = []
        for u in range(u_):
            k = g * u_ + u
            ar, ai = _twiddle(_load_slab(are, k), _load_slab(aim, k), tw_ref, k, conj=False)
            cols.append(jnp.concatenate([ar, ai], axis=0))
        x = _dot_const(mh_ref[...], ml_ref[...], jnp.concatenate(cols, axis=1))
        for u in range(u_):
            r0 = pl.multiple_of((g * u_ + u) * FFT_N2, FFT_N2)
            out_ref[0, 0, pl.ds(r0, FFT_N2), :] = x[0:FFT_N2, u * W_GROUP:(u + 1) * W_GROUP]
            out_ref[0, 1, pl.ds(r0, FFT_N2), :] = x[FFT_N2:2 * FFT_N2, u * W_GROUP:(u + 1) * W_GROUP]
        return carry

    lax.fori_loop(0, k1p // u_, slab, 0)


def _filter_tables(l_):
    pos = np.concatenate([np.arange(l_), l_ - np.arange(l_)]).astype(np.float32)
    pos[l_] = 0.0
    t = pos / np.float32(max(l_ - 1, 1))
    w = np.float32(2.0 * math.pi) * pos / np.float32(l_)
    f = np.linspace(1e-4, HYENA_BANDS - 1, HYENA_BANDS, dtype=np.float32)
    feat = np.zeros((2 * l_, 64), np.float32)
    feat[:, 0] = t
    feat[:, 1:1 + HYENA_BANDS] = np.cos(w[:, None] * f)
    feat[:, 1 + HYENA_BANDS:HYENA_EMB] = -np.sin(w[:, None] * f)
    deltas = np.abs(np.linspace(HYENA_MIN_DECAY, HYENA_MAX_DECAY, W_GROUP, dtype=np.float32))
    dec = np.exp(-t[:, None] * deltas[None, :]).astype(np.float32)
    dec[l_] = 0.0
    return jnp.asarray(feat), jnp.asarray(dec)


def _filter_spectra(l_, w1p, b1, w2, b2, w3, b3, freq, fc):
    n1 = 2 * l_ // FFT_N2
    plan = _FftPlan(n1)
    k1p = plan.k1p
    feat, dec = _filter_tables(l_)
    full = lambda shape: pl.BlockSpec(shape, lambda l: (0,) * len(shape))
    per = lambda shape: pl.BlockSpec((1,) + shape, lambda l: (l,) + (0,) * len(shape))
    kern = functools.partial(_filter_kernel, n1=n1, k1p=k1p)
    return pl.pallas_call(
        kern,
        grid=(DEPTH,),
        in_specs=[full((64, 2 * l_)), full((2 * l_, W_GROUP)),
                  per((64, HYENA_FFN)), per((HYENA_FFN, 1)), per((HYENA_FFN, HYENA_FFN)), per((HYENA_FFN, 1)),
                  per((HYENA_FFN, 2 * W_GROUP)), per((1, 2 * W_GROUP)), per((2, HYENA_FFN, 1)),
                  full((2 * k1p, n1)), full((2 * k1p, n1)), full((128, 128)), full((128, 128)),
                  full((2, k1p * FFT_N2, 128))],
        out_specs=pl.BlockSpec((1, 2, k1p * FFT_N2, W_GROUP), lambda l: (l, 0, 0, 0)),
        out_shape=jax.ShapeDtypeStruct((DEPTH, 2, k1p * FFT_N2, W_GROUP), F32),
        scratch_shapes=[pltpu.VMEM((2, n1 * FFT_PITCH, 128), F32),
                        pltpu.VMEM((2, k1p * FFT_PITCH, 128), F32),
                        pltpu.VMEM((2, k1p * FFT_PITCH, 128), F32)],
        compiler_params=pltpu.CompilerParams(vmem_limit_bytes=VMEM_LIMIT),
        name="hyena_filter_spectrum_%d" % l_,
    )(feat.T, dec, w1p, b1.reshape(DEPTH, HYENA_FFN, 1), w2, b2.reshape(DEPTH, HYENA_FFN, 1),
      w3, b3.reshape(DEPTH, 1, 2 * W_GROUP), freq.reshape(DEPTH, 2, HYENA_FFN, 1),
      fc["f1"][0], fc["f1"][1], fc["m_fwd"][0], fc["m_fwd"][1], fc["tw"])


def _short_dft_constants(l_):
    n = 2 * l_
    kp = -(-(l_ + 1) // 8) * 8
    k = np.arange(kp)[:, None].astype(np.float64)
    live = (np.arange(kp) <= l_)[:, None]
    pos = np.arange(n)[None, :].astype(np.float64)
    ang = 2.0 * np.pi * k * pos / n
    flt = np.concatenate([np.where(live, np.cos(ang), 0.0), np.where(live, -np.sin(ang), 0.0)], axis=0)
    wk = np.where(np.arange(kp) <= l_, 2.0, 0.0)
    wk[0] = 1.0
    wk[l_] = 1.0
    psi = 2.0 * np.pi * np.outer(np.arange(l_), np.arange(kp)) / n
    inv = np.concatenate([np.cos(psi) * wk[None, :], -np.sin(psi) * wk[None, :]], axis=1) / n

    def hl(a):
        a32 = jnp.asarray(a, F32)
        hi = a32.astype(BF16)
        return jnp.stack([hi, (a32 - hi.astype(F32)).astype(BF16)], axis=0)

    return dict(fwd=hl(flt[:, 0:l_]), flt=hl(flt), inv=hl(inv), kp=kp)


def _short_filter_kernel(feat_ref, dec_ref, w1_ref, b1_ref, w2_ref, b2_ref, w3_ref, b3_ref, fq_ref, flt_ref,
                         out_ref, *, l_):
    kp = out_ref.shape[2]
    halves = []
    for part in range(2):
        z = feat_ref[part * l_:(part + 1) * l_, :]
        h = jnp.sin(fq_ref[0, 0:1, :] * (_dot_f32(z, w1_ref[0]) + b1_ref[0]))
        h = jnp.sin(fq_ref[0, 1:2, :] * (_dot_f32(h, w2_ref[0]) + b2_ref[0]))
        h = _dot_f32(h, w3_ref[0]) + b3_ref[0]
        halves.append(h[:, part * W_GROUP:(part + 1) * W_GROUP] * dec_ref[part * l_:(part + 1) * l_, :])
    spec = _dot_const(flt_ref[0], flt_ref[1], jnp.concatenate(halves, axis=0))
    out_ref[0, 0] = spec[0:kp]
    out_ref[0, 1] = spec[kp:2 * kp]


def _short_filter_spectra(l_, w1p, b1, w2, b2, w3, b3, freq, sc):
    kp = sc["kp"]
    feat, dec = _filter_tables(l_)
    full = lambda shape: pl.BlockSpec(shape, lambda l: (0,) * len(shape))
    per = lambda shape: pl.BlockSpec((1,) + shape, lambda l: (l,) + (0,) * len(shape))
    return pl.pallas_call(
        functools.partial(_short_filter_kernel, l_=l_),
        grid=(DEPTH,),
        in_specs=[full((2 * l_, 64)), full((2 * l_, W_GROUP)),
                  per((64, HYENA_FFN)), per((1, HYENA_FFN)), per((HYENA_FFN, HYENA_FFN)), per((1, HYENA_FFN)),
                  per((HYENA_FFN, 2 * W_GROUP)), per((1, 2 * W_GROUP)), per((2, HYENA_FFN)),
                  full((2, 2 * kp, 2 * l_))],
        out_specs=pl.BlockSpec((1, 2, kp, W_GROUP), lambda l: (l, 0, 0, 0)),
        out_shape=jax.ShapeDtypeStruct((DEPTH, 2, kp, W_GROUP), F32),
        compiler_params=pltpu.CompilerParams(vmem_limit_bytes=VMEM_LIMIT),
        name="hyena_filter_spectrum_%d" % l_,
    )(feat, dec, w1p, b1.reshape(DEPTH, 1, HYENA_FFN), w2, b2.reshape(DEPTH, 1, HYENA_FFN),
      w3, b3.reshape(DEPTH, 1, 2 * W_GROUP), freq, sc["flt"])


def _hyena_segment(v_ref, x1_ref, x0_ref, sg_ref, cw_ref, cb_ref, d_ref, kf_ref, consts, scratch, out_ref,
                   *, row0, seg_len, n1):
    f1h, f1l, gh, gl, mfh, mfl, mih, mil, tw_ref = consts
    zbuf, are, aim, ybuf = scratch
    n_sig = seg_len // FFT_N2
    n_in = max(n_sig, 8)
    n_out = max(n_sig, 8)
    k1p = _FftPlan(n1).k1p
    row_i = lax.broadcasted_iota(jnp.int32, (FFT_N2, W_GROUP), 0)

    def conv3(ref, blk, jslab):
        r = pl.multiple_of(row0 + jslab * FFT_N2, FFT_N2)
        cur = ref[0, pl.ds(r, FFT_N2), :]
        pe = ref[0, pl.ds(jnp.maximum(r - 1, row0), 1), :] * jnp.where(jslab > 0, 1.0, 0.0)
        ne = ref[0, pl.ds(jnp.minimum(r + FFT_N2, row0 + seg_len - 1), 1), :] * jnp.where(jslab < n_sig - 1, 1.0, 0.0)
        prev = jnp.where(row_i == 0, pe, pltpu.roll(cur, 1, 0))
        nxt = jnp.where(row_i == FFT_N2 - 1, ne, pltpu.roll(cur, FFT_N2 - 1, 0))
        c0 = blk * W_GROUP
        return (cw_ref[0:1, c0:c0 + W_GROUP] * prev + cw_ref[1:2, c0:c0 + W_GROUP] * cur
                + cw_ref[2:3, c0:c0 + W_GROUP] * nxt + cb_ref[:, c0:c0 + W_GROUP])

    def pre(jslab, carry):
        _store_slab(zbuf, jslab, conv3(v_ref, 0, jslab) * conv3(x1_ref, 1, jslab))
        return carry

    lax.fori_loop(0, n_sig, pre, 0)
    for jz in range(n_sig, n_in):
        _store_slab(zbuf, jz, jnp.zeros((FFT_N2, W_GROUP), F32))

    _cross_slab([zbuf], n_in, f1h, f1l, [are, aim], k1p, dot=_dot_const2)

    u_ = FFT_SLAB_UNROLL

    def slab(g, carry):
        cols = []
        for u in range(u_):
            k = g * u_ + u
            ar, ai = _twiddle(_load_slab(are, k), _load_slab(aim, k), tw_ref, k, conj=False)
            cols.append(jnp.concatenate([ar, ai], axis=0))
        x = _dot_const2(mfh, mfl, jnp.concatenate(cols, axis=1))
        cols = []
        for u in range(u_):
            xr = x[0:FFT_N2, u * W_GROUP:(u + 1) * W_GROUP]
            xi = x[FFT_N2:2 * FFT_N2, u * W_GROUP:(u + 1) * W_GROUP]
            r0 = pl.multiple_of((g * u_ + u) * FFT_N2, FFT_N2)
            kr = kf_ref[0, 0, pl.ds(r0, FFT_N2), :]
            ki = kf_ref[0, 1, pl.ds(r0, FFT_N2), :]
            cols.append(jnp.concatenate([xr * kr - xi * ki, xr * ki + xi * kr], axis=0))
        bm = _dot_const2(mih, mil, jnp.concatenate(cols, axis=1))
        for u in range(u_):
            k = g * u_ + u
            br, bi = _twiddle(bm[0:FFT_N2, u * W_GROUP:(u + 1) * W_GROUP],
                              bm[FFT_N2:2 * FFT_N2, u * W_GROUP:(u + 1) * W_GROUP], tw_ref, k, conj=True)
            _store_slab(are, k, br)
            _store_slab(aim, k, bi)
        return carry

    lax.fori_loop(0, k1p // u_, slab, 0)
    _cross_slab([are, aim], k1p, gh, gl, [ybuf], n_out, dot=_dot_const2)

    def post(jslab, carry):
        r = pl.multiple_of(row0 + jslab * FFT_N2, FFT_N2)
        y = _load_slab(ybuf, jslab) + _load_slab(zbuf, jslab) * d_ref[...]
        out_ref[0, pl.ds(r, FFT_N2), :] = (conv3(x0_ref, 2, jslab) * y
                                           * _silu(sg_ref[0, pl.ds(r, FFT_N2), :])).astype(out_ref.dtype)
        return carry

    lax.fori_loop(0, n_sig, post, 0)


def _hyena_short_segment(v_ref, x1_ref, x0_ref, sg_ref, cw_ref, cb_ref, d_ref, kf_ref, fwd_ref, inv_ref,
                         zbuf, ybuf, out_ref, *, seg_len):
    n_sig = seg_len // FFT_N2
    kp = kf_ref.shape[2]
    row_i = lax.broadcasted_iota(jnp.int32, (FFT_N2, W_GROUP), 0)

    def conv3(ref, blk, jslab):
        r = pl.multiple_of(jslab * FFT_N2, FFT_N2)
        cur = ref[0, pl.ds(r, FFT_N2), :]
        pe = ref[0, pl.ds(jnp.maximum(r - 1, 0), 1), :] * jnp.where(jslab > 0, 1.0, 0.0)
        ne = ref[0, pl.ds(jnp.minimum(r + FFT_N2, seg_len - 1), 1), :] * jnp.where(jslab < n_sig - 1, 1.0, 0.0)
        prev = jnp.where(row_i == 0, pe, pltpu.roll(cur, 1, 0))
        nxt = jnp.where(row_i == FFT_N2 - 1, ne, pltpu.roll(cur, FFT_N2 - 1, 0))
        c0 = blk * W_GROUP
        return (cw_ref[0:1, c0:c0 + W_GROUP] * prev + cw_ref[1:2, c0:c0 + W_GROUP] * cur
                + cw_ref[2:3, c0:c0 + W_GROUP] * nxt + cb_ref[:, c0:c0 + W_GROUP])

    def pre(jslab, carry):
        r = pl.multiple_of(jslab * FFT_N2, FFT_N2)
        zbuf[pl.ds(r, FFT_N2), :] = conv3(v_ref, 0, jslab) * conv3(x1_ref, 1, jslab)
        return carry

    lax.fori_loop(0, n_sig, pre, 0)
    x = _dot_const2(fwd_ref[0], fwd_ref[1], zbuf[...])
    xr, xi = x[0:kp], x[kp:2 * kp]
    kr, ki = kf_ref[0, 0], kf_ref[0, 1]
    y = jnp.concatenate([xr * kr - xi * ki, xr * ki + xi * kr], axis=0)
    ybuf[...] = _dot_const2(inv_ref[0], inv_ref[1], y)

    def post(jslab, carry):
        r = pl.multiple_of(jslab * FFT_N2, FFT_N2)
        y_ = ybuf[pl.ds(r, FFT_N2), :] + zbuf[pl.ds(r, FFT_N2), :] * d_ref[...]
        out_ref[0, pl.ds(r, FFT_N2), :] = (conv3(x0_ref, 2, jslab) * y_
                                           * _silu(sg_ref[0, pl.ds(r, FFT_N2), :])).astype(out_ref.dtype)
        return carry

    lax.fori_loop(0, n_sig, post, 0)


def _hyena_kernel(v_ref, x1_ref, x0_ref, sg_ref, cw_ref, cb_ref, d_ref, kfc_ref, kfl_ref,
                  c_fwd, c_inv, l_f1h, l_f1l, l_gh, l_gl, l_tw,
                  mfh_ref, mfl_ref, mih_ref, mil_ref, out_ref,
                  zc, yc, zl, arl, ail, yl, *, ctx_len, lat_len):
    mats = (mfh_ref[...], mfl_ref[...], mih_ref[...], mil_ref[...])
    _hyena_short_segment(v_ref, x1_ref, x0_ref, sg_ref, cw_ref, cb_ref, d_ref, kfc_ref, c_fwd, c_inv,
                         zc, yc, out_ref, seg_len=ctx_len)
    _hyena_segment(v_ref, x1_ref, x0_ref, sg_ref, cw_ref, cb_ref, d_ref,
                   kfl_ref, (l_f1h[...], l_f1l[...], l_gh[...], l_gl[...]) + mats + (l_tw,), (zl, arl, ail, yl), out_ref,
                   row0=ctx_len, seg_len=lat_len, n1=2 * lat_len // FFT_N2)


def _hyena(p, conv_w, conv_b, d, kf_ctx, kf_lat, layer, fc_ctx, fc_lat, ctx_len):
    b_, lt, _ = p.shape
    lat_len = lt - ctx_len
    n1l = 2 * lat_len // FFT_N2
    kl = _FftPlan(n1l).k1p
    blk = lambda j: pl.BlockSpec((1, lt, W_GROUP), lambda b: (b, 0, j))
    full = lambda a: pl.BlockSpec(a.shape, lambda b: (0,) * a.ndim)
    per_layer = lambda a: pl.BlockSpec((1,) + a.shape[1:], lambda b: (layer,) + (0,) * (a.ndim - 1))
    cst = [fc_ctx["fwd"], fc_ctx["inv"],
           fc_lat["f1_in"][0], fc_lat["f1_in"][1], fc_lat["g"][0], fc_lat["g"][1], fc_lat["tw"],
           fc_lat["m_fwd"][0], fc_lat["m_fwd"][1], fc_lat["m_inv"][0], fc_lat["m_inv"][1]]
    small = [conv_w, conv_b.reshape(1, 3 * W_GROUP), d.reshape(1, W_GROUP)]
    sbuf = lambda n: pltpu.VMEM((2, n * FFT_PITCH, 128), F32)
    kern = functools.partial(_hyena_kernel, ctx_len=ctx_len, lat_len=lat_len)
    return pl.pallas_call(
        kern,
        grid=(b_,),
        in_specs=([blk(HV), blk(HX1), blk(HX0), blk(HG)] + [full(a) for a in small]
                  + [per_layer(kf_ctx), per_layer(kf_lat)] + [full(a) for a in cst]),
        out_specs=pl.BlockSpec((1, lt, W_GROUP), lambda b: (b, 0, 0)),
        out_shape=jax.ShapeDtypeStruct((b_, lt, W_GROUP), BF16),
        scratch_shapes=[pltpu.VMEM((ctx_len, W_GROUP), F32), pltpu.VMEM((ctx_len, W_GROUP), F32),
                        sbuf(n1l // 2), sbuf(kl), sbuf(kl), sbuf(n1l // 2)],
        compiler_params=pltpu.CompilerParams(vmem_limit_bytes=VMEM_LIMIT),
        name="hyena_mixer",
    )(p, p, p, p, *small, kf_ctx, kf_lat, *cst)


def _rope_tables(ctx_len, lat_len):
    pos = np.arange(lat_len)
    row = (pos // GRID_W).astype(np.float32)
    col = (pos % GRID_W).astype(np.float32)
    inv = (np.float32(ROPE_BASE) ** (-np.arange(ROPE_AXIS_FREQS, dtype=np.float32) / np.float32(ROPE_AXIS_FREQS)))
    ang = np.concatenate([row[:, None] * inv, col[:, None] * inv], axis=-1).astype(np.float32)
    cos, sin = np.cos(ang), np.sin(ang)
    cos_h = np.concatenate([cos, cos], axis=-1)
    sin_h = np.concatenate([-sin, sin], axis=-1)
    cos_l = np.tile(cos_h, (1, N_HEADS))
    sin_l = np.tile(sin_h, (1, N_HEADS))
    cos_t = np.concatenate([np.ones((ctx_len, W_GROUP), np.float32), cos_l], axis=0)
    sin_t = np.concatenate([np.zeros((ctx_len, W_GROUP), np.float32), sin_l], axis=0)
    return jnp.asarray(cos_t, F32), jnp.asarray(sin_t, F32)


def _rope_partner():
    d = np.arange(W_GROUP)
    partner = np.where(d % HEAD_DIM < HEAD_DIM // 2, d + HEAD_DIM // 2, d - HEAD_DIM // 2)
    p = np.zeros((W_GROUP, W_GROUP), np.float32)
    p[partner, d] = 1.0
    return jnp.asarray(p, BF16)


def _mlstm_expansion():
    e = np.zeros((128, 3 * W_GROUP), np.float32)
    for j in range(3):
        for h in range(N_HEADS):
            e[8 * (j + 1) + h, j * W_GROUP + h * HEAD_DIM:j * W_GROUP + (h + 1) * HEAD_DIM] = 1.0
    return jnp.asarray(e, BF16)


def _tri(n):
    t = np.tril(np.ones((n, n), np.float32))
    return jnp.asarray(t, BF16), jnp.asarray(t.T, BF16)


def _w_layout_kernel(w_ref, o_ref, ot_ref):
    h0 = 5 * W_GROUP + 16
    g0 = h0 + 4 * W_GROUP
    d0 = g0 + 4 * W_GROUP + 2 * GLA_RANK
    w = w_ref[0]
    zero4 = jnp.zeros((8 - N_HEADS, w.shape[1]), F32)
    ot_ref[0] = jnp.concatenate(
        [piece for g in range(4)
         for piece in (w[5 * W_GROUP + N_HEADS * g:5 * W_GROUP + N_HEADS * (g + 1)], zero4)], axis=0).astype(BF16)
    o_ref[0] = jnp.concatenate([
        w[0:5 * W_GROUP],
        w[h0:h0 + 4 * W_GROUP],
        w[g0:g0 + 4 * W_GROUP],
        w[d0:d0 + 3 * W_GROUP],
        w[5 * W_GROUP:5 * W_GROUP + 16],
        w[g0 + 4 * W_GROUP:g0 + 4 * W_GROUP + 2 * GLA_RANK],
        jnp.zeros((128 - 48, w.shape[1]), F32)], axis=0).T.astype(BF16)


def _layout_w_in(w_in):
    n_in = w_in.shape[-1]
    return pl.pallas_call(
        _w_layout_kernel,
        grid=(DEPTH, D_MODEL // ROW_TILE),
        in_specs=[pl.BlockSpec((1, n_in, ROW_TILE), lambda l, i: (l, 0, i))],
        out_specs=[pl.BlockSpec((1, ROW_TILE, N_PROJ), lambda l, i: (l, i, 0)),
                   pl.BlockSpec((1, GATE_ROWS, ROW_TILE), lambda l, i: (l, 0, i))],
        out_shape=[jax.ShapeDtypeStruct((DEPTH, D_MODEL, N_PROJ), BF16),
                   jax.ShapeDtypeStruct((DEPTH, GATE_ROWS, D_MODEL), BF16)],
        compiler_params=pltpu.CompilerParams(vmem_limit_bytes=VMEM_LIMIT),
        name="inproj_weight_layout",
    )(jnp.swapaxes(w_in, 1, 2))


def kernel(x, c, ctx, c_ctx, w_ada, b_ada, g_pre, g_post, w_in, mlstm_gate_b, mlstm_norm_g, hyena_conv_w,
           hyena_conv_b, hyena_w1, hyena_b1, hyena_w2, hyena_b2, hyena_w3, hyena_b3, hyena_freq, hyena_d,
           gla_w_alpha, gla_b_alpha, gla_norm_g, attn_sink, w_out):
    b_, lat_len, _ = x.shape
    ctx_len = ctx.shape[1]
    n_ctx_tiles = ctx_len // ROW_TILE

    tril_m, triu_m = _tri(MLSTM_CHUNK)
    tril_g, triu_g = _tri(GLA_CHUNK)
    bd = np.kron(np.eye(N_HEADS, dtype=np.float32), np.ones((HEAD_DIM, HEAD_DIM), np.float32))
    rope_cos, rope_sin = _rope_tables(ctx_len, lat_len)
    mones_m = np.kron(np.eye(N_HEADS, dtype=np.float32), np.ones((MLSTM_CHUNK, HEAD_DIM), np.float32))
    consts = dict(tril_m=tril_m, triu_m=triu_m, tril_g=tril_g, triu_g=triu_g, bd=jnp.asarray(bd, BF16),
                  mones_m=jnp.asarray(mones_m, BF16), eall_m=_mlstm_expansion(),
                  rope_cos=rope_cos, rope_sin=rope_sin, pswap=_rope_partner())
    fc_ctx = _short_dft_constants(ctx_len)
    fc_lat = _fft_constants(2 * lat_len // FFT_N2, lat_len // FFT_N2, lat_len // FFT_N2)

    cc = jnp.zeros((16, D_MODEL), F32).at[0:b_].set(c).at[b_].set(c_ctx)
    mod = _modulation(cc, w_ada, b_ada)
    mod_ctx = jnp.broadcast_to(mod[:, b_:b_ + 1], (DEPTH, b_, 3 * D_MODEL))
    mod_all = jnp.stack([mod_ctx, mod[:, 0:b_]], axis=2).reshape(DEPTH, 2 * b_, 1, 3 * D_MODEL)

    w_p, w_st = _layout_w_in(w_in)
    w_o = w_out.astype(BF16)
    w1p = jnp.zeros((DEPTH, 64, HYENA_FFN), F32).at[:, 0:HYENA_EMB].set(hyena_w1)
    filt = (w1p, hyena_b1, hyena_w2, hyena_b2, hyena_w3, hyena_b3, hyena_freq)
    kf_ctx = _short_filter_spectra(ctx_len, *filt, fc_ctx)
    kf_lat = _filter_spectra(lat_len, *filt, fc_lat)

    xs = (ctx, x)
    _, p, gt = _proj(None, xs, None, 0, w_o, None, g_pre[0], mod_all, w_p, w_st, n_ctx_tiles)
    for l in range(DEPTH):
        a = _mlstm(p, gt, mlstm_gate_b[l], mlstm_norm_g[l], consts, ctx_len)
        hy = _hyena(p, hyena_conv_w[l], hyena_conv_b[l], hyena_d[l], kf_ctx, kf_lat, l, fc_ctx, fc_lat, ctx_len)
        g = _gla(p, gla_w_alpha[l], gla_b_alpha[l], gla_norm_g[l], consts, ctx_len)
        d = _attn(p, attn_sink[l], consts, ctx_len)
        if l < DEPTH - 1:
            xs, p, gt = _proj((a, hy, g, d), xs, l, l + 1, w_o, g_post[l], g_pre[l + 1], mod_all, w_p, w_st,
                              n_ctx_tiles)
    return _outproj((a, hy, g, d), w_o, DEPTH - 1, g_post[DEPTH - 1], mod_all, xs, n_ctx_tiles, skip_ctx=True)
```

```python
import functools
import math

import numpy as np
import jax
import jax.numpy as jnp
from jax import lax
from jax.experimental import pallas as pl
from jax.experimental.pallas import tpu as pltpu

F32 = jnp.float32
BF16 = jnp.bfloat16

D_MODEL = 1024
DEPTH = 4
GRID_W = 64
W_GROUP = 256
HEAD_DIM = 64
N_HEADS = 4
ATT_KV_HEADS = 2
WINDOW = 128
ATT_BLOCK = 128
GLA_RANK = 16
GLA_NORMALIZER = 16.0
HYENA_BANDS = 16
HYENA_EMB = 1 + 2 * HYENA_BANDS
HYENA_FFN = 64
HYENA_MIN_DECAY = math.log(1e-2) / 1.5
HYENA_MAX_DECAY = math.log(1e-2) / 0.3
ROPE_BASE = 10000.0
ROPE_AXIS_FREQS = HEAD_DIM // 4
EPS = 1e-6

N_WIDE_BLOCKS = 16
SMALL_COL0 = N_WIDE_BLOCKS * W_GROUP
N_PROJ = SMALL_COL0 + 128
SMALL_BLOCK = SMALL_COL0 // 128
(MQ, MK, MV, MO, MG, HV, HX1, HX0, HG, GQ, GK, GV, GG, AQ, AKV, AG) = range(N_WIDE_BLOCKS)

GATE_ROWS = 32
ROW_TILE = 256
MLSTM_CHUNK = 256
MLSTM_SCAN_UNROLL = 3
GLA_CHUNK = 128
GLA_SCAN_UNROLL = 3
ATT_STEP_BLOCKS = 6
ATT_FILL_UNROLL = 3
FFT_N2 = 64
FFT_PITCH = 72
FFT_SLAB_UNROLL = 8
VMEM_LIMIT = 56 * 1024 * 1024


def _dot(a, b):
    return jnp.dot(a, b, preferred_element_type=F32)


def _dot_nt(a, b):
    return lax.dot_general(a, b, (((1,), (1,)), ((), ())), preferred_element_type=F32)


def _dot_tn(a, b):
    return lax.dot_general(a, b, (((0,), (0,)), ((), ())), preferred_element_type=F32)


def _split2(x):
    hi = x.astype(BF16)
    lo = (x - hi.astype(F32)).astype(BF16)
    return hi, lo


def _dot_const(ch, cl, x):
    xh, xl = _split2(x)
    return _dot(ch, xh) + _dot(ch, xl) + _dot(cl, xh)


def _dot_const2(ch, cl, x):
    xb = x.astype(BF16)
    return _dot(ch, xb) + _dot(cl, xb)


def _dot_f32(a, b):
    ah, al = _split2(a)
    bh, bl = _split2(b)
    return _dot(ah, bh) + _dot(ah, bl) + _dot(al, bh)


def _dot_f32_tn(a, b):
    ah, al = _split2(a)
    bh, bl = _split2(b)
    return _dot_tn(ah, bh) + _dot_tn(ah, bl) + _dot_tn(al, bh)


def _dot_exact_lhs(c, x):
    xh, xl = _split2(x)
    return _dot(c, xh) + _dot(c, xl)


def _log_sigmoid(x):
    return jnp.minimum(x, 0.0) - jnp.log(1.0 + jnp.exp(-jnp.abs(x)))


def _silu(x):
    return x * jax.nn.sigmoid(x)


def _head_masks():
    lane = lax.broadcasted_iota(jnp.int32, (1, W_GROUP), 1) // HEAD_DIM
    return [(lane == h).astype(F32) for h in range(N_HEADS)]


def _block_diag_mask():
    r = lax.broadcasted_iota(jnp.int32, (W_GROUP, W_GROUP), 0) // HEAD_DIM
    c = lax.broadcasted_iota(jnp.int32, (W_GROUP, W_GROUP), 1) // HEAD_DIM
    return (r == c).astype(F32)


def _spread(cols):
    lane = lax.broadcasted_iota(jnp.int32, (1, W_GROUP), 1)
    out = cols[N_HEADS - 1]
    for h in range(N_HEADS - 2, -1, -1):
        out = jnp.where(lane < (h + 1) * HEAD_DIM, cols[h], out)
    return out


def _head_rms_norm(h, bd_bf16, g_row):
    sq = h * h
    sh, sl = _split2(sq)
    ms = (_dot(sh, bd_bf16) + _dot(sl, bd_bf16)) * (1.0 / HEAD_DIM)
    return h * lax.rsqrt(ms + EPS) * g_row


def _mod_kernel(c_ref, w_ref, b_ref, o_ref):
    s = _silu(c_ref[...])
    o_ref[0] = _dot_f32(s, w_ref[0]) + b_ref[0]


def _modulation(cc, w_ada, b_ada):
    nblk = 3 * D_MODEL // 1024
    return pl.pallas_call(
        _mod_kernel,
        grid=(DEPTH, nblk),
        in_specs=[
            pl.BlockSpec((16, D_MODEL), lambda l, j: (0, 0)),
            pl.BlockSpec((1, D_MODEL, 1024), lambda l, j: (l, 0, j)),
            pl.BlockSpec((1, 1, 1024), lambda l, j: (l, 0, j)),
        ],
        out_specs=pl.BlockSpec((1, 16, 1024), lambda l, j: (l, 0, j)),
        out_shape=jax.ShapeDtypeStruct((DEPTH, 16, 3 * D_MODEL), F32),
        compiler_params=pltpu.CompilerParams(vmem_limit_bytes=VMEM_LIMIT),
        name="adaln_modulation",
    )(cc, w_ada, b_ada.reshape(DEPTH, 1, 3 * D_MODEL))


def _token_specs(xs, n_ctx_tiles, t0):
    shape = (1, ROW_TILE, D_MODEL)
    if not isinstance(xs, tuple):
        return [pl.BlockSpec(shape, lambda b, i: (b, i + t0, 0))], [xs]
    return [pl.BlockSpec(shape, lambda b, i: (b, jnp.minimum(i + t0, n_ctx_tiles - 1), 0)),
            pl.BlockSpec(shape, lambda b, i: (b, jnp.maximum(i + t0 - n_ctx_tiles, 0), 0))], list(xs)


def _token_tile(tok_refs, n_ctx_tiles, t0):
    if len(tok_refs) == 1:
        return tok_refs[0][0]
    return jnp.where(pl.program_id(1) + t0 < n_ctx_tiles, tok_refs[0][0], tok_refs[1][0])


def _stream_shape(xs):
    if isinstance(xs, tuple):
        return xs[1].shape[0], xs[0].shape[1] + xs[1].shape[1]
    return xs.shape[0], xs.shape[1]


PROJ_TILES = 2


def _proj_kernel(*refs, with_out, parts, n_ctx_tiles, nt):
    it = iter(refs)
    take = lambda n: [next(it) for _ in range(n)]
    if with_out:
        a_ref, h_ref, g_ref, d_ref, w_ref, gp_ref = take(6)
        mod_refs = take(PROJ_TILES)
    tok_refs = take(2 * PROJ_TILES if parts else 1)
    modn_refs = take(PROJ_TILES)
    gn_ref, wp_ref, wst_ref = take(3)
    if with_out:
        (xs_ref,) = take(1)
    p_ref, gt_ref = take(2)
    step = pl.program_id(0)

    def tile(u):
        rows = slice(u * ROW_TILE, (u + 1) * ROW_TILE)
        if parts:
            i_u = (PROJ_TILES * step + u) % nt
            x = jnp.where(i_u < n_ctx_tiles, tok_refs[2 * u][0], tok_refs[2 * u + 1][0])
        else:
            x = tok_refs[0][rows, :]
        if with_out:
            mix = jnp.concatenate([a_ref[rows, :], h_ref[rows, :], g_ref[rows, :], d_ref[rows, :]], axis=1)
            y = _dot(mix, w_ref[0])
            yield
            ms = jnp.mean(y * y, axis=-1, keepdims=True)
            x = x + mod_refs[u][0, 0, :, 2 * D_MODEL:3 * D_MODEL] * (y * lax.rsqrt(ms + EPS) * gp_ref[...])
            xs_ref[rows, :] = x
        ms = jnp.mean(x * x, axis=-1, keepdims=True)
        yy = x * lax.rsqrt(ms + EPS) * gn_ref[...]
        mod = modn_refs[u]
        hb = (yy * (1.0 + mod[0, 0, :, D_MODEL:2 * D_MODEL]) + mod[0, 0, :, 0:D_MODEL]).astype(BF16)
        yield
        p_ref[rows, :] = _dot(hb, wp_ref[0])
        gt_ref[u] = _dot_nt(wst_ref[0], hb)

    _interleave(*[tile(u) for u in range(PROJ_TILES)])


def _proj(mix, xs, layer_out, layer_in, w_o, g_post, g_pre, mod_all, w_p, w_st, n_ctx_tiles):
    b_, lt = _stream_shape(xs)
    nt = lt // ROW_TILE
    n_tiles = b_ * nt
    assert n_tiles % PROJ_TILES == 0
    step_rows = PROJ_TILES * ROW_TILE
    with_out, parts = mix is not None, isinstance(xs, tuple)
    b_of = lambda s, u: (PROJ_TILES * s + u) // nt
    i_of = lambda s, u: (PROJ_TILES * s + u) % nt
    flat = lambda width: pl.BlockSpec((step_rows, width), lambda s: (s, 0))
    mod_spec = lambda lyr, u: pl.BlockSpec(
        (1, 1, 1, 3 * D_MODEL), lambda s: (lyr, 2 * b_of(s, u) + jnp.where(i_of(s, u) >= n_ctx_tiles, 1, 0), 0, 0))
    in_specs, args = [], []
    if with_out:
        in_specs += [flat(W_GROUP)] * 4 + [pl.BlockSpec((1, D_MODEL, D_MODEL), lambda s: (layer_out, 0, 0)),
                                           pl.BlockSpec((1, D_MODEL), lambda s: (0, 0))]
        in_specs += [mod_spec(layer_out, u) for u in range(PROJ_TILES)]
        args += [m.reshape(b_ * lt, W_GROUP) for m in mix] + [w_o, g_post.reshape(1, D_MODEL)] + [mod_all] * PROJ_TILES
    if parts:
        for u in range(PROJ_TILES):
            in_specs += [pl.BlockSpec((1, ROW_TILE, D_MODEL),
                                      lambda s, u=u: (b_of(s, u), jnp.minimum(i_of(s, u), n_ctx_tiles - 1), 0)),
                         pl.BlockSpec((1, ROW_TILE, D_MODEL),
                                      lambda s, u=u: (b_of(s, u), jnp.maximum(i_of(s, u) - n_ctx_tiles, 0), 0))]
            args += list(xs)
    else:
        in_specs.append(flat(D_MODEL))
        args.append(xs.reshape(b_ * lt, D_MODEL))
    in_specs += [mod_spec(layer_in, u) for u in range(PROJ_TILES)]
    in_specs += [pl.BlockSpec((1, D_MODEL), lambda s: (0, 0)),
                 pl.BlockSpec((1, D_MODEL, N_PROJ), lambda s: (layer_in, 0, 0)),
                 pl.BlockSpec((1, GATE_ROWS, D_MODEL), lambda s: (layer_in, 0, 0))]
    args += [mod_all] * PROJ_TILES + [g_pre.reshape(1, D_MODEL), w_p, w_st]
    out_specs = [flat(N_PROJ), pl.BlockSpec((PROJ_TILES, GATE_ROWS, ROW_TILE), lambda s: (s, 0, 0))]
    out_shape = [jax.ShapeDtypeStruct((b_ * lt, N_PROJ), F32), jax.ShapeDtypeStruct((n_tiles, GATE_ROWS, ROW_TILE), F32)]
    if with_out:
        out_specs.insert(0, flat(D_MODEL))
        out_shape.insert(0, jax.ShapeDtypeStruct((b_ * lt, D_MODEL), F32))
    outs = pl.pallas_call(
        functools.partial(_proj_kernel, with_out=with_out, parts=parts, n_ctx_tiles=n_ctx_tiles, nt=nt),
        grid=(n_tiles // PROJ_TILES,),
        in_specs=in_specs,
        out_specs=out_specs,
        out_shape=out_shape,
        compiler_params=pltpu.CompilerParams(vmem_limit_bytes=VMEM_LIMIT),
        name="outproj_inproj" if with_out else "inproj",
    )(*args)
    xs_new = outs[0].reshape(b_, lt, D_MODEL) if with_out else None
    return xs_new, outs[-2].reshape(b_, lt, N_PROJ), outs[-1]


def _outproj_kernel(a_ref, h_ref, g_ref, d_ref, w_ref, gp_ref, mod_ref, *refs, n_tok, n_ctx_tiles, t0):
    o_ref = refs[n_tok]
    mix = jnp.concatenate([a_ref[0], h_ref[0], g_ref[0], d_ref[0]], axis=1)
    y = _dot(mix, w_ref[0])
    ms = jnp.mean(y * y, axis=-1, keepdims=True)
    yn = y * lax.rsqrt(ms + EPS) * gp_ref[...]
    gt = mod_ref[0, 0, :, 2 * D_MODEL:3 * D_MODEL]
    o_ref[0] = _token_tile(refs[:n_tok], n_ctx_tiles, t0) + gt * yn


def _outproj(mix, w_o, layer, g_post, mod_all, xs, n_ctx_tiles, skip_ctx):
    b_, lt = _stream_shape(xs)
    nt = lt // ROW_TILE
    t0 = n_ctx_tiles if skip_ctx else 0
    rows = lambda b, i: (b, i + t0, 0)
    tok_specs, tok_args = _token_specs(xs, n_ctx_tiles, t0)
    return pl.pallas_call(
        functools.partial(_outproj_kernel, n_tok=len(tok_args), n_ctx_tiles=n_ctx_tiles, t0=t0),
        grid=(b_, nt - t0),
        in_specs=[pl.BlockSpec((1, ROW_TILE, W_GROUP), rows)] * 4 + [
            pl.BlockSpec((1, D_MODEL, D_MODEL), lambda b, i: (layer, 0, 0)),
            pl.BlockSpec((1, D_MODEL), lambda b, i: (0, 0)),
            pl.BlockSpec((1, 1, 1, 3 * D_MODEL),
                         lambda b, i: (layer, 2 * b + jnp.where(i + t0 >= n_ctx_tiles, 1, 0), 0, 0)),
        ] + tok_specs,
        out_specs=pl.BlockSpec((1, ROW_TILE, D_MODEL), lambda b, i: (b, i, 0)),
        out_shape=jax.ShapeDtypeStruct((b_, lt - t0 * ROW_TILE, D_MODEL), F32),
        compiler_params=pltpu.CompilerParams(vmem_limit_bytes=VMEM_LIMIT),
        name="outproj",
    )(*mix, w_o, g_post.reshape(1, D_MODEL), mod_all, *tok_args)


def _interleave(*gens):
    live = list(gens)
    while live:
        live = [g for g in live if next(g, StopIteration) is not StopIteration]


def _scan_chunk(s, reverse, n_ctx, n_tot):
    if not reverse:
        return s
    return jnp.where(s < n_ctx, n_ctx - 1 - s, n_tot - 1 - (s - n_ctx))


def _mlstm_kernel(q_ref, k_ref, v_ref, o_ref, sg_ref, gt_ref, brow_ref, ng_ref,
                  bd_ref, mones_ref, eall_ref, out_ref, hf_ref, hb_ref, c_ref, n_ref, pt_ref, r8_ref, ab_ref,
                  *, chunk, n_ctx, n_tot):
    t_ = chunk
    masks = _head_masks()
    masks_bf = [m.astype(BF16) for m in masks]
    bdm = _block_diag_mask()
    row_i = lax.broadcasted_iota(jnp.int32, (t_, t_), 0)
    col_i = lax.broadcasted_iota(jnp.int32, (t_, t_), 1)
    lane_a = lax.broadcasted_iota(jnp.int32, (8 * n_tot, t_), 1)
    live = (lax.broadcasted_iota(jnp.int32, (8 * n_tot, t_), 0) % 8) < N_HEADS
    c_ref[...] = jnp.zeros_like(c_ref)
    n_ref[...] = jnp.zeros_like(n_ref)

    def scan(x, reverse, op):
        k = 1
        while k < t_:
            if reverse:
                x = jnp.where(lane_a < t_ - k, op(x, pltpu.roll(x, t_ - k, 1)), x)
            else:
                x = jnp.where(lane_a >= k, op(x, pltpu.roll(x, k, 1)), x)
            k *= 2
        return x

    def gates(reverse):
        d_ = 1 if reverse else 0
        end = 0 if reverse else t_ - 1
        order = list(range(n_tot))
        if reverse:
            order = order[:n_ctx][::-1] + order[n_ctx:][::-1]

        def rows_of(g):
            r_ = slice(8 * g, 8 * g + 8)
            return jnp.concatenate([gt_ref[c, r_, :] for c in range(n_tot)], axis=0) + jnp.concatenate(
                [brow_ref[r_, :]] * n_tot, axis=0)

        i_all = jnp.where(live, rows_of(2 * d_), 0.0)
        lf_all = jnp.where(live, _log_sigmoid(rows_of(2 * d_ + 1)), 0.0)
        b_all = scan(lf_all, reverse, jnp.add)
        r_all = i_all - b_all
        cm_all = scan(r_all, reverse, jnp.maximum)
        yield
        m_prev = jnp.zeros((8, 1), F32)
        for c in order:
            cols = slice(c * t_, (c + 1) * t_)
            blk8 = slice(8 * c, 8 * c + 8)
            i8, b8, r8 = i_all[blk8], b_all[blk8], r_all[blk8]
            inter = b8 + m_prev
            m_t = jnp.maximum(inter, b8 + cm_all[blk8])
            b_end = b8[:, end:end + 1]
            g8 = b_end - b8 + i8
            m_new = jnp.maximum(b_end + m_prev, jnp.max(g8, axis=1, keepdims=True))
            a_prev = jnp.exp(b_end + m_prev - m_new)
            packed = jnp.concatenate([b8 - m_t, jnp.exp(inter - m_t), jnp.exp(g8 - m_new), jnp.exp(-m_t),
                                      jnp.zeros((128 - 32, t_), F32)], axis=0)
            pt_ref[d_, cols, :] = packed.T
            r8_ref[d_, :, cols] = r8
            a_b = a_prev[0:1] * masks[0]
            for h in range(1, N_HEADS):
                a_b = a_b + a_prev[h:h + 1] * masks[h]
            ab_ref[d_, c] = a_b
            m_prev = m_new
            yield

    _interleave(gates(False), gates(True))

    def step(reverse, c_idx, h_ref):
        d_ = 1 if reverse else 0
        valid = (col_i >= row_i) if reverse else (col_i <= row_i)
        r0 = pl.multiple_of(c_idx * t_, t_)
        q = q_ref[0, pl.ds(r0, t_), :]
        k = k_ref[0, pl.ds(r0, t_), :] * (HEAD_DIM ** -0.5)
        qb = q.astype(BF16)
        kb = k.astype(BF16)
        vb = v_ref[0, pl.ds(r0, t_), :].astype(BF16)
        pt = pt_ref[d_, pl.ds(r0, t_), :]
        r8 = r8_ref[d_, :, pl.ds(r0, t_)]
        spread = _dot(pt.astype(BF16), eall_ref[...])
        s_cat = []
        for h in range(N_HEADS):
            w_intra = jnp.where(valid, jnp.exp(pt[:, h:h + 1] + r8[h:h + 1, :]), 0.0)
            s_cat.append((_dot_nt(qb * masks_bf[h], kb) * w_intra).astype(BF16))
        v_stack = jnp.concatenate([jnp.concatenate([vb * masks_bf[h] for h in range(N_HEADS)], axis=0),
                                   mones_ref[...]], axis=1)
        acc = _dot(jnp.concatenate(s_cat, axis=1), v_stack)
        yield
        c_prev = c_ref[d_]
        n_prev = n_ref[d_]
        qc = _dot(qb, c_prev.astype(BF16))
        qn = _dot((q * n_prev).astype(BF16), bd_ref[...])
        wi_b = spread[:, 0:W_GROUP]
        num = wi_b * qc + acc[:, 0:W_GROUP]
        den = wi_b * qn + acc[:, W_GROUP:2 * W_GROUP]
        h_ref[pl.ds(r0, t_), :] = num / jnp.maximum(jnp.abs(den), spread[:, 2 * W_GROUP:3 * W_GROUP])
        a_b = ab_ref[d_, c_idx]
        kw = k * spread[:, W_GROUP:2 * W_GROUP]
        c_ref[d_] = a_b * c_prev + bdm * _dot_tn(kw.astype(BF16), vb)
        n_ref[d_] = a_b * n_prev + jnp.sum(kw, axis=0, keepdims=True)

    def body(s, carry):
        _interleave(*[step(rev, _scan_chunk(MLSTM_SCAN_UNROLL * s + u, rev, n_ctx, n_tot), h_ref)
                      for u in range(MLSTM_SCAN_UNROLL) for rev, h_ref in ((False, hf_ref), (True, hb_ref))])
        return carry

    lax.fori_loop(0, n_tot // MLSTM_SCAN_UNROLL, body, 0)

    def finish_chunk(c_idx):
        r0 = pl.multiple_of(c_idx * t_, t_)
        hsum = hf_ref[pl.ds(r0, t_), :] + hb_ref[pl.ds(r0, t_), :]
        hsum = hsum * jax.nn.sigmoid(o_ref[0, pl.ds(r0, t_), :])
        yield
        hn = _head_rms_norm(hsum, bd_ref[...], ng_ref[...])
        out_ref[0, pl.ds(r0, t_), :] = (hn * _silu(sg_ref[0, pl.ds(r0, t_), :])).astype(out_ref.dtype)

    def finish(s, carry):
        _interleave(*[finish_chunk(MLSTM_SCAN_UNROLL * s + u) for u in range(MLSTM_SCAN_UNROLL)])
        return carry

    lax.fori_loop(0, n_tot // MLSTM_SCAN_UNROLL, finish, 0)


def _mlstm(p, gt, gate_b, norm_g, consts, ctx_len):
    b_, lt, _ = p.shape
    t_ = MLSTM_CHUNK
    assert t_ == ROW_TILE
    brow = jnp.pad(gate_b.reshape(4, N_HEADS), ((0, 0), (0, 8 - N_HEADS))).reshape(GATE_ROWS, 1)
    blk = lambda j: pl.BlockSpec((1, lt, W_GROUP), lambda b: (b, 0, j))
    full = lambda shape: pl.BlockSpec(shape, lambda b: (0,) * len(shape))
    kern = functools.partial(_mlstm_kernel, chunk=t_, n_ctx=ctx_len // t_, n_tot=lt // t_)
    return pl.pallas_call(
        kern,
        grid=(b_,),
        in_specs=[blk(MQ), blk(MK), blk(MV), blk(MO), blk(MG),
                  pl.BlockSpec((lt // t_, GATE_ROWS, t_), lambda b: (b, 0, 0)),
                  full((GATE_ROWS, 1)), full((1, W_GROUP)),
                  full((W_GROUP, W_GROUP)), full((N_HEADS * t_, W_GROUP)), full((128, 3 * W_GROUP))],
        out_specs=pl.BlockSpec((1, lt, W_GROUP), lambda b: (b, 0, 0)),
        out_shape=jax.ShapeDtypeStruct((b_, lt, W_GROUP), BF16),
        scratch_shapes=[pltpu.VMEM((lt, W_GROUP), F32), pltpu.VMEM((lt, W_GROUP), F32),
                        pltpu.VMEM((2, W_GROUP, W_GROUP), F32), pltpu.VMEM((2, 1, W_GROUP), F32),
                        pltpu.VMEM((2, lt, 128), F32), pltpu.VMEM((2, 8, lt), F32),
                        pltpu.VMEM((2, lt // t_, 1, W_GROUP), F32)],
        compiler_params=pltpu.CompilerParams(vmem_limit_bytes=VMEM_LIMIT),
        name="mlstm_mixer",
    )(p, p, p, p, p, gt, brow, norm_g.reshape(1, W_GROUP),
      consts["bd"], consts["mones_m"], consts["eall_m"])


def _gla_kernel(q_ref, k_ref, v_ref, sg_ref, sm_ref, wa_ref, ba_ref, ng_ref, tril_ref, triu_ref, bd_ref,
                out_ref, hf_ref, hb_ref, st_ref, qd_ref, kd_ref, kc_ref, eb_ref, *, chunk, n_ctx, n_tot):
    t_ = chunk
    masks_bf = [m.astype(BF16) for m in _head_masks()]
    bdm = _block_diag_mask()
    row_i = lax.broadcasted_iota(jnp.int32, (t_, t_), 0)
    col_i = lax.broadcasted_iota(jnp.int32, (t_, t_), 1)
    st_ref[...] = jnp.zeros_like(st_ref)
    wa_split = [_split2(wa_ref[d_]) for d_ in range(2)]

    def decay(reverse, c_idx):
        d_ = 1 if reverse else 0
        tri_c = triu_ref[...] if reverse else tril_ref[...]
        end_row = 0 if reverse else t_ - 1
        r0 = pl.multiple_of(c_idx * t_, t_)
        sm = sm_ref[0, pl.ds(r0, t_), :]
        smh, sml = _split2(sm)
        wah, wal = wa_split[d_]
        la = _log_sigmoid(_dot(smh, wah) + _dot(smh, wal) + _dot(sml, wah) + ba_ref[d_]) * (1.0 / GLA_NORMALIZER)
        yield
        bcum = _dot_exact_lhs(tri_c, la)
        yield
        q = q_ref[0, pl.ds(r0, t_), :] * (HEAD_DIM ** -0.5)
        k = k_ref[0, pl.ds(r0, t_), :]
        b_end = bcum[end_row:end_row + 1, :]
        qd_ref[d_, pl.ds(r0, t_), :] = (q * jnp.exp(bcum)).astype(BF16)
        kd_ref[d_, pl.ds(r0, t_), :] = (k * jnp.exp(-bcum)).astype(BF16)
        kc_ref[d_, pl.ds(r0, t_), :] = (k * jnp.exp(b_end - bcum)).astype(BF16)
        eb_ref[d_, c_idx] = jnp.exp(b_end)

    def decay_body(s, carry):
        _interleave(*[decay(rev, 2 * s + u) for u in range(2) for rev in (False, True)])
        return carry

    lax.fori_loop(0, n_tot // 2, decay_body, 0)

    def step(reverse, c_idx, h_ref):
        d_ = 1 if reverse else 0
        valid = (col_i >= row_i) if reverse else (col_i <= row_i)
        r0 = pl.multiple_of(c_idx * t_, t_)
        vb = v_ref[0, pl.ds(r0, t_), :].astype(BF16)
        qdb = qd_ref[d_, pl.ds(r0, t_), :]
        kd = kd_ref[d_, pl.ds(r0, t_), :]
        att = [jnp.where(valid, _dot_nt(qdb * masks_bf[h], kd), 0.0).astype(BF16) for h in range(N_HEADS)]
        yield
        st_prev = st_ref[d_]
        v_stack = jnp.concatenate([vb * masks_bf[h] for h in range(N_HEADS)], axis=0)
        h_ref[pl.ds(r0, t_), :] = (_dot_nt(qdb, st_prev.astype(BF16))
                                   + _dot(jnp.concatenate(att, axis=1), v_stack))
        st_ref[d_] = st_prev * eb_ref[d_, c_idx] + bdm * _dot_tn(vb, kc_ref[d_, pl.ds(r0, t_), :])

    def body(s, carry):
        _interleave(*[step(rev, _scan_chunk(GLA_SCAN_UNROLL * s + u, rev, n_ctx, n_tot), h_ref)
                      for u in range(GLA_SCAN_UNROLL) for rev, h_ref in ((False, hf_ref), (True, hb_ref))])
        return carry

    lax.fori_loop(0, n_tot // GLA_SCAN_UNROLL, body, 0)

    def finish_block(i):
        r0 = pl.multiple_of(i * ROW_TILE, ROW_TILE)
        hsum = hf_ref[pl.ds(r0, ROW_TILE), :] + hb_ref[pl.ds(r0, ROW_TILE), :]
        yield
        hn = _head_rms_norm(hsum, bd_ref[...], ng_ref[...])
        out_ref[0, pl.ds(r0, ROW_TILE), :] = (hn * _silu(sg_ref[0, pl.ds(r0, ROW_TILE), :])).astype(out_ref.dtype)

    def finish(s, carry):
        _interleave(*[finish_block(GLA_SCAN_UNROLL * s + u) for u in range(GLA_SCAN_UNROLL)])
        return carry

    lax.fori_loop(0, n_tot * t_ // ROW_TILE // GLA_SCAN_UNROLL, finish, 0)


def _gla(p, w_alpha, b_alpha, norm_g, consts, ctx_len):
    b_, lt, _ = p.shape
    t_ = GLA_CHUNK
    wa = jnp.zeros((2, 128, W_GROUP), F32)
    wa = wa.at[0, 16:32].set(w_alpha[0]).at[1, 32:48].set(w_alpha[1])
    blk = lambda j: pl.BlockSpec((1, lt, W_GROUP), lambda b: (b, 0, j))
    full = lambda shape: pl.BlockSpec(shape, lambda b: (0,) * len(shape))
    kern = functools.partial(_gla_kernel, chunk=t_, n_ctx=ctx_len // t_, n_tot=lt // t_)
    return pl.pallas_call(
        kern,
        grid=(b_,),
        in_specs=[blk(GQ), blk(GK), blk(GV), blk(GG),
                  pl.BlockSpec((1, lt, 128), lambda b: (b, 0, SMALL_BLOCK)),
                  full((2, 128, W_GROUP)), full((2, 1, W_GROUP)), full((1, W_GROUP)),
                  full((t_, t_)), full((t_, t_)), full((W_GROUP, W_GROUP))],
        out_specs=pl.BlockSpec((1, lt, W_GROUP), lambda b: (b, 0, 0)),
        out_shape=jax.ShapeDtypeStruct((b_, lt, W_GROUP), BF16),
        scratch_shapes=[pltpu.VMEM((lt, W_GROUP), F32), pltpu.VMEM((lt, W_GROUP), F32),
                        pltpu.VMEM((2, W_GROUP, W_GROUP), F32),
                        pltpu.VMEM((2, lt, W_GROUP), BF16), pltpu.VMEM((2, lt, W_GROUP), BF16),
                        pltpu.VMEM((2, lt, W_GROUP), BF16), pltpu.VMEM((2, lt // t_, 1, W_GROUP), F32)],
        compiler_params=pltpu.CompilerParams(vmem_limit_bytes=VMEM_LIMIT),
        name="gla_mixer",
    )(p, p, p, p, p, wa, b_alpha.reshape(2, 1, W_GROUP), norm_g.reshape(1, W_GROUP),
      consts["tril_g"], consts["triu_g"], consts["bd"])


def _attn_kernel(q_ref, kv_ref, sg_ref, cos_ref, sin_ref, sink_ref, pswap_ref, out_ref,
                 kb_ref, vb_ref, *, ctx_len, lat_len):
    g = pl.program_id(1)
    lt = ctx_len + lat_len
    n_ctx_blk = ctx_len // ATT_BLOCK
    nwin = 3 * ATT_BLOCK
    masks = _head_masks()
    log2e = math.log2(math.e)
    low_half = lax.broadcasted_iota(jnp.int32, (1, 2 * HEAD_DIM), 1) < HEAD_DIM

    def rope(x, r0, n):
        swapped = _dot(x.astype(BF16), pswap_ref[...])
        return x * cos_ref[pl.ds(r0, n), :] + swapped * sin_ref[pl.ds(r0, n), :]

    @pl.when(g == 0)
    def _():
        def fill_block(i):
            r0 = pl.multiple_of(i * ROW_TILE, ROW_TILE)
            k = kv_ref[0, pl.ds(r0, ROW_TILE), 0:2 * HEAD_DIM]
            v = kv_ref[0, pl.ds(r0, ROW_TILE), 2 * HEAD_DIM:4 * HEAD_DIM]
            k_sw = pltpu.roll(k, HEAD_DIM, 1)
            v_sw = pltpu.roll(v, HEAD_DIM, 1)
            k4 = jnp.concatenate([jnp.where(low_half, k, k_sw), jnp.where(low_half, k_sw, k)], axis=1)
            yield
            kb_ref[pl.ds(r0, ROW_TILE), :] = rope(k4, r0, ROW_TILE).astype(BF16)
            vb_ref[pl.ds(r0, ROW_TILE), :] = jnp.concatenate(
                [jnp.where(low_half, v, 1.0), jnp.where(low_half, v_sw, 1.0)], axis=1).astype(BF16)

        def fill(i, carry):
            _interleave(*[fill_block(ATT_FILL_UNROLL * i + u) for u in range(ATT_FILL_UNROLL)])
            return carry
        lax.fori_loop(0, lt // ROW_TILE // ATT_FILL_UNROLL, fill, 0)

    k_ctx = kb_ref[0:ctx_len, :]
    v_ctx = vb_ref[0:ctx_len, :]
    neg_inf = jnp.float32(-jnp.inf)
    rel0 = (lax.broadcasted_iota(jnp.int32, (ATT_BLOCK, nwin), 1)
            - lax.broadcasted_iota(jnp.int32, (ATT_BLOCK, nwin), 0))

    def block(u):
        j = g * ATT_STEP_BLOCKS + u
        rows = slice(u * ATT_BLOCK, (u + 1) * ATT_BLOCK)
        is_lat = j >= n_ctx_blk
        i_lat = jnp.maximum(j - n_ctx_blk, 0)
        w0 = pl.multiple_of(ctx_len + jnp.clip((i_lat - 1) * ATT_BLOCK, 0, lat_len - nwin), ATT_BLOCK)
        q0 = pl.multiple_of(j * ATT_BLOCK, ATT_BLOCK)
        q = rope(q_ref[0, rows, :], q0, ATT_BLOCK) * (HEAD_DIM ** -0.5 * log2e)
        k_win = kb_ref[pl.ds(w0, nwin), :]
        v_win = vb_ref[pl.ds(w0, nwin), :]
        band = is_lat & (jnp.abs(rel0 + (w0 - q0)) <= WINDOW)
        q_stack = jnp.concatenate([(q * masks[h]).astype(BF16) for h in range(N_HEADS)], axis=0)
        band4 = jnp.concatenate([band] * N_HEADS, axis=0)
        sink = jnp.concatenate([jnp.broadcast_to(sink_ref[h] * log2e, (ATT_BLOCK, 128)) for h in range(N_HEADS)],
                               axis=0)
        s_loc = jnp.where(band4, _dot_nt(q_stack, k_win), neg_inf)
        s_ctx = _dot_nt(q_stack, k_ctx)
        yield
        m = jnp.maximum(jnp.max(s_loc, axis=-1, keepdims=True), jnp.max(s_ctx, axis=-1, keepdims=True))
        m128 = jnp.maximum(jnp.broadcast_to(m, (N_HEADS * ATT_BLOCK, 128)), sink)
        p_loc = jnp.exp2(s_loc - jnp.concatenate([m128] * (nwin // 128), axis=1)).astype(BF16)
        p_ctx = jnp.exp2(s_ctx - jnp.concatenate([m128] * (ctx_len // 128), axis=1)).astype(BF16)
        yield
        o_all = _dot(p_loc, v_win) + _dot(p_ctx, v_ctx)
        e_sink = jnp.exp2(sink - m128)
        o = jnp.zeros((ATT_BLOCK, W_GROUP), F32)
        for h in range(N_HEADS):
            hr = slice(h * ATT_BLOCK, (h + 1) * ATT_BLOCK)
            if h % 2 == 0:
                num, den = o_all[hr], pltpu.roll(o_all[hr], W_GROUP - HEAD_DIM, 1)
            else:
                num, den = pltpu.roll(o_all[hr], HEAD_DIM, 1), o_all[hr]
            den = den + jnp.concatenate([e_sink[hr], e_sink[hr]], axis=1)
            o = jnp.where(masks[h] > 0.0, num / den, o)
        out_ref[0, rows, :] = (o * _silu(sg_ref[0, rows, :])).astype(out_ref.dtype)

    _interleave(*[block(u) for u in range(ATT_STEP_BLOCKS)])


def _attn(p, sink, consts, ctx_len):
    b_, lt, _ = p.shape
    step_rows = ATT_STEP_BLOCKS * ATT_BLOCK
    nb = lt // step_rows
    kern = functools.partial(_attn_kernel, ctx_len=ctx_len, lat_len=lt - ctx_len)
    return pl.pallas_call(
        kern,
        grid=(b_, nb),
        in_specs=[pl.BlockSpec((1, step_rows, W_GROUP), lambda b, j: (b, j, AQ)),
                  pl.BlockSpec((1, lt, W_GROUP), lambda b, j: (b, 0, AKV)),
                  pl.BlockSpec((1, step_rows, W_GROUP), lambda b, j: (b, j, AG)),
                  pl.BlockSpec((lt, W_GROUP), lambda b, j: (0, 0)),
                  pl.BlockSpec((lt, W_GROUP), lambda b, j: (0, 0)),
                  pl.BlockSpec((N_HEADS, 1, 1), lambda b, j: (0, 0, 0)),
                  pl.BlockSpec((W_GROUP, W_GROUP), lambda b, j: (0, 0))],
        out_specs=pl.BlockSpec((1, step_rows, W_GROUP), lambda b, j: (b, j, 0)),
        out_shape=jax.ShapeDtypeStruct((b_, lt, W_GROUP), BF16),
        scratch_shapes=[pltpu.VMEM((lt, W_GROUP), BF16), pltpu.VMEM((lt, W_GROUP), BF16)],
        compiler_params=pltpu.CompilerParams(vmem_limit_bytes=VMEM_LIMIT,
                                             dimension_semantics=("arbitrary", "arbitrary")),
        name="window_attention",
    )(p, p, p, consts["rope_cos"], consts["rope_sin"], sink.reshape(N_HEADS, 1, 1), consts["pswap"])


class _FftPlan:
    def __init__(self, n1):
        self.n1 = n1
        self.n = n1 * FFT_N2
        self.k1p = -(-(n1 // 2 + 1) // 8) * 8


def _fft_constants(n1, n_in, n_out):
    plan = _FftPlan(n1)
    n, k1p, half = plan.n, plan.k1p, n1 // 2
    k1 = np.arange(k1p)[:, None].astype(np.float64)
    live = (np.arange(k1p) <= half)[:, None]
    i1 = np.arange(n1)[None, :].astype(np.float64)
    ang = 2.0 * np.pi * k1 * i1 / n1
    f1 = np.concatenate([np.where(live, np.cos(ang), 0.0), np.where(live, -np.sin(ang), 0.0)], axis=0)
    i2 = np.arange(FFT_N2)[None, :].astype(np.float64)
    phi = 2.0 * np.pi * k1 * i2 / n
    tw = np.stack([np.cos(phi), -np.sin(phi)], axis=0)
    tw = np.broadcast_to(tw.reshape(2, k1p * FFT_N2, 1), (2, k1p * FFT_N2, 128))
    th = 2.0 * np.pi * np.outer(np.arange(FFT_N2), np.arange(FFT_N2)) / FFT_N2
    c2, s2 = np.cos(th), np.sin(th)
    m_fwd = np.block([[c2, s2], [-s2, c2]])
    m_inv = np.block([[c2, -s2], [s2, c2]])
    wk = np.where(np.arange(k1p) <= half, 2.0, 0.0)
    wk[0] = 1.0
    wk[half] = 1.0
    psi = 2.0 * np.pi * np.outer(np.arange(n1), np.arange(k1p)) / n1
    g = np.concatenate([np.cos(psi) * wk[None, :], -np.sin(psi) * wk[None, :]], axis=1) / n

    def hl(a):
        a32 = jnp.asarray(a, F32)
        hi = a32.astype(BF16)
        lo = (a32 - hi.astype(F32)).astype(BF16)
        return hi, lo

    return dict(f1=hl(f1), f1_in=hl(f1[:, 0:n_in]), g=hl(g[0:n_out]), m_fwd=hl(m_fwd), m_inv=hl(m_inv),
                tw=jnp.asarray(tw, F32))


def _slab_row(k):
    return k * FFT_PITCH if isinstance(k, int) else pl.multiple_of(k * FFT_PITCH, 8)


def _load_slab(ref, k):
    r0 = _slab_row(k)
    return jnp.concatenate([ref[0, pl.ds(r0, FFT_N2), :], ref[1, pl.ds(r0, FFT_N2), :]], axis=1)


def _store_slab(ref, k, val):
    r0 = _slab_row(k)
    ref[0, pl.ds(r0, FFT_N2), :] = val[:, 0:128]
    ref[1, pl.ds(r0, FFT_N2), :] = val[:, 128:256]


FFT_GROUP = 16


def _cross_slab(src_refs, n_src, mat_h, mat_l, dst_refs, n_dst, dot=_dot_const):
    def body(gidx, carry):
        cols = []
        for i in range(FFT_GROUP):
            i2 = gidx * FFT_GROUP + i
            parts = []
            for half in range(2):
                parts.append(jnp.concatenate(
                    [r[half, pl.ds(i2, n_src, stride=FFT_PITCH), :] for r in src_refs], axis=0))
            cols.append(jnp.concatenate(parts, axis=1))
        x = jnp.concatenate(cols, axis=1)
        y = dot(mat_h, mat_l, x)
        for i in range(FFT_GROUP):
            i2 = gidx * FFT_GROUP + i
            for half in range(2):
                c0 = (2 * i + half) * 128
                for d, ref in enumerate(dst_refs):
                    ref[half, pl.ds(i2, n_dst, stride=FFT_PITCH), :] = y[d * n_dst:(d + 1) * n_dst, c0:c0 + 128]
        return carry

    lax.fori_loop(0, FFT_N2 // FFT_GROUP, body, 0)


def _twiddle(ar, ai, tw_ref, k, conj):
    r0 = pl.multiple_of(k * FFT_N2, FFT_N2)
    tr = tw_ref[0, pl.ds(r0, FFT_N2), :]
    ti = tw_ref[1, pl.ds(r0, FFT_N2), :]
    tr = jnp.concatenate([tr, tr], axis=1)
    ti = jnp.concatenate([ti, ti], axis=1)
    if conj:
        return ar * tr + ai * ti, ai * tr - ar * ti
    return ar * tr - ai * ti, ar * ti + ai * tr


def _filter_kernel(feat_ref, dec_ref, w1_ref, b1_ref, w2_ref, b2_ref, w3_ref, b3_ref, fq_ref,
                   f1h_ref, f1l_ref, mh_ref, ml_ref, tw_ref, out_ref, zbuf, are, aim, *, n1, k1p):
    half = n1 // 2

    u_ = FFT_SLAB_UNROLL
    rows = u_ * FFT_N2

    def fill_block(g):
        r0 = pl.multiple_of(g * rows, rows)
        z = feat_ref[:, pl.ds(r0, rows)]
        h = jnp.sin(fq_ref[0, 0] * (_dot_f32_tn(w1_ref[0], z) + b1_ref[0]))
        yield
        h = jnp.sin(fq_ref[0, 1] * (_dot_f32_tn(w2_ref[0], h) + b2_ref[0]))
        yield
        h = _dot_f32_tn(h, w3_ref[0]) + b3_ref[0]
        hsel = jnp.where(g * u_ < half, h[:, 0:W_GROUP], h[:, W_GROUP:2 * W_GROUP])
        kern = hsel * dec_ref[pl.ds(r0, rows), :]
        for u in range(u_):
            _store_slab(zbuf, g * u_ + u, kern[u * FFT_N2:(u + 1) * FFT_N2])

    def fill(s, carry):
        _interleave(fill_block(2 * s), fill_block(2 * s + 1))
        return carry

    lax.fori_loop(0, n1 // u_ // 2, fill, 0)
    _cross_slab([zbuf], n1, f1h_ref[...], f1l_ref[...], [are, aim], k1p)

    def slab(g, carry):
        cols = []
        for u in range(u_):
            k = g * u_ + u
            ar, ai = _twiddle(_load_slab(are, k), _load_slab(aim, k), tw_ref, k, conj=False)
            cols.append(jnp.concatenate([ar, ai], axis=0))
        x = _dot_const(mh_ref[...], ml_ref[...], jnp.concatenate(cols, axis=1))
        for u in range(u_):
            r0 = pl.multiple_of((g * u_ + u) * FFT_N2, FFT_N2)
            out_ref[0, 0, pl.ds(r0, FFT_N2), :] = x[0:FFT_N2, u * W_GROUP:(u + 1) * W_GROUP]
            out_ref[0, 1, pl.ds(r0, FFT_N2), :] = x[FFT_N2:2 * FFT_N2, u * W_GROUP:(u + 1) * W_GROUP]
        return carry

    lax.fori_loop(0, k1p // u_, slab, 0)


def _filter_tables(l_):
    pos = np.concatenate([np.arange(l_), l_ - np.arange(l_)]).astype(np.float32)
    pos[l_] = 0.0
    t = pos / np.float32(max(l_ - 1, 1))
    w = np.float32(2.0 * math.pi) * pos / np.float32(l_)
    f = np.linspace(1e-4, HYENA_BANDS - 1, HYENA_BANDS, dtype=np.float32)
    feat = np.zeros((2 * l_, 64), np.float32)
    feat[:, 0] = t
    feat[:, 1:1 + HYENA_BANDS] = np.cos(w[:, None] * f)
    feat[:, 1 + HYENA_BANDS:HYENA_EMB] = -np.sin(w[:, None] * f)
    deltas = np.abs(np.linspace(HYENA_MIN_DECAY, HYENA_MAX_DECAY, W_GROUP, dtype=np.float32))
    dec = np.exp(-t[:, None] * deltas[None, :]).astype(np.float32)
    dec[l_] = 0.0
    return jnp.asarray(feat), jnp.asarray(dec)


def _filter_spectra(l_, w1p, b1, w2, b2, w3, b3, freq, fc):
    n1 = 2 * l_ // FFT_N2
    plan = _FftPlan(n1)
    k1p = plan.k1p
    feat, dec = _filter_tables(l_)
    full = lambda shape: pl.BlockSpec(shape, lambda l: (0,) * len(shape))
    per = lambda shape: pl.BlockSpec((1,) + shape, lambda l: (l,) + (0,) * len(shape))
    kern = functools.partial(_filter_kernel, n1=n1, k1p=k1p)
    return pl.pallas_call(
        kern,
        grid=(DEPTH,),
        in_specs=[full((64, 2 * l_)), full((2 * l_, W_GROUP)),
                  per((64, HYENA_FFN)), per((HYENA_FFN, 1)), per((HYENA_FFN, HYENA_FFN)), per((HYENA_FFN, 1)),
                  per((HYENA_FFN, 2 * W_GROUP)), per((1, 2 * W_GROUP)), per((2, HYENA_FFN, 1)),
                  full((2 * k1p, n1)), full((2 * k1p, n1)), full((128, 128)), full((128, 128)),
                  full((2, k1p * FFT_N2, 128))],
        out_specs=pl.BlockSpec((1, 2, k1p * FFT_N2, W_GROUP), lambda l: (l, 0, 0, 0)),
        out_shape=jax.ShapeDtypeStruct((DEPTH, 2, k1p * FFT_N2, W_GROUP), F32),
        scratch_shapes=[pltpu.VMEM((2, n1 * FFT_PITCH, 128), F32),
                        pltpu.VMEM((2, k1p * FFT_PITCH, 128), F32),
                        pltpu.VMEM((2, k1p * FFT_PITCH, 128), F32)],
        compiler_params=pltpu.CompilerParams(vmem_limit_bytes=VMEM_LIMIT),
        name="hyena_filter_spectrum_%d" % l_,
    )(feat.T, dec, w1p, b1.reshape(DEPTH, HYENA_FFN, 1), w2, b2.reshape(DEPTH, HYENA_FFN, 1),
      w3, b3.reshape(DEPTH, 1, 2 * W_GROUP), freq.reshape(DEPTH, 2, HYENA_FFN, 1),
      fc["f1"][0], fc["f1"][1], fc["m_fwd"][0], fc["m_fwd"][1], fc["tw"])


def _short_dft_constants(l_):
    n = 2 * l_
    kp = -(-(l_ + 1) // 8) * 8
    k = np.arange(kp)[:, None].astype(np.float64)
    live = (np.arange(kp) <= l_)[:, None]
    pos = np.arange(n)[None, :].astype(np.float64)
    ang = 2.0 * np.pi * k * pos / n
    flt = np.concatenate([np.where(live, np.cos(ang), 0.0), np.where(live, -np.sin(ang), 0.0)], axis=0)
    wk = np.where(np.arange(kp) <= l_, 2.0, 0.0)
    wk[0] = 1.0
    wk[l_] = 1.0
    psi = 2.0 * np.pi * np.outer(np.arange(l_), np.arange(kp)) / n
    inv = np.concatenate([np.cos(psi) * wk[None, :], -np.sin(psi) * wk[None, :]], axis=1) / n

    def hl(a):
        a32 = jnp.asarray(a, F32)
        hi = a32.astype(BF16)
        return jnp.stack([hi, (a32 - hi.astype(F32)).astype(BF16)], axis=0)

    return dict(fwd=hl(flt[:, 0:l_]), flt=hl(flt), inv=hl(inv), kp=kp)


def _short_filter_kernel(feat_ref, dec_ref, w1_ref, b1_ref, w2_ref, b2_ref, w3_ref, b3_ref, fq_ref, flt_ref,
                         out_ref, *, l_):
    kp = out_ref.shape[2]
    halves = []
    for part in range(2):
        z = feat_ref[part * l_:(part + 1) * l_, :]
        h = jnp.sin(fq_ref[0, 0:1, :] * (_dot_f32(z, w1_ref[0]) + b1_ref[0]))
        h = jnp.sin(fq_ref[0, 1:2, :] * (_dot_f32(h, w2_ref[0]) + b2_ref[0]))
        h = _dot_f32(h, w3_ref[0]) + b3_ref[0]
        halves.append(h[:, part * W_GROUP:(part + 1) * W_GROUP] * dec_ref[part * l_:(part + 1) * l_, :])
    spec = _dot_const(flt_ref[0], flt_ref[1], jnp.concatenate(halves, axis=0))
    out_ref[0, 0] = spec[0:kp]
    out_ref[0, 1] = spec[kp:2 * kp]


def _short_filter_spectra(l_, w1p, b1, w2, b2, w3, b3, freq, sc):
    kp = sc["kp"]
    feat, dec = _filter_tables(l_)
    full = lambda shape: pl.BlockSpec(shape, lambda l: (0,) * len(shape))
    per = lambda shape: pl.BlockSpec((1,) + shape, lambda l: (l,) + (0,) * len(shape))
    return pl.pallas_call(
        functools.partial(_short_filter_kernel, l_=l_),
        grid=(DEPTH,),
        in_specs=[full((2 * l_, 64)), full((2 * l_, W_GROUP)),
                  per((64, HYENA_FFN)), per((1, HYENA_FFN)), per((HYENA_FFN, HYENA_FFN)), per((1, HYENA_FFN)),
                  per((HYENA_FFN, 2 * W_GROUP)), per((1, 2 * W_GROUP)), per((2, HYENA_FFN)),
                  full((2, 2 * kp, 2 * l_))],
        out_specs=pl.BlockSpec((1, 2, kp, W_GROUP), lambda l: (l, 0, 0, 0)),
        out_shape=jax.ShapeDtypeStruct((DEPTH, 2, kp, W_GROUP), F32),
        compiler_params=pltpu.CompilerParams(vmem_limit_bytes=VMEM_LIMIT),
        name="hyena_filter_spectrum_%d" % l_,
    )(feat, dec, w1p, b1.reshape(DEPTH, 1, HYENA_FFN), w2, b2.reshape(DEPTH, 1, HYENA_FFN),
      w3, b3.reshape(DEPTH, 1, 2 * W_GROUP), freq, sc["flt"])


def _hyena_segment(v_ref, x1_ref, x0_ref, sg_ref, cw_ref, cb_ref, d_ref, kf_ref, consts, scratch, out_ref,
                   *, row0, seg_len, n1):
    f1h, f1l, gh, gl, mfh, mfl, mih, mil, tw_ref = consts
    zbuf, are, aim, ybuf = scratch
    n_sig = seg_len // FFT_N2
    n_in = max(n_sig, 8)
    n_out = max(n_sig, 8)
    k1p = _FftPlan(n1).k1p
    row_i = lax.broadcasted_iota(jnp.int32, (FFT_N2, W_GROUP), 0)

    def conv3(ref, blk, jslab):
        r = pl.multiple_of(row0 + jslab * FFT_N2, FFT_N2)
        cur = ref[0, pl.ds(r, FFT_N2), :]
        pe = ref[0, pl.ds(jnp.maximum(r - 1, row0), 1), :] * jnp.where(jslab > 0, 1.0, 0.0)
        ne = ref[0, pl.ds(jnp.minimum(r + FFT_N2, row0 + seg_len - 1), 1), :] * jnp.where(jslab < n_sig - 1, 1.0, 0.0)
        prev = jnp.where(row_i == 0, pe, pltpu.roll(cur, 1, 0))
        nxt = jnp.where(row_i == FFT_N2 - 1, ne, pltpu.roll(cur, FFT_N2 - 1, 0))
        c0 = blk * W_GROUP
        return (cw_ref[0:1, c0:c0 + W_GROUP] * prev + cw_ref[1:2, c0:c0 + W_GROUP] * cur
                + cw_ref[2:3, c0:c0 + W_GROUP] * nxt + cb_ref[:, c0:c0 + W_GROUP])

    def pre(jslab, carry):
        _store_slab(zbuf, jslab, conv3(v_ref, 0, jslab) * conv3(x1_ref, 1, jslab))
        return carry

    lax.fori_loop(0, n_sig, pre, 0)
    for jz in range(n_sig, n_in):
        _store_slab(zbuf, jz, jnp.zeros((FFT_N2, W_GROUP), F32))

    _cross_slab([zbuf], n_in, f1h, f1l, [are, aim], k1p, dot=_dot_const2)

    u_ = FFT_SLAB_UNROLL

    def slab(g, carry):
        cols = []
        for u in range(u_):
            k = g * u_ + u
            ar, ai = _twiddle(_load_slab(are, k), _load_slab(aim, k), tw_ref, k, conj=False)
            cols.append(jnp.concatenate([ar, ai], axis=0))
        x = _dot_const2(mfh, mfl, jnp.concatenate(cols, axis=1))
        cols = []
        for u in range(u_):
            xr = x[0:FFT_N2, u * W_GROUP:(u + 1) * W_GROUP]
            xi = x[FFT_N2:2 * FFT_N2, u * W_GROUP:(u + 1) * W_GROUP]
            r0 = pl.multiple_of((g * u_ + u) * FFT_N2, FFT_N2)
            kr = kf_ref[0, 0, pl.ds(r0, FFT_N2), :]
            ki = kf_ref[0, 1, pl.ds(r0, FFT_N2), :]
            cols.append(jnp.concatenate([xr * kr - xi * ki, xr * ki + xi * kr], axis=0))
        bm = _dot_const2(mih, mil, jnp.concatenate(cols, axis=1))
        for u in range(u_):
            k = g * u_ + u
            br, bi = _twiddle(bm[0:FFT_N2, u * W_GROUP:(u + 1) * W_GROUP],
                              bm[FFT_N2:2 * FFT_N2, u * W_GROUP:(u + 1) * W_GROUP], tw_ref, k, conj=True)
            _store_slab(are, k, br)
            _store_slab(aim, k, bi)
        return carry

    lax.fori_loop(0, k1p // u_, slab, 0)
    _cross_slab([are, aim], k1p, gh, gl, [ybuf], n_out, dot=_dot_const2)

    def post(jslab, carry):
        r = pl.multiple_of(row0 + jslab * FFT_N2, FFT_N2)
        y = _load_slab(ybuf, jslab) + _load_slab(zbuf, jslab) * d_ref[...]
        out_ref[0, pl.ds(r, FFT_N2), :] = (conv3(x0_ref, 2, jslab) * y
                                           * _silu(sg_ref[0, pl.ds(r, FFT_N2), :])).astype(out_ref.dtype)
        return carry

    lax.fori_loop(0, n_sig, post, 0)


def _hyena_short_segment(v_ref, x1_ref, x0_ref, sg_ref, cw_ref, cb_ref, d_ref, kf_ref, fwd_ref, inv_ref,
                         zbuf, ybuf, out_ref, *, seg_len):
    n_sig = seg_len // FFT_N2
    kp = kf_ref.shape[2]
    row_i = lax.broadcasted_iota(jnp.int32, (FFT_N2, W_GROUP), 0)

    def conv3(ref, blk, jslab):
        r = pl.multiple_of(jslab * FFT_N2, FFT_N2)
        cur = ref[0, pl.ds(r, FFT_N2), :]
        pe = ref[0, pl.ds(jnp.maximum(r - 1, 0), 1), :] * jnp.where(jslab > 0, 1.0, 0.0)
        ne = ref[0, pl.ds(jnp.minimum(r + FFT_N2, seg_len - 1), 1), :] * jnp.where(jslab < n_sig - 1, 1.0, 0.0)
        prev = jnp.where(row_i == 0, pe, pltpu.roll(cur, 1, 0))
        nxt = jnp.where(row_i == FFT_N2 - 1, ne, pltpu.roll(cur, FFT_N2 - 1, 0))
        c0 = blk * W_GROUP
        return (cw_ref[0:1, c0:c0 + W_GROUP] * prev + cw_ref[1:2, c0:c0 + W_GROUP] * cur
                + cw_ref[2:3, c0:c0 + W_GROUP] * nxt + cb_ref[:, c0:c0 + W_GROUP])

    def pre(jslab, carry):
        r = pl.multiple_of(jslab * FFT_N2, FFT_N2)
        zbuf[pl.ds(r, FFT_N2), :] = conv3(v_ref, 0, jslab) * conv3(x1_ref, 1, jslab)
        return carry

    lax.fori_loop(0, n_sig, pre, 0)
    x = _dot_const2(fwd_ref[0], fwd_ref[1], zbuf[...])
    xr, xi = x[0:kp], x[kp:2 * kp]
    kr, ki = kf_ref[0, 0], kf_ref[0, 1]
    y = jnp.concatenate([xr * kr - xi * ki, xr * ki + xi * kr], axis=0)
    ybuf[...] = _dot_const2(inv_ref[0], inv_ref[1], y)

    def post(jslab, carry):
        r = pl.multiple_of(jslab * FFT_N2, FFT_N2)
        y_ = ybuf[pl.ds(r, FFT_N2), :] + zbuf[pl.ds(r, FFT_N2), :] * d_ref[...]
        out_ref[0, pl.ds(r, FFT_N2), :] = (conv3(x0_ref, 2, jslab) * y_
                                           * _silu(sg_ref[0, pl.ds(r, FFT_N2), :])).astype(out_ref.dtype)
        return carry

    lax.fori_loop(0, n_sig, post, 0)


def _hyena_kernel(v_ref, x1_ref, x0_ref, sg_ref, cw_ref, cb_ref, d_ref, kfc_ref, kfl_ref,
                  c_fwd, c_inv, l_f1h, l_f1l, l_gh, l_gl, l_tw,
                  mfh_ref, mfl_ref, mih_ref, mil_ref, out_ref,
                  zc, yc, zl, arl, ail, yl, *, ctx_len, lat_len):
    mats = (mfh_ref[...], mfl_ref[...], mih_ref[...], mil_ref[...])
    _hyena_short_segment(v_ref, x1_ref, x0_ref, sg_ref, cw_ref, cb_ref, d_ref, kfc_ref, c_fwd, c_inv,
                         zc, yc, out_ref, seg_len=ctx_len)
    _hyena_segment(v_ref, x1_ref, x0_ref, sg_ref, cw_ref, cb_ref, d_ref,
                   kfl_ref, (l_f1h[...], l_f1l[...], l_gh[...], l_gl[...]) + mats + (l_tw,), (zl, arl, ail, yl), out_ref,
                   row0=ctx_len, seg_len=lat_len, n1=2 * lat_len // FFT_N2)


def _hyena(p, conv_w, conv_b, d, kf_ctx, kf_lat, layer, fc_ctx, fc_lat, ctx_len):
    b_, lt, _ = p.shape
    lat_len = lt - ctx_len
    n1l = 2 * lat_len // FFT_N2
    kl = _FftPlan(n1l).k1p
    blk = lambda j: pl.BlockSpec((1, lt, W_GROUP), lambda b: (b, 0, j))
    full = lambda a: pl.BlockSpec(a.shape, lambda b: (0,) * a.ndim)
    per_layer = lambda a: pl.BlockSpec((1,) + a.shape[1:], lambda b: (layer,) + (0,) * (a.ndim - 1))
    cst = [fc_ctx["fwd"], fc_ctx["inv"],
           fc_lat["f1_in"][0], fc_lat["f1_in"][1], fc_lat["g"][0], fc_lat["g"][1], fc_lat["tw"],
           fc_lat["m_fwd"][0], fc_lat["m_fwd"][1], fc_lat["m_inv"][0], fc_lat["m_inv"][1]]
    small = [conv_w, conv_b.reshape(1, 3 * W_GROUP), d.reshape(1, W_GROUP)]
    sbuf = lambda n: pltpu.VMEM((2, n * FFT_PITCH, 128), F32)
    kern = functools.partial(_hyena_kernel, ctx_len=ctx_len, lat_len=lat_len)
    return pl.pallas_call(
        kern,
        grid=(b_,),
        in_specs=([blk(HV), blk(HX1), blk(HX0), blk(HG)] + [full(a) for a in small]
                  + [per_layer(kf_ctx), per_layer(kf_lat)] + [full(a) for a in cst]),
        out_specs=pl.BlockSpec((1, lt, W_GROUP), lambda b: (b, 0, 0)),
        out_shape=jax.ShapeDtypeStruct((b_, lt, W_GROUP), BF16),
        scratch_shapes=[pltpu.VMEM((ctx_len, W_GROUP), F32), pltpu.VMEM((ctx_len, W_GROUP), F32),
                        sbuf(n1l // 2), sbuf(kl), sbuf(kl), sbuf(n1l // 2)],
        compiler_params=pltpu.CompilerParams(vmem_limit_bytes=VMEM_LIMIT),
        name="hyena_mixer",
    )(p, p, p, p, *small, kf_ctx, kf_lat, *cst)


def _rope_tables(ctx_len, lat_len):
    pos = np.arange(lat_len)
    row = (pos // GRID_W).astype(np.float32)
    col = (pos % GRID_W).astype(np.float32)
    inv = (np.float32(ROPE_BASE) ** (-np.arange(ROPE_AXIS_FREQS, dtype=np.float32) / np.float32(ROPE_AXIS_FREQS)))
    ang = np.concatenate([row[:, None] * inv, col[:, None] * inv], axis=-1).astype(np.float32)
    cos, sin = np.cos(ang), np.sin(ang)
    cos_h = np.concatenate([cos, cos], axis=-1)
    sin_h = np.concatenate([-sin, sin], axis=-1)
    cos_l = np.tile(cos_h, (1, N_HEADS))
    sin_l = np.tile(sin_h, (1, N_HEADS))
    cos_t = np.concatenate([np.ones((ctx_len, W_GROUP), np.float32), cos_l], axis=0)
    sin_t = np.concatenate([np.zeros((ctx_len, W_GROUP), np.float32), sin_l], axis=0)
    return jnp.asarray(cos_t, F32), jnp.asarray(sin_t, F32)


def _rope_partner():
    d = np.arange(W_GROUP)
    partner = np.where(d % HEAD_DIM < HEAD_DIM // 2, d + HEAD_DIM // 2, d - HEAD_DIM // 2)
    p = np.zeros((W_GROUP, W_GROUP), np.float32)
    p[partner, d] = 1.0
    return jnp.asarray(p, BF16)


def _mlstm_expansion():
    e = np.zeros((128, 3 * W_GROUP), np.float32)
    for j in range(3):
        for h in range(N_HEADS):
            e[8 * (j + 1) + h, j * W_GROUP + h * HEAD_DIM:j * W_GROUP + (h + 1) * HEAD_DIM] = 1.0
    return jnp.asarray(e, BF16)


def _tri(n):
    t = np.tril(np.ones((n, n), np.float32))
    return jnp.asarray(t, BF16), jnp.asarray(t.T, BF16)


def _w_layout_kernel(w_ref, o_ref, ot_ref):
    h0 = 5 * W_GROUP + 16
    g0 = h0 + 4 * W_GROUP
    d0 = g0 + 4 * W_GROUP + 2 * GLA_RANK
    w = w_ref[0]
    zero4 = jnp.zeros((8 - N_HEADS, w.shape[1]), F32)
    ot_ref[0] = jnp.concatenate(
        [piece for g in range(4)
         for piece in (w[5 * W_GROUP + N_HEADS * g:5 * W_GROUP + N_HEADS * (g + 1)], zero4)], axis=0).astype(BF16)
    o_ref[0] = jnp.concatenate([
        w[0:5 * W_GROUP],
        w[h0:h0 + 4 * W_GROUP],
        w[g0:g0 + 4 * W_GROUP],
        w[d0:d0 + 3 * W_GROUP],
        w[5 * W_GROUP:5 * W_GROUP + 16],
        w[g0 + 4 * W_GROUP:g0 + 4 * W_GROUP + 2 * GLA_RANK],
        jnp.zeros((128 - 48, w.shape[1]), F32)], axis=0).T.astype(BF16)


def _layout_w_in(w_in):
    n_in = w_in.shape[-1]
    return pl.pallas_call(
        _w_layout_kernel,
        grid=(DEPTH, D_MODEL // ROW_TILE),
        in_specs=[pl.BlockSpec((1, n_in, ROW_TILE), lambda l, i: (l, 0, i))],
        out_specs=[pl.BlockSpec((1, ROW_TILE, N_PROJ), lambda l, i: (l, i, 0)),
                   pl.BlockSpec((1, GATE_ROWS, ROW_TILE), lambda l, i: (l, 0, i))],
        out_shape=[jax.ShapeDtypeStruct((DEPTH, D_MODEL, N_PROJ), BF16),
                   jax.ShapeDtypeStruct((DEPTH, GATE_ROWS, D_MODEL), BF16)],
        compiler_params=pltpu.CompilerParams(vmem_limit_bytes=VMEM_LIMIT),
        name="inproj_weight_layout",
    )(jnp.swapaxes(w_in, 1, 2))


def kernel(x, c, ctx, c_ctx, w_ada, b_ada, g_pre, g_post, w_in, mlstm_gate_b, mlstm_norm_g, hyena_conv_w,
           hyena_conv_b, hyena_w1, hyena_b1, hyena_w2, hyena_b2, hyena_w3, hyena_b3, hyena_freq, hyena_d,
           gla_w_alpha, gla_b_alpha, gla_norm_g, attn_sink, w_out):
    b_, lat_len, _ = x.shape
    ctx_len = ctx.shape[1]
    n_ctx_tiles = ctx_len // ROW_TILE

    tril_m, triu_m = _tri(MLSTM_CHUNK)
    tril_g, triu_g = _tri(GLA_CHUNK)
    bd = np.kron(np.eye(N_HEADS, dtype=np.float32), np.ones((HEAD_DIM, HEAD_DIM), np.float32))
    rope_cos, rope_sin = _rope_tables(ctx_len, lat_len)
    mones_m = np.kron(np.eye(N_HEADS, dtype=np.float32), np.ones((MLSTM_CHUNK, HEAD_DIM), np.float32))
    consts = dict(tril_m=tril_m, triu_m=triu_m, tril_g=tril_g, triu_g=triu_g, bd=jnp.asarray(bd, BF16),
                  mones_m=jnp.asarray(mones_m, BF16), eall_m=_mlstm_expansion(),
                  rope_cos=rope_cos, rope_sin=rope_sin, pswap=_rope_partner())
    fc_ctx = _short_dft_constants(ctx_len)
    fc_lat = _fft_constants(2 * lat_len // FFT_N2, lat_len // FFT_N2, lat_len // FFT_N2)

    cc = jnp.zeros((16, D_MODEL), F32).at[0:b_].set(c).at[b_].set(c_ctx)
    mod = _modulation(cc, w_ada, b_ada)
    mod_ctx = jnp.broadcast_to(mod[:, b_:b_ + 1], (DEPTH, b_, 3 * D_MODEL))
    mod_all = jnp.stack([mod_ctx, mod[:, 0:b_]], axis=2).reshape(DEPTH, 2 * b_, 1, 3 * D_MODEL)

    w_p, w_st = _layout_w_in(w_in)
    w_o = w_out.astype(BF16)
    w1p = jnp.zeros((DEPTH, 64, HYENA_FFN), F32).at[:, 0:HYENA_EMB].set(hyena_w1)
    filt = (w1p, hyena_b1, hyena_w2, hyena_b2, hyena_w3, hyena_b3, hyena_freq)
    kf_ctx = _short_filter_spectra(ctx_len, *filt, fc_ctx)
    kf_lat = _filter_spectra(lat_len, *filt, fc_lat)

    xs = (ctx, x)
    _, p, gt = _proj(None, xs, None, 0, w_o, None, g_pre[0], mod_all, w_p, w_st, n_ctx_tiles)
    for l in range(DEPTH):
        a = _mlstm(p, gt, mlstm_gate_b[l], mlstm_norm_g[l], consts, ctx_len)
        hy = _hyena(p, hyena_conv_w[l], hyena_conv_b[l], hyena_d[l], kf_ctx, kf_lat, l, fc_ctx, fc_lat, ctx_len)
        g = _gla(p, gla_w_alpha[l], gla_b_alpha[l], gla_norm_g[l], consts, ctx_len)
        d = _attn(p, attn_sink[l], consts, ctx_len)
        if l < DEPTH - 1:
            xs, p, gt = _proj((a, hy, g, d), xs, l, l + 1, w_o, g_post[l], g_pre[l + 1], mod_all, w_p, w_st,
                              n_ctx_tiles)
    return _outproj((a, hy, g, d), w_o, DEPTH - 1, g_post[DEPTH - 1], mod_all, xs, n_ctx_tiles, skip_ctx=True)
```
